```python
import jax, jax.numpy as jnp
from jax import lax
import numpy as np

D_MODEL = 2048
BATCH = 8
SEQ = 4096
DEPTH = 1
DEC_BATCH = 16
DEC_SEQ = 2048
PAST_LEN = 128

N_MEM = 256
EPS = 1e-5
ATT_HEADS = 16
ATT_KV_HEADS = 4
ATT_HEAD_DIM = 64
ATT_WIDTH = ATT_HEADS * ATT_HEAD_DIM
KV_WIDTH = ATT_KV_HEADS * ATT_HEAD_DIM
WINDOW = 128
ATT_BLOCK = 128
ROPE_THETA = 10000.0
HG_HEADS = 8
HG_KEY_DIM = 128
HG_VAL_DIM = 128
HG_KEY = HG_HEADS * HG_KEY_DIM
HG_VAL = HG_HEADS * HG_VAL_DIM
HG_CHUNK = 64
IN_SIZES = (ATT_WIDTH, KV_WIDTH, KV_WIDTH, HG_KEY, HG_KEY, HG_KEY, HG_VAL, HG_VAL)
IN_WIDTH = ATT_WIDTH + 2 * KV_WIDTH + 3 * HG_KEY + 2 * HG_VAL
MIX_WIDTH = ATT_WIDTH + HG_VAL
X_HEADS = 4
X_HEAD_DIM = 128
X_WIDTH = X_HEADS * X_HEAD_DIM
N_EXPERTS = 32
TOP_K = 4
D_FF = 2048
SWIGLU_ALPHA = 1.702
SWIGLU_LIMIT = 7.0
MOE_BLOCK = 256

kernel_name = "hymba_hgrn2_swa_moe_encoder"


def rmsnorm(x, w):
    xf = x.astype(jnp.float32)
    y = xf * lax.rsqrt(jnp.mean(xf * xf, axis=-1, keepdims=True) + EPS)
    return (y * w.astype(jnp.float32)).astype(x.dtype)


def rope(x, pos):
    hd = x.shape[-1]
    inv = ROPE_THETA ** (-jnp.arange(0, hd, 2, dtype=jnp.float32) / hd)
    ang = pos.astype(jnp.float32)[:, None] * inv[None, :]
    cos = jnp.concatenate([jnp.cos(ang), jnp.cos(ang)], -1)[None, :, None, :]
    sin = jnp.concatenate([jnp.sin(ang), jnp.sin(ang)], -1)[None, :, None, :]
    xf = x.astype(jnp.float32)
    x1, x2 = jnp.split(xf, 2, axis=-1)
    rot = jnp.concatenate([-x2, x1], -1)
    return (xf * cos + rot * sin).astype(x.dtype)


def window_attention(q, k, v, sink):
    B, T, _, hd = q.shape
    nb = T // ATT_BLOCK
    G = ATT_HEADS // ATT_KV_HEADS
    qb = q.reshape(B, nb, ATT_BLOCK, ATT_KV_HEADS, G, hd)
    pad = ((0, 0), (ATT_BLOCK, ATT_BLOCK), (0, 0), (0, 0))
    kp = jnp.pad(k, pad).reshape(B, nb + 2, ATT_BLOCK, ATT_KV_HEADS, hd)
    vp = jnp.pad(v, pad).reshape(B, nb + 2, ATT_BLOCK, ATT_KV_HEADS, hd)
    kb = jnp.concatenate([kp[:, :-2], kp[:, 1:-1], kp[:, 2:]], axis=2)
    vb = jnp.concatenate([vp[:, :-2], vp[:, 1:-1], vp[:, 2:]], axis=2)
    s = jnp.einsum('bnqhgd,bnkhd->bnhgqk', qb, kb, preferred_element_type=jnp.float32) * (hd ** -0.5)
    a = np.arange(ATT_BLOCK)[:, None]
    c = np.arange(3 * ATT_BLOCK)[None, :]
    n = np.arange(nb)[:, None, None]
    kpos = (n - 1) * ATT_BLOCK + c[None]
    mask = (np.abs(c - ATT_BLOCK - a)[None] <= WINDOW) & (kpos >= 0) & (kpos < T)
    s = jnp.where(jnp.asarray(mask)[None, :, None, None], s, -1e30)
    sk = sink.astype(jnp.float32).reshape(1, 1, ATT_KV_HEADS, G, 1, 1)
    m = jnp.maximum(jnp.max(s, axis=-1, keepdims=True), sk)
    p = jnp.exp(s - m)
    prob = p / (jnp.sum(p, axis=-1, keepdims=True) + jnp.exp(sk - m))
    o = jnp.einsum('bnhgqk,bnkhd->bnqhgd', prob.astype(v.dtype), vb)
    return o.reshape(B, T, ATT_HEADS * hd)


def hgrn2_direction(q, k, v, logf):
    B, H, T, dk = q.shape
    dv = v.shape[-1]
    C = HG_CHUNK
    nc = T // C
    q = q.reshape(B, H, nc, C, dk)
    k = k.reshape(B, H, nc, C, dk)
    v = v.reshape(B, H, nc, C, dv)
    b = jnp.cumsum(logf.reshape(B, H, nc, C, dk), axis=3)
    b_ref = b[:, :, :, C // 2 - 1:C // 2, :]
    b_last = b[:, :, :, -1:, :]
    A = jnp.einsum('bhncd,bhnsd->bhncs', q * jnp.exp(b - b_ref), k * jnp.exp(b_ref - b))
    A = jnp.where(jnp.tril(jnp.ones((C, C), dtype=bool)), A, 0.0)
    o = jnp.einsum('bhncs,bhnse->bhnce', A, v)
    u = jnp.einsum('bhnsd,bhnse->bhnde', k * jnp.exp(b_last - b), v)
    decay = jnp.exp(b_last[:, :, :, 0, :])

    def step(S, inp):
        d, un = inp
        return S * d[..., None] + un, S

    _, S_prev = lax.scan(step, jnp.zeros((B, H, dk, dv), jnp.float32),
                         (jnp.moveaxis(decay, 2, 0), jnp.moveaxis(u, 2, 0)))
    S_prev = jnp.moveaxis(S_prev, 0, 2)
    o = o + jnp.einsum('bhncd,bhnde->bhnce', q * jnp.exp(b), S_prev)
    return o.reshape(B, H, T, dv)


def parallel_mixer(h, w_in, w_out, sink, lb, hg_norm):
    B, T, _ = h.shape
    z = h @ w_in
    cuts, acc = [], 0
    for sz in IN_SIZES[:-1]:
        acc += sz
        cuts.append(acc)
    qa, ka, va, qh, ffw, fbw, ih, gh = jnp.split(z, cuts, axis=-1)
    pos = jnp.arange(T)
    qa = rope(qa.reshape(B, T, ATT_HEADS, ATT_HEAD_DIM), pos)
    ka = rope(ka.reshape(B, T, ATT_KV_HEADS, ATT_HEAD_DIM), pos)
    va = va.reshape(B, T, ATT_KV_HEADS, ATT_HEAD_DIM)
    att_out = window_attention(qa, ka, va, sink)
    def heads(t, d):
        return t.reshape(B, T, HG_HEADS, d).transpose(0, 2, 1, 3).astype(jnp.float32)
    qh = jax.nn.silu(heads(qh, HG_KEY_DIM)) * (HG_KEY_DIM ** -0.5)
    vh = heads(ih, HG_VAL_DIM)
    lbf = lb.astype(jnp.float32).reshape(2, HG_HEADS, 1, HG_KEY_DIM)
    f_fw = lbf[0] + (1.0 - lbf[0]) * jax.nn.sigmoid(heads(ffw, HG_KEY_DIM))
    f_bw = lbf[1] + (1.0 - lbf[1]) * jax.nn.sigmoid(heads(fbw, HG_KEY_DIM))
    o_fw = hgrn2_direction(qh, 1.0 - f_fw, vh, jnp.log(f_fw))
    flip = lambda t: jnp.flip(t, axis=2)
    o_bw = flip(hgrn2_direction(flip(qh), flip(1.0 - f_bw), flip(vh), flip(jnp.log(f_bw))))
    o = (o_fw + o_bw).transpose(0, 2, 1, 3)
    o = o * lax.rsqrt(jnp.mean(o * o, axis=-1, keepdims=True) + EPS) * hg_norm.astype(jnp.float32)
    g = gh.reshape(B, T, HG_HEADS, HG_VAL_DIM).astype(jnp.float32)
    hg_out = (o * jax.nn.silu(g)).reshape(B, T, HG_VAL).astype(h.dtype)
    return jnp.concatenate([att_out, hg_out], axis=-1) @ w_out


def memory_cross_attention(h, mem_n, w_xq, w_xkv, w_xo):
    B, T, _ = h.shape
    M = mem_n.shape[1]
    q = (h @ w_xq).reshape(B, T, X_HEADS, X_HEAD_DIM)
    k, v = jnp.split((mem_n @ w_xkv).reshape(B, M, 2 * X_HEADS, X_HEAD_DIM), 2, axis=2)
    s = jnp.einsum('bqhd,bkhd->bhqk', q, k, preferred_element_type=jnp.float32) * (X_HEAD_DIM ** -0.5)
    p = jax.nn.softmax(s, axis=-1)
    o = jnp.einsum('bhqk,bkhd->bqhd', p.astype(v.dtype), v).reshape(B, T, X_WIDTH)
    return o @ w_xo


def moe(h, w_router, b_router, w1, b1, w2, b2):
    B, T, D = h.shape
    xt = h.reshape(-1, D)
    M = xt.shape[0]
    logits = (xt @ w_router).astype(jnp.float32) + b_router.astype(jnp.float32)
    top_val, top_idx = lax.top_k(logits, TOP_K)
    gate = jax.nn.softmax(top_val, axis=-1)
    A = M * TOP_K
    e_flat = top_idx.reshape(-1).astype(jnp.int32)
    tok_flat = jnp.repeat(jnp.arange(M, dtype=jnp.int32), TOP_K)
    order = jnp.argsort(e_flat)
    e_s, tok_s, w_s = e_flat[order], tok_flat[order], gate.reshape(-1)[order]
    counts = jnp.bincount(e_flat, length=N_EXPERTS).astype(jnp.int32)
    padded = (counts + MOE_BLOCK - 1) // MOE_BLOCK * MOE_BLOCK
    start = jnp.cumsum(counts) - counts
    pend = jnp.cumsum(padded)
    pstart = pend - padded
    dest = pstart[e_s] + jnp.arange(A, dtype=jnp.int32) - start[e_s]
    n_blocks = -(-A // MOE_BLOCK) + N_EXPERTS
    P = n_blocks * MOE_BLOCK
    row_tok = jnp.zeros((P,), jnp.int32).at[dest].set(tok_s)
    row_w = jnp.zeros((P,), jnp.float32).at[dest].set(w_s)
    block_e = jnp.minimum(jnp.searchsorted(pend, jnp.arange(n_blocks, dtype=jnp.int32) * MOE_BLOCK,
                                           side='right'), N_EXPERTS - 1)

    def expert_block(args):
        tok, wt, e = args
        hb = xt[tok] @ w1[e] + b1[e]
        glu, lin = jnp.split(hb, 2, axis=-1)
        glu = jnp.minimum(glu, SWIGLU_LIMIT)
        lin = jnp.clip(lin, -SWIGLU_LIMIT, SWIGLU_LIMIT)
        act = glu * jax.nn.sigmoid(SWIGLU_ALPHA * glu) * (lin + 1.0)
        out = act @ w2[e] + b2[e]
        return out * wt[:, None].astype(out.dtype)

    outs = lax.map(expert_block, (row_tok.reshape(n_blocks, MOE_BLOCK),
                                  row_w.reshape(n_blocks, MOE_BLOCK), block_e))
    y = jax.ops.segment_sum(outs.reshape(P, D), row_tok, num_segments=M)
    return y.reshape(B, T, D).astype(h.dtype)


def encoder_trunk(x, mem, norm_mix, w_in, att_sink, hg_lb_logits, hg_norm, w_out,
                  norm_cross, norm_mem, w_xq, w_xkv, w_xo,
                  norm_ffn, w_router, b_router, w_moe1, b_moe1, w_moe2, b_moe2, norm_final):
    lb_all = jnp.cumsum(jax.nn.softmax(hg_lb_logits.astype(jnp.float32), axis=0), axis=0)
    for l in range(DEPTH):
        h = rmsnorm(x, norm_mix[l])
        x = x + parallel_mixer(h, w_in[l], w_out[l], att_sink[l], lb_all[l], hg_norm[l])
        h = rmsnorm(x, norm_cross[l])
        x = x + memory_cross_attention(h, rmsnorm(mem, norm_mem[l]), w_xq[l], w_xkv[l], w_xo[l])
        h = rmsnorm(x, norm_ffn[l])
        x = x + moe(h, w_router[l], b_router[l], w_moe1[l], b_moe1[l], w_moe2[l], b_moe2[l])
    return rmsnorm(x, norm_final)


def setup_inputs(seed: int = 0) -> dict:
    key = jax.random.key(seed)
    ks = jax.random.split(key, 24)
    f32 = jnp.float32
    nrm = lambda k, shape, s: jax.random.normal(k, shape, f32) * s
    gain = lambda k, shape: 1.0 + 0.05 * jax.random.normal(k, shape, f32)
    return {
        "x_prompt": nrm(ks[0], (BATCH, SEQ, D_MODEL), 1.0),
        "x_sample": nrm(ks[1], (DEC_BATCH, DEC_SEQ, D_MODEL), 1.0),
        "mem_prompt": nrm(ks[2], (BATCH, N_MEM, D_MODEL), 1.0),
        "mem_sample": nrm(ks[3], (DEC_BATCH, N_MEM, D_MODEL), 1.0),
        "norm_mix": gain(ks[4], (DEPTH, D_MODEL)),
        "w_in": nrm(ks[5], (DEPTH, D_MODEL, IN_WIDTH), D_MODEL ** -0.5),
        "att_sink": nrm(ks[6], (DEPTH, ATT_HEADS), 0.5),
        "hg_lb_logits": nrm(ks[7], (DEPTH + 1, 2, HG_KEY), 0.5),
        "hg_norm": gain(ks[8], (DEPTH, HG_VAL_DIM)),
        "w_out": nrm(ks[9], (DEPTH, MIX_WIDTH, D_MODEL), MIX_WIDTH ** -0.5),
        "norm_cross": gain(ks[10], (DEPTH, D_MODEL)),
        "norm_mem": gain(ks[11], (DEPTH, D_MODEL)),
        "w_xq": nrm(ks[12], (DEPTH, D_MODEL, X_WIDTH), D_MODEL ** -0.5),
        "w_xkv": nrm(ks[13], (DEPTH, D_MODEL, 2 * X_WIDTH), D_MODEL ** -0.5),
        "w_xo": nrm(ks[14], (DEPTH, X_WIDTH, D_MODEL), X_WIDTH ** -0.5),
        "norm_ffn": gain(ks[15], (DEPTH, D_MODEL)),
        "w_router": nrm(ks[16], (DEPTH, D_MODEL, N_EXPERTS), D_MODEL ** -0.5),
        "b_router": nrm(ks[17], (DEPTH, N_EXPERTS), 0.01),
        "w_moe1": nrm(ks[18], (DEPTH, N_EXPERTS, D_MODEL, 2 * D_FF), D_MODEL ** -0.5),
        "b_moe1": nrm(ks[19], (DEPTH, N_EXPERTS, 2 * D_FF), 0.02),
        "w_moe2": nrm(ks[20], (DEPTH, N_EXPERTS, D_FF, D_MODEL), D_FF ** -0.5),
        "b_moe2": nrm(ks[21], (DEPTH, N_EXPERTS, D_MODEL), 0.02),
        "norm_final": gain(ks[22], (D_MODEL,)),
    }


def reference(x_prompt, x_sample, mem_prompt, mem_sample, norm_mix, w_in, att_sink, hg_lb_logits, hg_norm,
              w_out, norm_cross, norm_mem, w_xq, w_xkv, w_xo, norm_ffn, w_router, b_router,
              w_moe1, b_moe1, w_moe2, b_moe2, norm_final):
    y_prompt = encoder_trunk(x_prompt, mem_prompt, norm_mix, w_in, att_sink, hg_lb_logits, hg_norm, w_out,
                             norm_cross, norm_mem, w_xq, w_xkv, w_xo, norm_ffn, w_router, b_router,
                             w_moe1, b_moe1, w_moe2, b_moe2, norm_final)
    y_sample = encoder_trunk(x_sample, mem_sample, norm_mix, w_in, att_sink, hg_lb_logits, hg_norm, w_out,
                             norm_cross, norm_mem, w_xq, w_xkv, w_xo, norm_ffn, w_router, b_router,
                             w_moe1, b_moe1, w_moe2, b_moe2, norm_final)
    return (y_prompt, y_sample)
```

```python
import functools

import jax
import jax.numpy as jnp
from jax import lax
from jax.experimental import pallas as pl
from jax.experimental.pallas import tpu as pltpu

F32 = jnp.float32
BF16 = jnp.bfloat16
I32 = jnp.int32

EPS = 1e-5
ATT_HEADS = 16
ATT_KV_HEADS = 4
ATT_HEAD_DIM = 64
ATT_WIDTH = ATT_HEADS * ATT_HEAD_DIM
KV_WIDTH = ATT_KV_HEADS * ATT_HEAD_DIM
WINDOW = 128
ROPE_THETA = 10000.0
HG_HEADS = 8
HG_DIM = 128
HG_WIDTH = HG_HEADS * HG_DIM
HG_CHUNK = 64
X_HEADS = 4
X_HEAD_DIM = 128
X_WIDTH = X_HEADS * X_HEAD_DIM
N_EXPERTS = 32
TOP_K = 4
SWIGLU_ALPHA = 1.702
SWIGLU_LIMIT = 7.0

COL_Q_ATT = 0
COL_K_ATT = ATT_WIDTH
COL_V_ATT = ATT_WIDTH + KV_WIDTH
COL_Q_HG = ATT_WIDTH + 2 * KV_WIDTH
COL_F_FWD = COL_Q_HG + HG_WIDTH
COL_F_BWD = COL_F_FWD + HG_WIDTH
COL_I_HG = COL_F_BWD + HG_WIDTH
COL_G_HG = COL_I_HG + HG_WIDTH

LANES = 128
VMEM_LIMIT = 56 * 1024 * 1024
SMEM_STAGE = 1024
MOE_TILE = 512

NT_DIMS = (((1,), (1,)), ((), ()))
TN_DIMS = (((0,), (0,)), ((), ()))


def _params(*sem):
    return pltpu.CompilerParams(dimension_semantics=sem, vmem_limit_bytes=VMEM_LIMIT)


def _rms(x, g):
    ms = jnp.mean(x * x, axis=-1, keepdims=True)
    return x * lax.rsqrt(ms + EPS) * g


def _sigmoid(x):
    return 1.0 / (1.0 + jnp.exp(-x))


def _norm_matmul_kernel(x_ref, g_ref, w_ref, o_ref, xn_ref):
    @pl.when(pl.program_id(1) == 0)
    def _():
        xn_ref[...] = _rms(x_ref[...], g_ref[...]).astype(BF16)

    o_ref[...] = jnp.dot(xn_ref[...], w_ref[...], preferred_element_type=F32).astype(o_ref.dtype)


def _norm_matmul(x, g, w, tm, tn, out_dtype, name):
    m, d = x.shape
    n = w.shape[1]
    return pl.pallas_call(
        _norm_matmul_kernel,
        grid=(m // tm, n // tn),
        in_specs=[pl.BlockSpec((tm, d), lambda i, j: (i, 0)),
                  pl.BlockSpec((1, d), lambda i, j: (0, 0)),
                  pl.BlockSpec((d, tn), lambda i, j: (0, j))],
        out_specs=pl.BlockSpec((tm, tn), lambda i, j: (i, j)),
        out_shape=jax.ShapeDtypeStruct((m, n), out_dtype),
        scratch_shapes=[pltpu.VMEM((tm, d), BF16)],
        compiler_params=_params("parallel", "arbitrary"),
        name=name,
    )(x, g.reshape(1, d), w)


def _rope_pair(x, cos, sin_signed):
    lane = lax.broadcasted_iota(I32, (1, LANES), 1)
    first_half = (lane % ATT_HEAD_DIM) < (ATT_HEAD_DIM // 2)
    rot = jnp.where(first_half, pltpu.roll(x, LANES - ATT_HEAD_DIM // 2, 1), pltpu.roll(x, ATT_HEAD_DIM // 2, 1))
    return x * cos + rot * sin_signed


def _attn_kernel(sink_ref, q_ref, k_ref, v_ref, cq_ref, sq_ref, ck_ref, sk_ref, o_ref, kr_ref, vr_ref, *, seq, tq):
    qi = pl.program_id(1)
    win = tq + 2 * WINDOW

    @pl.when(qi == 0)
    def _():
        for j in range(KV_WIDTH // LANES):
            sl = slice(j * LANES, (j + 1) * LANES)
            kr_ref[:, sl] = _rope_pair(k_ref[:, sl], ck_ref[...], sk_ref[...]).astype(BF16)
        vr_ref[...] = v_ref[...].astype(BF16)

    q0 = qi * tq
    ks = pl.multiple_of(jnp.clip(q0 - WINDOW, 0, seq - win), WINDOW)
    kwin = kr_ref[pl.ds(ks, win), :]
    vwin = vr_ref[pl.ds(ks, win), :]
    qpos = q0 + lax.broadcasted_iota(I32, (tq, 1), 0)
    kpos = ks + lax.broadcasted_iota(I32, (1, win), 1)
    valid = jnp.abs(kpos - qpos) <= WINDOW
    cq = cq_ref[...]
    sq = sq_ref[...]
    group = ATT_HEADS // ATT_KV_HEADS
    for j in range(ATT_WIDTH // LANES):
        qs = (_rope_pair(q_ref[:, j * LANES:(j + 1) * LANES], cq, sq) * (ATT_HEAD_DIM ** -0.5)).astype(BF16)
        outs = []
        for hh in range(LANES // ATT_HEAD_DIM):
            h = j * (LANES // ATT_HEAD_DIM) + hh
            g = h // group
            qh = qs[:, hh * ATT_HEAD_DIM:(hh + 1) * ATT_HEAD_DIM]
            kh = kwin[:, g * ATT_HEAD_DIM:(g + 1) * ATT_HEAD_DIM]
            vh = vwin[:, g * ATT_HEAD_DIM:(g + 1) * ATT_HEAD_DIM]
            s = lax.dot_general(qh, kh, NT_DIMS, preferred_element_type=F32)
            s = jnp.where(valid, s, -1e30)
            sk = sink_ref[h]
            m = jnp.maximum(jnp.max(s, axis=-1, keepdims=True), sk)
            p = jnp.exp(s - m)
            denom = jnp.sum(p, axis=-1, keepdims=True) + jnp.exp(sk - m)
            o = jnp.dot(p.astype(BF16), vh, preferred_element_type=F32)
            outs.append(o / denom)
        o_ref[:, j * LANES:(j + 1) * LANES] = jnp.concatenate(outs, axis=-1).astype(o_ref.dtype)


def _window_attention(z, sink, cos, sin_signed, batch, seq, tq):
    m = z.shape[0]
    nq = seq // tq
    qblk = COL_Q_ATT // ATT_WIDTH
    kblk = COL_K_ATT // KV_WIDTH
    vblk = COL_V_ATT // KV_WIDTH
    return pl.pallas_call(
        functools.partial(_attn_kernel, seq=seq, tq=tq),
        grid=(batch, nq),
        in_specs=[pl.BlockSpec(memory_space=pltpu.SMEM),
                  pl.BlockSpec((tq, ATT_WIDTH), lambda b, i: (b * nq + i, qblk)),
                  pl.BlockSpec((seq, KV_WIDTH), lambda b, i: (b, kblk)),
                  pl.BlockSpec((seq, KV_WIDTH), lambda b, i: (b, vblk)),
                  pl.BlockSpec((tq, LANES), lambda b, i: (i, 0)),
                  pl.BlockSpec((tq, LANES), lambda b, i: (i, 0)),
                  pl.BlockSpec((seq, LANES), lambda b, i: (0, 0)),
                  pl.BlockSpec((seq, LANES), lambda b, i: (0, 0))],
        out_specs=pl.BlockSpec((tq, ATT_WIDTH), lambda b, i: (b * nq + i, 0)),
        out_shape=jax.ShapeDtypeStruct((m, ATT_WIDTH), BF16),
        scratch_shapes=[pltpu.VMEM((seq, KV_WIDTH), BF16), pltpu.VMEM((seq, KV_WIDTH), BF16)],
        compiler_params=_params("parallel", "arbitrary"),
        name="window_attention",
    )(sink, z, z, z, cos, sin_signed, cos, sin_signed)


def _split_cumsum(tri, x):
    hi = x.astype(BF16)
    r1 = x - hi.astype(F32)
    mid = r1.astype(BF16)
    lo = (r1 - mid.astype(F32)).astype(BF16)
    dot = functools.partial(jnp.dot, preferred_element_type=F32)
    return dot(tri, hi) + dot(tri, mid) + dot(tri, lo)


def _hgrn_group(zq, zf, v, lbv, state, mask, tri, forward):
    rows = zq.shape[0]
    c = HG_CHUNK
    nchunk = rows // c
    q = zq * _sigmoid(zq) * (HG_DIM ** -0.5)
    f = lbv + (1.0 - lbv) * _sigmoid(zf)
    logf = jnp.log(f)
    k = 1.0 - f
    b = _split_cumsum(tri, logf)
    ref_row = c // 2 - 1 if forward else c // 2
    last_row = c - 1 if forward else 0
    qe, ke, kd, qb, dec = [], [], [], [], []
    for n in range(nchunk):
        sl = slice(n * c, (n + 1) * c)
        bn, qn, kn = b[sl], q[sl], k[sl]
        bref = bn[ref_row:ref_row + 1]
        blast = bn[last_row:last_row + 1]
        qe.append(qn * jnp.exp(bn - bref))
        ke.append(kn * jnp.exp(bref - bn))
        kd.append((kn * jnp.exp(blast - bn)).astype(BF16))
        qb.append((qn * jnp.exp(bn)).astype(BF16))
        dec.append(jnp.exp(blast))
    qe = jnp.concatenate(qe, axis=0).astype(BF16)
    ke = jnp.concatenate(ke, axis=0).astype(BF16)
    vb = v.astype(BF16)
    a = lax.dot_general(qe, ke, NT_DIMS, preferred_element_type=F32)
    a = jnp.where(mask, a, 0.0).astype(BF16)
    o = jnp.dot(a, vb, preferred_element_type=F32)
    parts = [None] * nchunk
    order = range(nchunk) if forward else range(nchunk - 1, -1, -1)
    for n in order:
        sl = slice(n * c, (n + 1) * c)
        parts[n] = o[sl] + lax.dot_general(qb[n], state.astype(BF16), NT_DIMS, preferred_element_type=F32)
        u = lax.dot_general(vb[sl], kd[n], TN_DIMS, preferred_element_type=F32)
        state = state * dec[n] + u
    return jnp.concatenate(parts, axis=0), state


def _hgrn_kernel(zq_ref, zff_ref, zfb_ref, zi_ref, zg_ref, lb_ref, gn_ref, o_ref, acc_ref, *, seq, rows):
    ngroups = seq // rows
    r = lax.broadcasted_iota(I32, (rows, rows), 0)
    cidx = lax.broadcasted_iota(I32, (rows, rows), 1)
    same = (r // HG_CHUNK) == (cidx // HG_CHUNK)
    mask_f = jnp.logical_and(same, cidx <= r)
    mask_b = jnp.logical_and(same, cidx >= r)
    tri_f = jnp.where(mask_f, 1.0, 0.0).astype(BF16)
    tri_b = jnp.where(mask_b, 1.0, 0.0).astype(BF16)
    lb_f = lb_ref[0:1, :]
    lb_b = lb_ref[1:2, :]
    zero_state = jnp.zeros((HG_DIM, HG_DIM), F32)

    def fwd_body(gi, state):
        sl = pl.ds(pl.multiple_of(gi * rows, rows), rows)
        o, state = _hgrn_group(zq_ref[sl, :], zff_ref[sl, :], zi_ref[sl, :], lb_f, state, mask_f, tri_f, True)
        acc_ref[sl, :] = o
        return state

    lax.fori_loop(0, ngroups, fwd_body, zero_state)

    def bwd_body(it, state):
        gi = ngroups - 1 - it
        sl = pl.ds(pl.multiple_of(gi * rows, rows), rows)
        o, state = _hgrn_group(zq_ref[sl, :], zfb_ref[sl, :], zi_ref[sl, :], lb_b, state, mask_b, tri_b, False)
        tot = acc_ref[sl, :] + o
        zg = zg_ref[sl, :]
        y = _rms(tot, gn_ref[...]) * (zg * _sigmoid(zg))
        o_ref[sl, :] = y.astype(o_ref.dtype)
        return state

    lax.fori_loop(0, ngroups, bwd_body, zero_state)


def _hgrn2(z, lb, hg_norm, batch, seq, rows):
    m = z.shape[0]

    def zspec(col):
        base = col // HG_DIM
        return pl.BlockSpec((seq, HG_DIM), lambda b, h: (b, base + h))

    return pl.pallas_call(
        functools.partial(_hgrn_kernel, seq=seq, rows=rows),
        grid=(batch, HG_HEADS),
        in_specs=[zspec(COL_Q_HG), zspec(COL_F_FWD), zspec(COL_F_BWD), zspec(COL_I_HG), zspec(COL_G_HG),
                  pl.BlockSpec((2, HG_DIM), lambda b, h: (0, h)),
                  pl.BlockSpec((1, HG_DIM), lambda b, h: (0, 0))],
        out_specs=pl.BlockSpec((seq, HG_DIM), lambda b, h: (b, h)),
        out_shape=jax.ShapeDtypeStruct((m, HG_WIDTH), BF16),
        scratch_shapes=[pltpu.VMEM((seq, HG_DIM), F32)],
        compiler_params=_params("parallel", "parallel"),
        name="hgrn2",
    )(z, z, z, z, z, lb, hg_norm.reshape(1, HG_DIM))


def _outproj_kernel(a_ref, h_ref, x_ref, wa_ref, wh_ref, o_ref):
    acc = jnp.dot(a_ref[...], wa_ref[...], preferred_element_type=F32)
    acc = acc + jnp.dot(h_ref[...], wh_ref[...], preferred_element_type=F32)
    o_ref[...] = x_ref[...] + acc


def _out_projection(att, hg, x, w_out, tm):
    m, d = x.shape
    return pl.pallas_call(
        _outproj_kernel,
        grid=(m // tm,),
        in_specs=[pl.BlockSpec((tm, ATT_WIDTH), lambda i: (i, 0)),
                  pl.BlockSpec((tm, HG_WIDTH), lambda i: (i, 0)),
                  pl.BlockSpec((tm, d), lambda i: (i, 0)),
                  pl.BlockSpec((ATT_WIDTH, d), lambda i: (0, 0)),
                  pl.BlockSpec((HG_WIDTH, d), lambda i: (ATT_WIDTH // HG_WIDTH, 0))],
        out_specs=pl.BlockSpec((tm, d), lambda i: (i, 0)),
        out_shape=jax.ShapeDtypeStruct((m, d), F32),
        compiler_params=_params("parallel"),
        name="out_projection",
    )(att, hg, x, w_out, w_out)


def _cross_kernel(x_ref, g_ref, kv_ref, wq_ref, wo_ref, o_ref):
    x = x_ref[...]
    h = _rms(x, g_ref[...]).astype(BF16)
    q = jnp.dot(h, wq_ref[...], preferred_element_type=F32).astype(BF16)
    outs = []
    for hd in range(X_HEADS):
        sl = slice(hd * X_HEAD_DIM, (hd + 1) * X_HEAD_DIM)
        kh = kv_ref[:, sl]
        vh = kv_ref[:, X_WIDTH + hd * X_HEAD_DIM:X_WIDTH + (hd + 1) * X_HEAD_DIM]
        s = lax.dot_general(q[:, sl], kh, NT_DIMS, preferred_element_type=F32) * (X_HEAD_DIM ** -0.5)
        p = jnp.exp(s - jnp.max(s, axis=-1, keepdims=True))
        denom = jnp.sum(p, axis=-1, keepdims=True)
        outs.append(jnp.dot(p.astype(BF16), vh, preferred_element_type=F32) / denom)
    oc = jnp.concatenate(outs, axis=-1).astype(BF16)
    o_ref[...] = x + jnp.dot(oc, wo_ref[...], preferred_element_type=F32)


def _cross_attention(x, g, kv, w_xq, w_xo, batch, seq, n_mem, tm):
    m, d = x.shape
    nt = seq // tm
    return pl.pallas_call(
        _cross_kernel,
        grid=(batch, nt),
        in_specs=[pl.BlockSpec((tm, d), lambda b, i: (b * nt + i, 0)),
                  pl.BlockSpec((1, d), lambda b, i: (0, 0)),
                  pl.BlockSpec((n_mem, 2 * X_WIDTH), lambda b, i: (b, 0)),
                  pl.BlockSpec((d, X_WIDTH), lambda b, i: (0, 0)),
                  pl.BlockSpec((X_WIDTH, d), lambda b, i: (0, 0))],
        out_specs=pl.BlockSpec((tm, d), lambda b, i: (b * nt + i, 0)),
        out_shape=jax.ShapeDtypeStruct((m, d), F32),
        compiler_params=_params("parallel", "parallel"),
        name="cross_attention",
    )(x, g.reshape(1, d), kv, w_xq, w_xo)


def _router_kernel(x_ref, g_ref, wr_ref, br_ref, h_ref, idx_ref, gate_ref, rank_ref, cnt_ref, base_ref):
    tm = x_ref.shape[0]

    @pl.when(pl.program_id(0) == 0)
    def _():
        base_ref[...] = jnp.zeros_like(base_ref)

    h = _rms(x_ref[...], g_ref[...])
    h_ref[...] = h
    hh = h.astype(BF16)
    hl = (h - hh.astype(F32)).astype(BF16)
    w = wr_ref[...]
    wh = w.astype(BF16)
    wl = (w - wh.astype(F32)).astype(BF16)
    nt = functools.partial(lax.dot_general, dimension_numbers=NT_DIMS, preferred_element_type=F32)
    logits = nt(wh, hh) + nt(wh, hl) + nt(wl, hh) + br_ref[...]

    eio = lax.broadcasted_iota(I32, (N_EXPERTS, tm), 0).astype(F32)
    work = logits
    vals, onehots = [], []
    for k in range(TOP_K):
        mx = jnp.max(work, axis=0, keepdims=True)
        ix = jnp.min(jnp.where(work == mx, eio, float(N_EXPERTS)), axis=0, keepdims=True)
        sel = eio == ix
        vals.append(mx)
        onehots.append(sel)
        idx_ref[k:k + 1, :] = ix.astype(I32)
        work = jnp.where(sel, -jnp.inf, work)
    ex = [jnp.exp(v - vals[0]) for v in vals]
    denom = ex[0] + ex[1] + ex[2] + ex[3]
    for k in range(TOP_K):
        gate_ref[k:k + 1, :] = ex[k] / denom

    oh = [jnp.where(s, 1.0, 0.0) for s in onehots]
    oh_all = oh[0] + oh[1] + oh[2] + oh[3]
    r = lax.broadcasted_iota(I32, (tm, tm), 0)
    c = lax.broadcasted_iota(I32, (tm, tm), 1)
    upper = jnp.where(r < c, 1.0, 0.0).astype(BF16)
    before = jnp.dot(oh_all.astype(BF16), upper, preferred_element_type=F32) + base_ref[...]
    for k in range(TOP_K):
        rank_ref[k:k + 1, :] = jnp.sum(oh[k] * before, axis=0, keepdims=True).astype(I32)
    base_ref[...] = base_ref[...] + jnp.sum(oh_all, axis=1, keepdims=True)
    cnt_ref[...] = base_ref[...]


def _router(x, g, w_router_t, b_router, tm):
    m, d = x.shape
    row4 = pl.BlockSpec((TOP_K, tm), lambda i: (0, i))
    return pl.pallas_call(
        _router_kernel,
        grid=(m // tm,),
        in_specs=[pl.BlockSpec((tm, d), lambda i: (i, 0)),
                  pl.BlockSpec((1, d), lambda i: (0, 0)),
                  pl.BlockSpec((N_EXPERTS, d), lambda i: (0, 0)),
                  pl.BlockSpec((N_EXPERTS, 1), lambda i: (0, 0))],
        out_specs=[pl.BlockSpec((tm, d), lambda i: (i, 0)), row4, row4, row4,
                   pl.BlockSpec((N_EXPERTS, 1), lambda i: (0, 0))],
        out_shape=[jax.ShapeDtypeStruct((m, d), F32),
                   jax.ShapeDtypeStruct((TOP_K, m), I32),
                   jax.ShapeDtypeStruct((TOP_K, m), F32),
                   jax.ShapeDtypeStruct((TOP_K, m), I32),
                   jax.ShapeDtypeStruct((N_EXPERTS, 1), F32)],
        scratch_shapes=[pltpu.VMEM((N_EXPERTS, 1), F32)],
        compiler_params=_params("arbitrary"),
        name="router",
    )(x, g.reshape(1, d), w_router_t, b_router.reshape(N_EXPERTS, 1))


def _dispatch_kernel(seg_ref, dest_hbm, h_hbm, xs_hbm, dsm, zrow, sem_idx, sem_row, *, m_tokens, p_rows):
    i = pl.program_id(0)
    tt = SMEM_STAGE

    def idx_copy(k):
        return pltpu.make_async_copy(dest_hbm.at[pl.ds(k * m_tokens + i * tt, tt)],
                                     dsm.at[pl.ds(k * tt, tt)], sem_idx)

    for k in range(TOP_K):
        idx_copy(k).start()
    for k in range(TOP_K):
        idx_copy(k).wait()

    def row_copy(src_row, dst_row):
        return pltpu.make_async_copy(h_hbm.at[pl.ds(src_row, 1)], xs_hbm.at[pl.ds(dst_row, 1)], sem_row)

    def issue(t, carry):
        for k in range(TOP_K):
            row_copy(i * tt + t, dsm[k * tt + t]).start()
        return carry

    lax.fori_loop(0, tt, issue, 0, unroll=8)

    def drain(t, carry):
        for k in range(TOP_K):
            row_copy(0, 0).wait()
        return carry

    lax.fori_loop(0, tt, drain, 0, unroll=8)

    @pl.when(i == 0)
    def _():
        zrow[...] = jnp.zeros_like(zrow)

        def zero_copy(dst_row):
            return pltpu.make_async_copy(zrow.at[pl.ds(0, 1)], xs_hbm.at[pl.ds(dst_row, 1)], sem_row)

        def fill(lo, hi):
            def body(rw, carry):
                zero_copy(rw).start()
                return carry

            lax.fori_loop(lo, hi, body, 0)

            def wbody(rw, carry):
                zero_copy(0).wait()
                return carry

            lax.fori_loop(lo, hi, wbody, 0)

        def per_expert(e, carry):
            fill(seg_ref[0, e] + seg_ref[1, e], seg_ref[2, e])
            return carry

        lax.fori_loop(0, N_EXPERTS, per_expert, 0)
        fill(seg_ref[2, N_EXPERTS - 1], p_rows)


def _dispatch(seg, dest_flat, h, p_rows):
    m, d = h.shape
    return pl.pallas_call(
        functools.partial(_dispatch_kernel, m_tokens=m, p_rows=p_rows),
        grid=(m // SMEM_STAGE,),
        in_specs=[pl.BlockSpec(memory_space=pltpu.SMEM),
                  pl.BlockSpec(memory_space=pl.ANY),
                  pl.BlockSpec(memory_space=pl.ANY)],
        out_specs=pl.BlockSpec(memory_space=pl.ANY),
        out_shape=jax.ShapeDtypeStruct((p_rows, d), F32),
        scratch_shapes=[pltpu.SMEM((TOP_K * SMEM_STAGE,), I32),
                        pltpu.VMEM((8, d), F32),
                        pltpu.SemaphoreType.DMA,
                        pltpu.SemaphoreType.DMA],
        compiler_params=_params("arbitrary"),
        name="dispatch",
    )(seg, dest_flat, h)


def _expert_kernel(te_ref, tv_ref, nu_ref, x_ref, w1g_ref, w1l_ref, b1g_ref, b1l_ref, w2_ref, b2_ref,
                   o_ref, xb_ref, acc_ref):
    i = pl.program_id(0)
    f = pl.program_id(1)
    nf = pl.num_programs(1)
    live = tv_ref[i] > 0

    @pl.when(live)
    def _():
        @pl.when(f == 0)
        def _():
            xb_ref[...] = x_ref[...].astype(BF16)

        xb = xb_ref[...]
        glu = jnp.dot(xb, w1g_ref[0], preferred_element_type=F32) + b1g_ref[0]
        lin = jnp.dot(xb, w1l_ref[0], preferred_element_type=F32) + b1l_ref[0]
        glu = jnp.minimum(glu, SWIGLU_LIMIT)
        lin = jnp.clip(lin, -SWIGLU_LIMIT, SWIGLU_LIMIT)
        act = glu * _sigmoid(SWIGLU_ALPHA * glu) * (lin + 1.0)
        part = jnp.dot(act.astype(BF16), w2_ref[0], preferred_element_type=F32)

        @pl.when(f == 0)
        def _():
            acc_ref[...] = part + b2_ref[0]

        @pl.when(f > 0)
        def _():
            acc_ref[...] = acc_ref[...] + part

        @pl.when(f == nf - 1)
        def _():
            o_ref[...] = acc_ref[...]

    @pl.when(jnp.logical_and(jnp.logical_not(live), f == nf - 1))
    def _():
        o_ref[...] = jnp.zeros_like(o_ref)


def _expert_mlp(tile_e, tile_valid, n_used, xs, w1, b1, w2, b2, tmb, tf):
    p_rows, d = xs.shape
    d_ff = w2.shape[1]
    nf = d_ff // tf
    n_tiles = p_rows // tmb

    def fsel(i, f, tv):
        return jnp.where(tv[i] > 0, f, nf - 1)

    grid_spec = pltpu.PrefetchScalarGridSpec(
        num_scalar_prefetch=3,
        grid=(n_tiles, nf),
        in_specs=[
            pl.BlockSpec((tmb, d), lambda i, f, te, tv, nu: (jnp.minimum(i, nu[0] - 1), 0)),
            pl.BlockSpec((1, d, tf), lambda i, f, te, tv, nu: (te[i], 0, fsel(i, f, tv))),
            pl.BlockSpec((1, d, tf), lambda i, f, te, tv, nu: (te[i], 0, nf + fsel(i, f, tv))),
            pl.BlockSpec((1, 1, tf), lambda i, f, te, tv, nu: (te[i], 0, fsel(i, f, tv))),
            pl.BlockSpec((1, 1, tf), lambda i, f, te, tv, nu: (te[i], 0, nf + fsel(i, f, tv))),
            pl.BlockSpec((1, tf, d), lambda i, f, te, tv, nu: (te[i], fsel(i, f, tv), 0)),
            pl.BlockSpec((1, 1, d), lambda i, f, te, tv, nu: (te[i], 0, 0)),
        ],
        out_specs=pl.BlockSpec((tmb, d), lambda i, f, te, tv, nu: (i, 0)),
        scratch_shapes=[pltpu.VMEM((tmb, d), BF16), pltpu.VMEM((tmb, d), F32)],
    )
    return pl.pallas_call(
        _expert_kernel,
        grid_spec=grid_spec,
        out_shape=jax.ShapeDtypeStruct((p_rows, d), F32),
        compiler_params=_params("arbitrary", "arbitrary"),
        name="expert_mlp",
    )(tile_e, tile_valid, n_used, xs, w1, w1, b1, b1, w2, b2)


def _combine_kernel(dest_hbm, ys_hbm, x_ref, gate_ref, g_ref, o_ref, dsm, buf, sem_idx, sem_row, *, m_tokens, tc):
    i = pl.program_id(0)
    per_stage = SMEM_STAGE // tc
    stage = i // per_stage
    off = (i % per_stage) * tc

    @pl.when(i % per_stage == 0)
    def _():
        def idx_copy(k):
            return pltpu.make_async_copy(dest_hbm.at[pl.ds(k * m_tokens + stage * SMEM_STAGE, SMEM_STAGE)],
                                         dsm.at[pl.ds(k * SMEM_STAGE, SMEM_STAGE)], sem_idx)

        for k in range(TOP_K):
            idx_copy(k).start()
        for k in range(TOP_K):
            idx_copy(k).wait()

    def row_copy(k, t, src_row):
        return pltpu.make_async_copy(ys_hbm.at[pl.ds(src_row, 1)], buf.at[k, pl.ds(t, 1)], sem_row)

    def issue(t, carry):
        for k in range(TOP_K):
            row_copy(k, t, dsm[k * SMEM_STAGE + off + t]).start()
        return carry

    lax.fori_loop(0, tc, issue, 0, unroll=8)

    def drain(t, carry):
        for k in range(TOP_K):
            row_copy(k, 0, 0).wait()
        return carry

    lax.fori_loop(0, tc, drain, 0, unroll=8)

    gate = gate_ref[...]
    y = x_ref[...]
    for k in range(TOP_K):
        y = y + buf[k] * gate[:, k:k + 1]
    o_ref[...] = _rms(y, g_ref[...])


def _combine(dest_flat, ys, x, gate_t, g, tc):
    m, d = x.shape
    return pl.pallas_call(
        functools.partial(_combine_kernel, m_tokens=m, tc=tc),
        grid=(m // tc,),
        in_specs=[pl.BlockSpec(memory_space=pl.ANY),
                  pl.BlockSpec(memory_space=pl.ANY),
                  pl.BlockSpec((tc, d), lambda i: (i, 0)),
                  pl.BlockSpec((tc, TOP_K), lambda i: (i, 0)),
                  pl.BlockSpec((1, d), lambda i: (0, 0))],
        out_specs=pl.BlockSpec((tc, d), lambda i: (i, 0)),
        out_shape=jax.ShapeDtypeStruct((m, d), F32),
        scratch_shapes=[pltpu.SMEM((TOP_K * SMEM_STAGE,), I32),
                        pltpu.VMEM((TOP_K, tc, d), F32),
                        pltpu.SemaphoreType.DMA,
                        pltpu.SemaphoreType.DMA],
        compiler_params=_params("arbitrary"),
        name="combine",
    )(dest_flat, ys, x, gate_t, g.reshape(1, d))


def _rope_tables(seq):
    half = ATT_HEAD_DIM // 2
    inv = ROPE_THETA ** (-jnp.arange(0, ATT_HEAD_DIM, 2, dtype=F32) / ATT_HEAD_DIM)
    ang = jnp.arange(seq, dtype=F32)[:, None] * inv[None, :]
    cos = jnp.tile(jnp.cos(ang), (1, LANES // half))
    sin = jnp.sin(ang)
    sin_signed = jnp.tile(jnp.concatenate([-sin, sin], axis=-1), (1, LANES // ATT_HEAD_DIM))
    return cos, sin_signed


def _pick(n, prefs):
    for p in prefs:
        if n % p == 0:
            return p
    return n


def _trunk(x, mem, wts, moe_tile, ff_tile):
    batch, seq, d = x.shape
    n_mem = mem.shape[1]
    m = batch * seq
    x2d = x.reshape(m, d)
    tm = _pick(m, (512, 256, 128))

    n_in = wts["w_in"].shape[1]
    z = _norm_matmul(x2d, wts["norm_mix"], wts["w_in"], tm, _pick(n_in, (1664, 1024, 512, 256, 128)), F32, "in_projection")

    cos, sin_signed = _rope_tables(seq)
    att = _window_attention(z, wts["att_sink"], cos, sin_signed, batch, seq, _pick(seq, (256, 128)))
    hg = _hgrn2(z, wts["lb"], wts["hg_norm"], batch, seq, _pick(seq, (256, 128, 64)))
    x1 = _out_projection(att, hg, x2d, wts["w_out"], tm)

    kv = _norm_matmul(mem.reshape(batch * n_mem, d), wts["norm_mem"], wts["w_xkv"], n_mem, 2 * X_WIDTH, BF16,
                      "memory_kv")
    x2 = _cross_attention(x1, wts["norm_cross"], kv, wts["w_xq"], wts["w_xo"], batch, seq, n_mem,
                          _pick(seq, (512, 256, 128)))

    h3, idx, gate, rank, cnt = _router(x2, wts["norm_ffn"], wts["w_router_t"], wts["b_router"], tm)

    counts = cnt[:, 0].astype(I32)
    padded = (counts + moe_tile - 1) // moe_tile * moe_tile
    pend = jnp.cumsum(padded)
    pstart = pend - padded
    dest = pstart[idx] + rank
    n_tiles = -(-(m * TOP_K) // moe_tile) + N_EXPERTS
    p_rows = n_tiles * moe_tile
    tile_row = jnp.arange(n_tiles, dtype=I32) * moe_tile
    tile_e = jnp.minimum(jnp.searchsorted(pend, tile_row, side="right"), N_EXPERTS - 1).astype(I32)
    tile_valid = jnp.clip(pstart[tile_e] + counts[tile_e] - tile_row, 0, moe_tile).astype(I32)
    n_used = (pend[-1:] // moe_tile).astype(I32)
    seg = jnp.stack([pstart, counts, pend]).astype(I32)
    dest_flat = dest.reshape(-1)

    xs = _dispatch(seg, dest_flat, h3, p_rows)
    ys = _expert_mlp(tile_e, tile_valid, n_used, xs, wts["w_moe1"], wts["b_moe1"], wts["w_moe2"], wts["b_moe2"],
                     moe_tile, ff_tile)
    y = _combine(dest_flat, ys, x2, gate.T, wts["norm_final"], _pick(m, (256, 128)))
    return y.reshape(batch, seq, d)


def kernel(x_prompt, x_sample, mem_prompt, mem_sample, norm_mix, w_in, att_sink, hg_lb_logits, hg_norm, w_out,
           norm_cross, norm_mem, w_xq, w_xkv, w_xo, norm_ffn, w_router, b_router, w_moe1, b_moe1, w_moe2, b_moe2,
           norm_final):
    assert w_in.shape[0] == 1, "the final norm is fused after the single layer"
    lb_all = jnp.cumsum(jax.nn.softmax(hg_lb_logits.astype(F32), axis=0), axis=0)
    d_ff = w_moe2.shape[2]
    wts = dict(
        norm_mix=norm_mix[0], w_in=w_in[0].astype(BF16), att_sink=att_sink[0], lb=lb_all[0],
        hg_norm=hg_norm[0], w_out=w_out[0].astype(BF16), norm_cross=norm_cross[0], norm_mem=norm_mem[0],
        w_xq=w_xq[0].astype(BF16), w_xkv=w_xkv[0].astype(BF16), w_xo=w_xo[0].astype(BF16),
        norm_ffn=norm_ffn[0], w_router_t=w_router[0].T, b_router=b_router[0],
        w_moe1=w_moe1[0].astype(BF16), b_moe1=b_moe1[0][:, None, :],
        w_moe2=w_moe2[0].astype(BF16), b_moe2=b_moe2[0][:, None, :],
        norm_final=norm_final,
    )
    ff_tile = _pick(d_ff, (512, 256, 128))
    return (_trunk(x_prompt, mem_prompt, wts, MOE_TILE, ff_tile),
            _trunk(x_sample, mem_sample, wts, MOE_TILE, ff_tile))
```

```python
import functools

import jax
import jax.numpy as jnp
from jax import lax
from jax.experimental import pallas as pl
from jax.experimental.pallas import tpu as pltpu

F32 = jnp.float32
BF16 = jnp.bfloat16
I32 = jnp.int32

EPS = 1e-5
ATT_HEADS = 16
ATT_KV_HEADS = 4
ATT_HEAD_DIM = 64
ATT_WIDTH = ATT_HEADS * ATT_HEAD_DIM
KV_WIDTH = ATT_KV_HEADS * ATT_HEAD_DIM
WINDOW = 128
ROPE_THETA = 10000.0
HG_HEADS = 8
HG_DIM = 128
HG_WIDTH = HG_HEADS * HG_DIM
HG_CHUNK = 64
X_HEADS = 4
X_HEAD_DIM = 128
X_WIDTH = X_HEADS * X_HEAD_DIM
N_EXPERTS = 32
TOP_K = 4
SWIGLU_ALPHA = 1.702
SWIGLU_LIMIT = 7.0

COL_Q_ATT = 0
COL_K_ATT = ATT_WIDTH
COL_V_ATT = ATT_WIDTH + KV_WIDTH
COL_Q_HG = ATT_WIDTH + 2 * KV_WIDTH
COL_F_FWD = COL_Q_HG + HG_WIDTH
COL_F_BWD = COL_F_FWD + HG_WIDTH
COL_I_HG = COL_F_BWD + HG_WIDTH
COL_G_HG = COL_I_HG + HG_WIDTH

LANES = 128
VMEM_LIMIT = 56 * 1024 * 1024
SMEM_STAGE = 1024
MOE_TILE = 512

NT_DIMS = (((1,), (1,)), ((), ()))
TN_DIMS = (((0,), (0,)), ((), ()))


def _params(*sem):
    return pltpu.CompilerParams(dimension_semantics=sem, vmem_limit_bytes=VMEM_LIMIT)


def _rms(x, g):
    ms = jnp.mean(x * x, axis=-1, keepdims=True)
    return x * lax.rsqrt(ms + EPS) * g


def _sigmoid(x):
    return 1.0 / (1.0 + jnp.exp(-x))


def _norm_matmul_kernel(x_ref, g_ref, w_ref, o_ref, xn_ref):
    @pl.when(pl.program_id(1) == 0)
    def _():
        xn_ref[...] = _rms(x_ref[...], g_ref[...]).astype(BF16)

    o_ref[...] = jnp.dot(xn_ref[...], w_ref[...], preferred_element_type=F32).astype(o_ref.dtype)


def _norm_matmul(x, g, w, tm, tn, out_dtype, name):
    m, d = x.shape
    n = w.shape[1]
    return pl.pallas_call(
        _norm_matmul_kernel,
        grid=(m // tm, n // tn),
        in_specs=[pl.BlockSpec((tm, d), lambda i, j: (i, 0)),
                  pl.BlockSpec((1, d), lambda i, j: (0, 0)),
                  pl.BlockSpec((d, tn), lambda i, j: (0, j))],
        out_specs=pl.BlockSpec((tm, tn), lambda i, j: (i, j)),
        out_shape=jax.ShapeDtypeStruct((m, n), out_dtype),
        scratch_shapes=[pltpu.VMEM((tm, d), BF16)],
        compiler_params=_params("arbitrary", "arbitrary"),
        name=name,
    )(x, g.reshape(1, d), w)


def _rope_pair(x, cos, sin_signed):
    lane = lax.broadcasted_iota(I32, (1, LANES), 1)
    first_half = (lane % ATT_HEAD_DIM) < (ATT_HEAD_DIM // 2)
    rot = jnp.where(first_half, pltpu.roll(x, LANES - ATT_HEAD_DIM // 2, 1), pltpu.roll(x, ATT_HEAD_DIM // 2, 1))
    return x * cos + rot * sin_signed


def _attn_kernel(sink_ref, q_ref, k_ref, v_ref, cq_ref, sq_ref, ck_ref, sk_ref, o_ref, kr_ref, vr_ref, *, seq, tq):
    qi = pl.program_id(1)
    win = tq + 2 * WINDOW

    @pl.when(qi == 0)
    def _():
        for j in range(KV_WIDTH // LANES):
            sl = slice(j * LANES, (j + 1) * LANES)
            kr_ref[:, sl] = _rope_pair(k_ref[:, sl], ck_ref[...], sk_ref[...]).astype(BF16)
        vr_ref[...] = v_ref[...].astype(BF16)

    q0 = qi * tq
    ks = pl.multiple_of(jnp.clip(q0 - WINDOW, 0, seq - win), WINDOW)
    kwin = kr_ref[pl.ds(ks, win), :]
    vwin = vr_ref[pl.ds(ks, win), :]
    qpos = q0 + lax.broadcasted_iota(I32, (tq, 1), 0)
    kpos = ks + lax.broadcasted_iota(I32, (1, win), 1)
    valid = jnp.abs(kpos - qpos) <= WINDOW
    cq = cq_ref[...]
    sq = sq_ref[...]
    group = ATT_HEADS // ATT_KV_HEADS
    for j in range(ATT_WIDTH // LANES):
        qs = (_rope_pair(q_ref[:, j * LANES:(j + 1) * LANES], cq, sq) * (ATT_HEAD_DIM ** -0.5)).astype(BF16)
        outs = []
        for hh in range(LANES // ATT_HEAD_DIM):
            h = j * (LANES // ATT_HEAD_DIM) + hh
            g = h // group
            qh = qs[:, hh * ATT_HEAD_DIM:(hh + 1) * ATT_HEAD_DIM]
            kh = kwin[:, g * ATT_HEAD_DIM:(g + 1) * ATT_HEAD_DIM]
            vh = vwin[:, g * ATT_HEAD_DIM:(g + 1) * ATT_HEAD_DIM]
            s = lax.dot_general(qh, kh, NT_DIMS, preferred_element_type=F32)
            s = jnp.where(valid, s, -1e30)
            sk = sink_ref[h]
            m = jnp.maximum(jnp.max(s, axis=-1, keepdims=True), sk)
            p = jnp.exp(s - m)
            denom = jnp.sum(p, axis=-1, keepdims=True) + jnp.exp(sk - m)
            o = jnp.dot(p.astype(BF16), vh, preferred_element_type=F32)
            outs.append(o / denom)
        o_ref[:, j * LANES:(j + 1) * LANES] = jnp.concatenate(outs, axis=-1).astype(o_ref.dtype)


def _window_attention(z, sink, cos, sin_signed, batch, seq, tq):
    m = z.shape[0]
    nq = seq // tq
    qblk = COL_Q_ATT // ATT_WIDTH
    kblk = COL_K_ATT // KV_WIDTH
    vblk = COL_V_ATT // KV_WIDTH
    return pl.pallas_call(
        functools.partial(_attn_kernel, seq=seq, tq=tq),
        grid=(batch, nq),
        in_specs=[pl.BlockSpec(memory_space=pltpu.SMEM),
                  pl.BlockSpec((tq, ATT_WIDTH), lambda b, i: (b * nq + i, qblk)),
                  pl.BlockSpec((seq, KV_WIDTH), lambda b, i: (b, kblk)),
                  pl.BlockSpec((seq, KV_WIDTH), lambda b, i: (b, vblk)),
                  pl.BlockSpec((tq, LANES), lambda b, i: (i, 0)),
                  pl.BlockSpec((tq, LANES), lambda b, i: (i, 0)),
                  pl.BlockSpec((seq, LANES), lambda b, i: (0, 0)),
                  pl.BlockSpec((seq, LANES), lambda b, i: (0, 0))],
        out_specs=pl.BlockSpec((tq, ATT_WIDTH), lambda b, i: (b * nq + i, 0)),
        out_shape=jax.ShapeDtypeStruct((m, ATT_WIDTH), BF16),
        scratch_shapes=[pltpu.VMEM((seq, KV_WIDTH), BF16), pltpu.VMEM((seq, KV_WIDTH), BF16)],
        compiler_params=_params("arbitrary", "arbitrary"),
        name="window_attention",
    )(sink, z, z, z, cos, sin_signed, cos, sin_signed)


def _split_cumsum(tri, x):
    hi = x.astype(BF16)
    r1 = x - hi.astype(F32)
    mid = r1.astype(BF16)
    lo = (r1 - mid.astype(F32)).astype(BF16)
    dot = functools.partial(jnp.dot, preferred_element_type=F32)
    return dot(tri, hi) + dot(tri, mid) + dot(tri, lo)


def _hgrn_group(zq, zf, v, lbv, state, mask, tri, forward):
    rows = zq.shape[0]
    c = HG_CHUNK
    nchunk = rows // c
    q = zq * _sigmoid(zq) * (HG_DIM ** -0.5)
    f = lbv + (1.0 - lbv) * _sigmoid(zf)
    logf = jnp.log(f)
    k = 1.0 - f
    b = _split_cumsum(tri, logf)
    ref_row = c // 2 - 1 if forward else c // 2
    last_row = c - 1 if forward else 0
    qe, ke, kd, qb, dec = [], [], [], [], []
    for n in range(nchunk):
        sl = slice(n * c, (n + 1) * c)
        bn, qn, kn = b[sl], q[sl], k[sl]
        bref = bn[ref_row:ref_row + 1]
        blast = bn[last_row:last_row + 1]
        qe.append(qn * jnp.exp(bn - bref))
        ke.append(kn * jnp.exp(bref - bn))
        kd.append((kn * jnp.exp(blast - bn)).astype(BF16))
        qb.append((qn * jnp.exp(bn)).astype(BF16))
        dec.append(jnp.exp(blast))
    qe = jnp.concatenate(qe, axis=0).astype(BF16)
    ke = jnp.concatenate(ke, axis=0).astype(BF16)
    vb = v.astype(BF16)
    a = lax.dot_general(qe, ke, NT_DIMS, preferred_element_type=F32)
    a = jnp.where(mask, a, 0.0).astype(BF16)
    o = jnp.dot(a, vb, preferred_element_type=F32)
    parts = [None] * nchunk
    order = range(nchunk) if forward else range(nchunk - 1, -1, -1)
    for n in order:
        sl = slice(n * c, (n + 1) * c)
        parts[n] = o[sl] + lax.dot_general(qb[n], state.astype(BF16), NT_DIMS, preferred_element_type=F32)
        u = lax.dot_general(vb[sl], kd[n], TN_DIMS, preferred_element_type=F32)
        state = state * dec[n] + u
    return jnp.concatenate(parts, axis=0), state


def _hgrn_kernel(zq_ref, zff_ref, zfb_ref, zi_ref, zg_ref, lb_ref, gn_ref, o_ref, accf_ref, accb_ref, *, seq, rows):
    ngroups = seq // rows
    r = lax.broadcasted_iota(I32, (rows, rows), 0)
    cidx = lax.broadcasted_iota(I32, (rows, rows), 1)
    same = (r // HG_CHUNK) == (cidx // HG_CHUNK)
    mask_f = jnp.logical_and(same, cidx <= r)
    mask_b = jnp.logical_and(same, cidx >= r)
    tri_f = jnp.where(mask_f, 1.0, 0.0).astype(BF16)
    tri_b = jnp.where(mask_b, 1.0, 0.0).astype(BF16)
    lb_f = lb_ref[0:1, :]
    lb_b = lb_ref[1:2, :]
    zero_state = jnp.zeros((HG_DIM, HG_DIM), F32)

    def scan_body(it, states):
        sf, sb = states
        slf = pl.ds(pl.multiple_of(it * rows, rows), rows)
        slb = pl.ds(pl.multiple_of((ngroups - 1 - it) * rows, rows), rows)
        of, sf = _hgrn_group(zq_ref[slf, :], zff_ref[slf, :], zi_ref[slf, :], lb_f, sf, mask_f, tri_f, True)
        ob, sb = _hgrn_group(zq_ref[slb, :], zfb_ref[slb, :], zi_ref[slb, :], lb_b, sb, mask_b, tri_b, False)
        accf_ref[slf, :] = of
        accb_ref[slb, :] = ob
        return sf, sb

    lax.fori_loop(0, ngroups, scan_body, (zero_state, zero_state))

    def readout_body(gi, carry):
        sl = pl.ds(pl.multiple_of(gi * rows, rows), rows)
        tot = accf_ref[sl, :] + accb_ref[sl, :]
        zg = zg_ref[sl, :]
        y = _rms(tot, gn_ref[...]) * (zg * _sigmoid(zg))
        o_ref[sl, :] = y.astype(o_ref.dtype)
        return carry

    lax.fori_loop(0, ngroups, readout_body, 0)


def _hgrn2(z, lb, hg_norm, batch, seq, rows):
    m = z.shape[0]

    def zspec(col):
        base = col // HG_DIM
        return pl.BlockSpec((seq, HG_DIM), lambda b, h: (b, base + h))

    return pl.pallas_call(
        functools.partial(_hgrn_kernel, seq=seq, rows=rows),
        grid=(batch, HG_HEADS),
        in_specs=[zspec(COL_Q_HG), zspec(COL_F_FWD), zspec(COL_F_BWD), zspec(COL_I_HG), zspec(COL_G_HG),
                  pl.BlockSpec((2, HG_DIM), lambda b, h: (0, h)),
                  pl.BlockSpec((1, HG_DIM), lambda b, h: (0, 0))],
        out_specs=pl.BlockSpec((seq, HG_DIM), lambda b, h: (b, h)),
        out_shape=jax.ShapeDtypeStruct((m, HG_WIDTH), BF16),
        scratch_shapes=[pltpu.VMEM((seq, HG_DIM), F32), pltpu.VMEM((seq, HG_DIM), F32)],
        compiler_params=_params("arbitrary", "arbitrary"),
        name="hgrn2",
    )(z, z, z, z, z, lb, hg_norm.reshape(1, HG_DIM))


def _outproj_kernel(a_ref, h_ref, x_ref, wa_ref, wh_ref, o_ref):
    acc = jnp.dot(a_ref[...], wa_ref[...], preferred_element_type=F32)
    acc = acc + jnp.dot(h_ref[...], wh_ref[...], preferred_element_type=F32)
    o_ref[...] = x_ref[...] + acc


def _out_projection(att, hg, x, w_out, tm):
    m, d = x.shape
    return pl.pallas_call(
        _outproj_kernel,
        grid=(m // tm,),
        in_specs=[pl.BlockSpec((tm, ATT_WIDTH), lambda i: (i, 0)),
                  pl.BlockSpec((tm, HG_WIDTH), lambda i: (i, 0)),
                  pl.BlockSpec((tm, d), lambda i: (i, 0)),
                  pl.BlockSpec((ATT_WIDTH, d), lambda i: (0, 0)),
                  pl.BlockSpec((HG_WIDTH, d), lambda i: (ATT_WIDTH // HG_WIDTH, 0))],
        out_specs=pl.BlockSpec((tm, d), lambda i: (i, 0)),
        out_shape=jax.ShapeDtypeStruct((m, d), F32),
        compiler_params=_params("arbitrary"),
        name="out_projection",
    )(att, hg, x, w_out, w_out)


def _cross_kernel(x_ref, g_ref, kv_ref, wq_ref, wo_ref, o_ref):
    x = x_ref[...]
    h = _rms(x, g_ref[...]).astype(BF16)
    q = jnp.dot(h, wq_ref[...], preferred_element_type=F32).astype(BF16)
    outs = []
    for hd in range(X_HEADS):
        sl = slice(hd * X_HEAD_DIM, (hd + 1) * X_HEAD_DIM)
        kh = kv_ref[:, sl]
        vh = kv_ref[:, X_WIDTH + hd * X_HEAD_DIM:X_WIDTH + (hd + 1) * X_HEAD_DIM]
        s = lax.dot_general(q[:, sl], kh, NT_DIMS, preferred_element_type=F32) * (X_HEAD_DIM ** -0.5)
        p = jnp.exp(s - jnp.max(s, axis=-1, keepdims=True))
        denom = jnp.sum(p, axis=-1, keepdims=True)
        outs.append(jnp.dot(p.astype(BF16), vh, preferred_element_type=F32) / denom)
    oc = jnp.concatenate(outs, axis=-1).astype(BF16)
    o_ref[...] = x + jnp.dot(oc, wo_ref[...], preferred_element_type=F32)


def _cross_attention(x, g, kv, w_xq, w_xo, batch, seq, n_mem, tm):
    m, d = x.shape
    nt = seq // tm
    return pl.pallas_call(
        _cross_kernel,
        grid=(batch, nt),
        in_specs=[pl.BlockSpec((tm, d), lambda b, i: (b * nt + i, 0)),
                  pl.BlockSpec((1, d), lambda b, i: (0, 0)),
                  pl.BlockSpec((n_mem, 2 * X_WIDTH), lambda b, i: (b, 0)),
                  pl.BlockSpec((d, X_WIDTH), lambda b, i: (0, 0)),
                  pl.BlockSpec((X_WIDTH, d), lambda b, i: (0, 0))],
        out_specs=pl.BlockSpec((tm, d), lambda b, i: (b * nt + i, 0)),
        out_shape=jax.ShapeDtypeStruct((m, d), F32),
        compiler_params=_params("arbitrary", "arbitrary"),
        name="cross_attention",
    )(x, g.reshape(1, d), kv, w_xq, w_xo)


def _router_kernel(x_ref, g_ref, wr_ref, br_ref, h_ref, idx_ref, gate_ref, rank_ref, cnt_ref, base_ref):
    tm = x_ref.shape[0]

    @pl.when(pl.program_id(0) == 0)
    def _():
        base_ref[...] = jnp.zeros_like(base_ref)

    h = _rms(x_ref[...], g_ref[...])
    h_ref[...] = h
    hh = h.astype(BF16)
    hl = (h - hh.astype(F32)).astype(BF16)
    w = wr_ref[...]
    wh = w.astype(BF16)
    wl = (w - wh.astype(F32)).astype(BF16)
    nt = functools.partial(lax.dot_general, dimension_numbers=NT_DIMS, preferred_element_type=F32)
    logits = nt(wh, hh) + nt(wh, hl) + nt(wl, hh) + br_ref[...]

    eio = lax.broadcasted_iota(I32, (N_EXPERTS, tm), 0).astype(F32)
    work = logits
    vals, onehots = [], []
    for k in range(TOP_K):
        mx = jnp.max(work, axis=0, keepdims=True)
        ix = jnp.min(jnp.where(work == mx, eio, float(N_EXPERTS)), axis=0, keepdims=True)
        sel = eio == ix
        vals.append(mx)
        onehots.append(sel)
        idx_ref[k:k + 1, :] = ix.astype(I32)
        work = jnp.where(sel, -jnp.inf, work)
    ex = [jnp.exp(v - vals[0]) for v in vals]
    denom = ex[0] + ex[1] + ex[2] + ex[3]
    for k in range(TOP_K):
        gate_ref[k:k + 1, :] = ex[k] / denom

    oh = [jnp.where(s, 1.0, 0.0) for s in onehots]
    oh_all = oh[0] + oh[1] + oh[2] + oh[3]
    r = lax.broadcasted_iota(I32, (tm, tm), 0)
    c = lax.broadcasted_iota(I32, (tm, tm), 1)
    upper = jnp.where(r < c, 1.0, 0.0).astype(BF16)
    before = jnp.dot(oh_all.astype(BF16), upper, preferred_element_type=F32) + base_ref[...]
    for k in range(TOP_K):
        rank_ref[k:k + 1, :] = jnp.sum(oh[k] * before, axis=0, keepdims=True).astype(I32)
    base_ref[...] = base_ref[...] + jnp.sum(oh_all, axis=1, keepdims=True)
    cnt_ref[...] = base_ref[...]


def _router(x, g, w_router_t, b_router, tm):
    m, d = x.shape
    row4 = pl.BlockSpec((TOP_K, tm), lambda i: (0, i))
    return pl.pallas_call(
        _router_kernel,
        grid=(m // tm,),
        in_specs=[pl.BlockSpec((tm, d), lambda i: (i, 0)),
                  pl.BlockSpec((1, d), lambda i: (0, 0)),
                  pl.BlockSpec((N_EXPERTS, d), lambda i: (0, 0)),
                  pl.BlockSpec((N_EXPERTS, 1), lambda i: (0, 0))],
        out_specs=[pl.BlockSpec((tm, d), lambda i: (i, 0)), row4, row4, row4,
                   pl.BlockSpec((N_EXPERTS, 1), lambda i: (0, 0))],
        out_shape=[jax.ShapeDtypeStruct((m, d), F32),
                   jax.ShapeDtypeStruct((TOP_K, m), I32),
                   jax.ShapeDtypeStruct((TOP_K, m), F32),
                   jax.ShapeDtypeStruct((TOP_K, m), I32),
                   jax.ShapeDtypeStruct((N_EXPERTS, 1), F32)],
        scratch_shapes=[pltpu.VMEM((N_EXPERTS, 1), F32)],
        compiler_params=_params("arbitrary"),
        name="router",
    )(x, g.reshape(1, d), w_router_t, b_router.reshape(N_EXPERTS, 1))


def _assign_rows_kernel(pstart_ref, idx_ref, rank_ref, dest_ref):
    idx = idx_ref[...]
    start = jnp.zeros_like(idx)
    for e in range(N_EXPERTS):
        start = jnp.where(idx == e, pstart_ref[e], start)
    dest_ref[...] = start + rank_ref[...]


def _assign_rows(pstart, idx, rank):
    full = pl.BlockSpec(idx.shape, lambda: (0, 0))
    return pl.pallas_call(
        _assign_rows_kernel,
        in_specs=[pl.BlockSpec(memory_space=pltpu.SMEM), full, full],
        out_specs=full,
        out_shape=jax.ShapeDtypeStruct(idx.shape, I32),
        name="assign_rows",
    )(pstart, idx, rank)


def _dispatch_kernel(seg_ref, dest_hbm, h_ref, xs_hbm, dsm, zrow, sem_idx, sem_row, *, m_tokens, p_rows):
    i = pl.program_id(0)
    tt = SMEM_STAGE

    def idx_copy(k):
        return pltpu.make_async_copy(dest_hbm.at[pl.ds(k * m_tokens + i * tt, tt)],
                                     dsm.at[pl.ds(k * tt, tt)], sem_idx)

    for k in range(TOP_K):
        idx_copy(k).start()
    for k in range(TOP_K):
        idx_copy(k).wait()

    def row_copy(t, dst_row):
        return pltpu.make_async_copy(h_ref.at[pl.ds(t, 1)], xs_hbm.at[pl.ds(dst_row, 1)], sem_row)

    def issue(t, carry):
        for k in range(TOP_K):
            row_copy(t, dsm[k * tt + t]).start()
        return carry

    lax.fori_loop(0, tt, issue, 0, unroll=8)

    for k in range(TOP_K):
        pltpu.make_async_copy(h_ref, xs_hbm.at[pl.ds(0, tt)], sem_row).wait()

    @pl.when(i == 0)
    def _():
        zrow[...] = jnp.zeros_like(zrow)

        def zero_copy(dst_row):
            return pltpu.make_async_copy(zrow.at[pl.ds(0, 1)], xs_hbm.at[pl.ds(dst_row, 1)], sem_row)

        def fill(lo, hi):
            def body(rw, carry):
                zero_copy(rw).start()
                return carry

            lax.fori_loop(lo, hi, body, 0)

            def wbody(rw, carry):
                zero_copy(0).wait()
                return carry

            lax.fori_loop(lo, hi, wbody, 0)

        def per_expert(e, carry):
            fill(seg_ref[0, e] + seg_ref[1, e], seg_ref[2, e])
            return carry

        lax.fori_loop(0, N_EXPERTS, per_expert, 0)
        fill(seg_ref[2, N_EXPERTS - 1], p_rows)


def _dispatch(seg, dest_flat, h, p_rows):
    m, d = h.shape
    return pl.pallas_call(
        functools.partial(_dispatch_kernel, m_tokens=m, p_rows=p_rows),
        grid=(m // SMEM_STAGE,),
        in_specs=[pl.BlockSpec(memory_space=pltpu.SMEM),
                  pl.BlockSpec(memory_space=pl.ANY),
                  pl.BlockSpec((SMEM_STAGE, d), lambda i: (i, 0))],
        out_specs=pl.BlockSpec(memory_space=pl.ANY),
        out_shape=jax.ShapeDtypeStruct((p_rows, d), F32),
        scratch_shapes=[pltpu.SMEM((TOP_K * SMEM_STAGE,), I32),
                        pltpu.VMEM((8, d), F32),
                        pltpu.SemaphoreType.DMA,
                        pltpu.SemaphoreType.DMA],
        compiler_params=_params("arbitrary"),
        name="dispatch",
    )(seg, dest_flat, h)


def _expert_kernel(te_ref, tv_ref, nu_ref, x_ref, w1g_ref, w1l_ref, b1g_ref, b1l_ref, w2_ref, b2_ref,
                   o_ref, xb_ref, acc_ref):
    i = pl.program_id(0)
    f = pl.program_id(1)
    nf = pl.num_programs(1)
    live = tv_ref[i] > 0

    @pl.when(live)
    def _():
        @pl.when(f == 0)
        def _():
            xb_ref[...] = x_ref[...].astype(BF16)

        xb = xb_ref[...]
        glu = jnp.dot(xb, w1g_ref[0], preferred_element_type=F32) + b1g_ref[0]
        lin = jnp.dot(xb, w1l_ref[0], preferred_element_type=F32) + b1l_ref[0]
        glu = jnp.minimum(glu, SWIGLU_LIMIT)
        lin = jnp.clip(lin, -SWIGLU_LIMIT, SWIGLU_LIMIT)
        act = glu * _sigmoid(SWIGLU_ALPHA * glu) * (lin + 1.0)
        part = jnp.dot(act.astype(BF16), w2_ref[0], preferred_element_type=F32)

        @pl.when(f == 0)
        def _():
            acc_ref[...] = part + b2_ref[0]

        @pl.when(f > 0)
        def _():
            acc_ref[...] = acc_ref[...] + part

        @pl.when(f == nf - 1)
        def _():
            o_ref[...] = acc_ref[...]

    @pl.when(jnp.logical_and(jnp.logical_not(live), f == nf - 1))
    def _():
        o_ref[...] = jnp.zeros_like(o_ref)


def _expert_mlp(tile_e, tile_valid, n_used, xs, w1, b1, w2, b2, tmb, tf):
    p_rows, d = xs.shape
    d_ff = w2.shape[1]
    nf = d_ff // tf
    n_tiles = p_rows // tmb

    def fsel(i, f, tv):
        return jnp.where(tv[i] > 0, f, nf - 1)

    grid_spec = pltpu.PrefetchScalarGridSpec(
        num_scalar_prefetch=3,
        grid=(n_tiles, nf),
        in_specs=[
            pl.BlockSpec((tmb, d), lambda i, f, te, tv, nu: (jnp.minimum(i, nu[0] - 1), 0)),
            pl.BlockSpec((1, d, tf), lambda i, f, te, tv, nu: (te[i], 0, fsel(i, f, tv))),
            pl.BlockSpec((1, d, tf), lambda i, f, te, tv, nu: (te[i], 0, nf + fsel(i, f, tv))),
            pl.BlockSpec((1, 1, tf), lambda i, f, te, tv, nu: (te[i], 0, fsel(i, f, tv))),
            pl.BlockSpec((1, 1, tf), lambda i, f, te, tv, nu: (te[i], 0, nf + fsel(i, f, tv))),
            pl.BlockSpec((1, tf, d), lambda i, f, te, tv, nu: (te[i], fsel(i, f, tv), 0)),
            pl.BlockSpec((1, 1, d), lambda i, f, te, tv, nu: (te[i], 0, 0)),
        ],
        out_specs=pl.BlockSpec((tmb, d), lambda i, f, te, tv, nu: (i, 0)),
        scratch_shapes=[pltpu.VMEM((tmb, d), BF16), pltpu.VMEM((tmb, d), F32)],
    )
    return pl.pallas_call(
        _expert_kernel,
        grid_spec=grid_spec,
        out_shape=jax.ShapeDtypeStruct((p_rows, d), F32),
        compiler_params=_params("arbitrary", "arbitrary"),
        name="expert_mlp",
    )(tile_e, tile_valid, n_used, xs, w1, w1, b1, b1, w2, b2)


def _combine_kernel(dest_hbm, ys_hbm, x_ref, gate_ref, g_ref, o_ref, dsm, buf, sem_idx, sem_row, *, m_tokens, tc):
    i = pl.program_id(0)
    per_stage = SMEM_STAGE // tc
    stage = i // per_stage
    off = (i % per_stage) * tc

    @pl.when(i % per_stage == 0)
    def _():
        def idx_copy(k):
            return pltpu.make_async_copy(dest_hbm.at[pl.ds(k * m_tokens + stage * SMEM_STAGE, SMEM_STAGE)],
                                         dsm.at[pl.ds(k * SMEM_STAGE, SMEM_STAGE)], sem_idx)

        for k in range(TOP_K):
            idx_copy(k).start()
        for k in range(TOP_K):
            idx_copy(k).wait()

    def row_copy(k, t, src_row):
        return pltpu.make_async_copy(ys_hbm.at[pl.ds(src_row, 1)], buf.at[k, pl.ds(t, 1)], sem_row)

    def issue(t, carry):
        for k in range(TOP_K):
            row_copy(k, t, dsm[k * SMEM_STAGE + off + t]).start()
        return carry

    lax.fori_loop(0, tc, issue, 0, unroll=8)

    def drain(t, carry):
        for k in range(TOP_K):
            row_copy(k, 0, 0).wait()
        return carry

    lax.fori_loop(0, tc, drain, 0, unroll=8)

    gate = gate_ref[...]
    y = x_ref[...]
    for k in range(TOP_K):
        y = y + buf[k] * gate[:, k:k + 1]
    o_ref[...] = _rms(y, g_ref[...])


def _combine(dest_flat, ys, x, gate_t, g, tc):
    m, d = x.shape
    return pl.pallas_call(
        functools.partial(_combine_kernel, m_tokens=m, tc=tc),
        grid=(m // tc,),
        in_specs=[pl.BlockSpec(memory_space=pl.ANY),
                  pl.BlockSpec(memory_space=pl.ANY),
                  pl.BlockSpec((tc, d), lambda i: (i, 0)),
                  pl.BlockSpec((tc, TOP_K), lambda i: (i, 0)),
                  pl.BlockSpec((1, d), lambda i: (0, 0))],
        out_specs=pl.BlockSpec((tc, d), lambda i: (i, 0)),
        out_shape=jax.ShapeDtypeStruct((m, d), F32),
        scratch_shapes=[pltpu.SMEM((TOP_K * SMEM_STAGE,), I32),
                        pltpu.VMEM((TOP_K, tc, d), F32),
                        pltpu.SemaphoreType.DMA,
                        pltpu.SemaphoreType.DMA],
        compiler_params=_params("arbitrary"),
        name="combine",
    )(dest_flat, ys, x, gate_t, g.reshape(1, d))


def _rope_tables(seq):
    half = ATT_HEAD_DIM // 2
    inv = ROPE_THETA ** (-jnp.arange(0, ATT_HEAD_DIM, 2, dtype=F32) / ATT_HEAD_DIM)
    ang = jnp.arange(seq, dtype=F32)[:, None] * inv[None, :]
    cos = jnp.tile(jnp.cos(ang), (1, LANES // half))
    sin = jnp.sin(ang)
    sin_signed = jnp.tile(jnp.concatenate([-sin, sin], axis=-1), (1, LANES // ATT_HEAD_DIM))
    return cos, sin_signed


def _pick(n, prefs):
    for p in prefs:
        if n % p == 0:
            return p
    return n


def _trunk(x, mem, wts, moe_tile, ff_tile):
    batch, seq, d = x.shape
    n_mem = mem.shape[1]
    m = batch * seq
    x2d = x.reshape(m, d)
    tm = _pick(m, (512, 256, 128))

    n_in = wts["w_in"].shape[1]
    z = _norm_matmul(x2d, wts["norm_mix"], wts["w_in"], tm, _pick(n_in, (1664, 1024, 512, 256, 128)), F32, "in_projection")

    cos, sin_signed = _rope_tables(seq)
    att = _window_attention(z, wts["att_sink"], cos, sin_signed, batch, seq, _pick(seq, (256, 128)))
    hg = _hgrn2(z, wts["lb"], wts["hg_norm"], batch, seq, _pick(seq, (256, 128, 64)))
    x1 = _out_projection(att, hg, x2d, wts["w_out"], tm)

    kv = _norm_matmul(mem.reshape(batch * n_mem, d), wts["norm_mem"], wts["w_xkv"], n_mem, 2 * X_WIDTH, BF16,
                      "memory_kv")
    x2 = _cross_attention(x1, wts["norm_cross"], kv, wts["w_xq"], wts["w_xo"], batch, seq, n_mem,
                          _pick(seq, (512, 256, 128)))

    h3, idx, gate, rank, cnt = _router(x2, wts["norm_ffn"], wts["w_router_t"], wts["b_router"], tm)

    counts = cnt[:, 0].astype(I32)
    padded = (counts + moe_tile - 1) // moe_tile * moe_tile
    pend = jnp.cumsum(padded)
    pstart = pend - padded
    dest = _assign_rows(pstart.astype(I32), idx, rank)
    n_tiles = -(-(m * TOP_K) // moe_tile) + N_EXPERTS
    p_rows = n_tiles * moe_tile
    tile_row = jnp.arange(n_tiles, dtype=I32) * moe_tile
    tile_e = jnp.minimum(jnp.searchsorted(pend, tile_row, side="right"), N_EXPERTS - 1).astype(I32)
    tile_valid = jnp.clip(pstart[tile_e] + counts[tile_e] - tile_row, 0, moe_tile).astype(I32)
    n_used = (pend[-1:] // moe_tile).astype(I32)
    seg = jnp.stack([pstart, counts, pend]).astype(I32)
    dest_flat = dest.reshape(-1)

    xs = _dispatch(seg, dest_flat, h3, p_rows)
    ys = _expert_mlp(tile_e, tile_valid, n_used, xs, wts["w_moe1"], wts["b_moe1"], wts["w_moe2"], wts["b_moe2"],
                     moe_tile, ff_tile)
    y = _combine(dest_flat, ys, x2, gate.T, wts["norm_final"], _pick(m, (256, 128)))
    return y.reshape(batch, seq, d)


def kernel(x_prompt, x_sample, mem_prompt, mem_sample, norm_mix, w_in, att_sink, hg_lb_logits, hg_norm, w_out,
           norm_cross, norm_mem, w_xq, w_xkv, w_xo, norm_ffn, w_router, b_router, w_moe1, b_moe1, w_moe2, b_moe2,
           norm_final):
    assert w_in.shape[0] == 1, "the final norm is fused after the single layer"
    lb_all = jnp.cumsum(jax.nn.softmax(hg_lb_logits.astype(F32), axis=0), axis=0)
    d_ff = w_moe2.shape[2]
    wts = dict(
        norm_mix=norm_mix[0], w_in=w_in[0].astype(BF16), att_sink=att_sink[0], lb=lb_all[0],
        hg_norm=hg_norm[0], w_out=w_out[0].astype(BF16), norm_cross=norm_cross[0], norm_mem=norm_mem[0],
        w_xq=w_xq[0].astype(BF16), w_xkv=w_xkv[0].astype(BF16), w_xo=w_xo[0].astype(BF16),
        norm_ffn=norm_ffn[0], w_router_t=w_router[0].T, b_router=b_router[0],
        w_moe1=w_moe1[0].astype(BF16), b_moe1=b_moe1[0][:, None, :],
        w_moe2=w_moe2[0].astype(BF16), b_moe2=b_moe2[0][:, None, :],
        norm_final=norm_final,
    )
    ff_tile = _pick(d_ff, (512, 256, 128))
    return (_trunk(x_prompt, mem_prompt, wts, MOE_TILE, ff_tile),
            _trunk(x_sample, mem_sample, wts, MOE_TILE, ff_tile))
```

```python
import functools

import jax
import jax.numpy as jnp
from jax import lax
from jax.experimental import pallas as pl
from jax.experimental.pallas import tpu as pltpu

F32 = jnp.float32
BF16 = jnp.bfloat16
I32 = jnp.int32
U32 = jnp.uint32

EPS = 1e-5
ATT_HEADS = 16
ATT_KV_HEADS = 4
ATT_HEAD_DIM = 64
ATT_WIDTH = ATT_HEADS * ATT_HEAD_DIM
KV_WIDTH = ATT_KV_HEADS * ATT_HEAD_DIM
WINDOW = 128
ROPE_THETA = 10000.0
HG_HEADS = 8
HG_DIM = 128
HG_WIDTH = HG_HEADS * HG_DIM
HG_CHUNK = 64
X_HEADS = 4
X_HEAD_DIM = 128
X_WIDTH = X_HEADS * X_HEAD_DIM
N_EXPERTS = 32
TOP_K = 4
SWIGLU_ALPHA = 1.702
SWIGLU_LIMIT = 7.0

COL_Q_ATT = 0
COL_K_ATT = ATT_WIDTH
COL_V_ATT = ATT_WIDTH + KV_WIDTH
COL_Q_HG = ATT_WIDTH + 2 * KV_WIDTH
COL_F_FWD = COL_Q_HG + HG_WIDTH
COL_F_BWD = COL_F_FWD + HG_WIDTH
COL_I_HG = COL_F_BWD + HG_WIDTH
COL_G_HG = COL_I_HG + HG_WIDTH

LANES = 128
VMEM_LIMIT = 56 * 1024 * 1024
SMEM_STAGE = 1024
MOE_TILE = 512

NT_DIMS = (((1,), (1,)), ((), ()))
TN_DIMS = (((0,), (0,)), ((), ()))


def _params(*sem):
    return pltpu.CompilerParams(dimension_semantics=sem, vmem_limit_bytes=VMEM_LIMIT)


def _rms(x, g):
    ms = jnp.mean(x * x, axis=-1, keepdims=True)
    return x * lax.rsqrt(ms + EPS) * g


def _sigmoid(x):
    return 1.0 / (1.0 + jnp.exp(-x))


def _norm_matmul_kernel(x_ref, g_ref, w_ref, o_ref, xn_ref):
    @pl.when(pl.program_id(1) == 0)
    def _():
        xn_ref[...] = _rms(x_ref[...], g_ref[...]).astype(BF16)

    o_ref[...] = jnp.dot(xn_ref[...], w_ref[...], preferred_element_type=F32).astype(o_ref.dtype)


def _norm_matmul(x, g, w, tm, tn, out_dtype, name):
    m, d = x.shape
    n = w.shape[1]
    return pl.pallas_call(
        _norm_matmul_kernel,
        grid=(m // tm, n // tn),
        in_specs=[pl.BlockSpec((tm, d), lambda i, j: (i, 0)),
                  pl.BlockSpec((1, d), lambda i, j: (0, 0)),
                  pl.BlockSpec((d, tn), lambda i, j: (0, j))],
        out_specs=pl.BlockSpec((tm, tn), lambda i, j: (i, j)),
        out_shape=jax.ShapeDtypeStruct((m, n), out_dtype),
        scratch_shapes=[pltpu.VMEM((tm, d), BF16)],
        compiler_params=_params("arbitrary", "arbitrary"),
        name=name,
    )(x, g.reshape(1, d), w)


def _rope_pair(x, cos, sin_signed):
    lane = lax.broadcasted_iota(I32, (1, LANES), 1)
    first_half = (lane % ATT_HEAD_DIM) < (ATT_HEAD_DIM // 2)
    rot = jnp.where(first_half, pltpu.roll(x, LANES - ATT_HEAD_DIM // 2, 1), pltpu.roll(x, ATT_HEAD_DIM // 2, 1))
    return x * cos + rot * sin_signed


def _attn_kernel(sink_ref, q_ref, k_ref, v_ref, cq_ref, sq_ref, ck_ref, sk_ref, o_ref, kr_ref, vr_ref, *, seq, tq):
    qi = pl.program_id(1)
    win = tq + 2 * WINDOW

    @pl.when(qi == 0)
    def _():
        for j in range(KV_WIDTH // LANES):
            sl = slice(j * LANES, (j + 1) * LANES)
            kr_ref[:, sl] = _rope_pair(k_ref[:, sl], ck_ref[...], sk_ref[...]).astype(BF16)
        vr_ref[...] = v_ref[...].astype(BF16)

    q0 = qi * tq
    ks = pl.multiple_of(jnp.clip(q0 - WINDOW, 0, seq - win), WINDOW)
    kwin = kr_ref[pl.ds(ks, win), :]
    vwin = vr_ref[pl.ds(ks, win), :]
    qpos = q0 + lax.broadcasted_iota(I32, (tq, 1), 0)
    kpos = ks + lax.broadcasted_iota(I32, (1, win), 1)
    valid = jnp.abs(kpos - qpos) <= WINDOW
    cq = cq_ref[...]
    sq = sq_ref[...]
    group = ATT_HEADS // ATT_KV_HEADS
    for j in range(ATT_WIDTH // LANES):
        qs = (_rope_pair(q_ref[:, j * LANES:(j + 1) * LANES], cq, sq) * (ATT_HEAD_DIM ** -0.5)).astype(BF16)
        outs = []
        for hh in range(LANES // ATT_HEAD_DIM):
            h = j * (LANES // ATT_HEAD_DIM) + hh
            g = h // group
            qh = qs[:, hh * ATT_HEAD_DIM:(hh + 1) * ATT_HEAD_DIM]
            kh = kwin[:, g * ATT_HEAD_DIM:(g + 1) * ATT_HEAD_DIM]
            vh = vwin[:, g * ATT_HEAD_DIM:(g + 1) * ATT_HEAD_DIM]
            s = lax.dot_general(qh, kh, NT_DIMS, preferred_element_type=F32)
            s = jnp.where(valid, s, -1e30)
            sk = sink_ref[h]
            m = jnp.maximum(jnp.max(s, axis=-1, keepdims=True), sk)
            p = jnp.exp(s - m)
            denom = jnp.sum(p, axis=-1, keepdims=True) + jnp.exp(sk - m)
            o = jnp.dot(p.astype(BF16), vh, preferred_element_type=F32)
            outs.append(o / denom)
        o_ref[:, j * LANES:(j + 1) * LANES] = jnp.concatenate(outs, axis=-1).astype(o_ref.dtype)


def _window_attention(z, sink, cos, sin_signed, batch, seq, tq):
    m = z.shape[0]
    nq = seq // tq
    qblk = COL_Q_ATT // ATT_WIDTH
    kblk = COL_K_ATT // KV_WIDTH
    vblk = COL_V_ATT // KV_WIDTH
    return pl.pallas_call(
        functools.partial(_attn_kernel, seq=seq, tq=tq),
        grid=(batch, nq),
        in_specs=[pl.BlockSpec(memory_space=pltpu.SMEM),
                  pl.BlockSpec((tq, ATT_WIDTH), lambda b, i: (b * nq + i, qblk)),
                  pl.BlockSpec((seq, KV_WIDTH), lambda b, i: (b, kblk)),
                  pl.BlockSpec((seq, KV_WIDTH), lambda b, i: (b, vblk)),
                  pl.BlockSpec((tq, LANES), lambda b, i: (i, 0)),
                  pl.BlockSpec((tq, LANES), lambda b, i: (i, 0)),
                  pl.BlockSpec((seq, LANES), lambda b, i: (0, 0)),
                  pl.BlockSpec((seq, LANES), lambda b, i: (0, 0))],
        out_specs=pl.BlockSpec((tq, ATT_WIDTH), lambda b, i: (b * nq + i, 0)),
        out_shape=jax.ShapeDtypeStruct((m, ATT_WIDTH), BF16),
        scratch_shapes=[pltpu.VMEM((seq, KV_WIDTH), BF16), pltpu.VMEM((seq, KV_WIDTH), BF16)],
        compiler_params=_params("arbitrary", "arbitrary"),
        name="window_attention",
    )(sink, z, z, z, cos, sin_signed, cos, sin_signed)


def _split_cumsum(tri, x):
    hi = x.astype(BF16)
    r1 = x - hi.astype(F32)
    mid = r1.astype(BF16)
    lo = (r1 - mid.astype(F32)).astype(BF16)
    dot = functools.partial(jnp.dot, preferred_element_type=F32)
    return dot(tri, hi) + dot(tri, mid) + dot(tri, lo)


def _hgrn_group(zq, zf, v, lbv, state, mask, tri, forward):
    rows = zq.shape[0]
    c = HG_CHUNK
    nchunk = rows // c
    q = zq * _sigmoid(zq) * (HG_DIM ** -0.5)
    f = lbv + (1.0 - lbv) * _sigmoid(zf)
    logf = jnp.log(f)
    k = 1.0 - f
    b = _split_cumsum(tri, logf)
    ref_row = c // 2 - 1 if forward else c // 2
    last_row = c - 1 if forward else 0
    qe, ke, kd, qb, dec = [], [], [], [], []
    for n in range(nchunk):
        sl = slice(n * c, (n + 1) * c)
        bn, qn, kn = b[sl], q[sl], k[sl]
        bref = bn[ref_row:ref_row + 1]
        blast = bn[last_row:last_row + 1]
        qe.append(qn * jnp.exp(bn - bref))
        ke.append(kn * jnp.exp(bref - bn))
        kd.append((kn * jnp.exp(blast - bn)).astype(BF16))
        qb.append((qn * jnp.exp(bn)).astype(BF16))
        dec.append(jnp.exp(blast))
    qe = jnp.concatenate(qe, axis=0).astype(BF16)
    ke = jnp.concatenate(ke, axis=0).astype(BF16)
    vb = v.astype(BF16)
    a = lax.dot_general(qe, ke, NT_DIMS, preferred_element_type=F32)
    a = jnp.where(mask, a, 0.0).astype(BF16)
    o = jnp.dot(a, vb, preferred_element_type=F32)
    parts = [None] * nchunk
    order = range(nchunk) if forward else range(nchunk - 1, -1, -1)
    for n in order:
        sl = slice(n * c, (n + 1) * c)
        parts[n] = o[sl] + lax.dot_general(qb[n], state.astype(BF16), NT_DIMS, preferred_element_type=F32)
        u = lax.dot_general(vb[sl], kd[n], TN_DIMS, preferred_element_type=F32)
        state = state * dec[n] + u
    return jnp.concatenate(parts, axis=0), state


def _hgrn_kernel(zq_ref, zff_ref, zfb_ref, zi_ref, zg_ref, lb_ref, gn_ref, o_ref, accf_ref, accb_ref, *, seq, rows):
    ngroups = seq // rows
    r = lax.broadcasted_iota(I32, (rows, rows), 0)
    cidx = lax.broadcasted_iota(I32, (rows, rows), 1)
    same = (r // HG_CHUNK) == (cidx // HG_CHUNK)
    mask_f = jnp.logical_and(same, cidx <= r)
    mask_b = jnp.logical_and(same, cidx >= r)
    tri_f = jnp.where(mask_f, 1.0, 0.0).astype(BF16)
    tri_b = jnp.where(mask_b, 1.0, 0.0).astype(BF16)
    lb_f = lb_ref[0:1, :]
    lb_b = lb_ref[1:2, :]
    zero_state = jnp.zeros((HG_DIM, HG_DIM), F32)

    def scan_body(it, states):
        sf, sb = states
        slf = pl.ds(pl.multiple_of(it * rows, rows), rows)
        slb = pl.ds(pl.multiple_of((ngroups - 1 - it) * rows, rows), rows)
        of, sf = _hgrn_group(zq_ref[slf, :], zff_ref[slf, :], zi_ref[slf, :], lb_f, sf, mask_f, tri_f, True)
        ob, sb = _hgrn_group(zq_ref[slb, :], zfb_ref[slb, :], zi_ref[slb, :], lb_b, sb, mask_b, tri_b, False)
        accf_ref[slf, :] = of
        accb_ref[slb, :] = ob
        return sf, sb

    lax.fori_loop(0, ngroups, scan_body, (zero_state, zero_state))

    def readout_body(gi, carry):
        sl = pl.ds(pl.multiple_of(gi * rows, rows), rows)
        tot = accf_ref[sl, :] + accb_ref[sl, :]
        zg = zg_ref[sl, :]
        y = _rms(tot, gn_ref[...]) * (zg * _sigmoid(zg))
        o_ref[sl, :] = y.astype(o_ref.dtype)
        return carry

    lax.fori_loop(0, ngroups, readout_body, 0)


def _hgrn2(z, lb, hg_norm, batch, seq, rows):
    m = z.shape[0]

    def zspec(col):
        base = col // HG_DIM
        return pl.BlockSpec((seq, HG_DIM), lambda b, h: (b, base + h))

    return pl.pallas_call(
        functools.partial(_hgrn_kernel, seq=seq, rows=rows),
        grid=(batch, HG_HEADS),
        in_specs=[zspec(COL_Q_HG), zspec(COL_F_FWD), zspec(COL_F_BWD), zspec(COL_I_HG), zspec(COL_G_HG),
                  pl.BlockSpec((2, HG_DIM), lambda b, h: (0, h)),
                  pl.BlockSpec((1, HG_DIM), lambda b, h: (0, 0))],
        out_specs=pl.BlockSpec((seq, HG_DIM), lambda b, h: (b, h)),
        out_shape=jax.ShapeDtypeStruct((m, HG_WIDTH), BF16),
        scratch_shapes=[pltpu.VMEM((seq, HG_DIM), F32), pltpu.VMEM((seq, HG_DIM), F32)],
        compiler_params=_params("arbitrary", "arbitrary"),
        name="hgrn2",
    )(z, z, z, z, z, lb, hg_norm.reshape(1, HG_DIM))


def _outproj_kernel(a_ref, h_ref, x_ref, wa_ref, wh_ref, o_ref):
    acc = jnp.dot(a_ref[...], wa_ref[...], preferred_element_type=F32)
    acc = acc + jnp.dot(h_ref[...], wh_ref[...], preferred_element_type=F32)
    o_ref[...] = x_ref[...] + acc


def _out_projection(att, hg, x, w_out, tm):
    m, d = x.shape
    return pl.pallas_call(
        _outproj_kernel,
        grid=(m // tm,),
        in_specs=[pl.BlockSpec((tm, ATT_WIDTH), lambda i: (i, 0)),
                  pl.BlockSpec((tm, HG_WIDTH), lambda i: (i, 0)),
                  pl.BlockSpec((tm, d), lambda i: (i, 0)),
                  pl.BlockSpec((ATT_WIDTH, d), lambda i: (0, 0)),
                  pl.BlockSpec((HG_WIDTH, d), lambda i: (ATT_WIDTH // HG_WIDTH, 0))],
        out_specs=pl.BlockSpec((tm, d), lambda i: (i, 0)),
        out_shape=jax.ShapeDtypeStruct((m, d), F32),
        compiler_params=_params("arbitrary"),
        name="out_projection",
    )(att, hg, x, w_out, w_out)


def _cross_kernel(x_ref, g_ref, kv_ref, wq_ref, wo_ref, o_ref):
    x = x_ref[...]
    h = _rms(x, g_ref[...]).astype(BF16)
    q = jnp.dot(h, wq_ref[...], preferred_element_type=F32).astype(BF16)
    outs = []
    for hd in range(X_HEADS):
        sl = slice(hd * X_HEAD_DIM, (hd + 1) * X_HEAD_DIM)
        kh = kv_ref[:, sl]
        vh = kv_ref[:, X_WIDTH + hd * X_HEAD_DIM:X_WIDTH + (hd + 1) * X_HEAD_DIM]
        s = lax.dot_general(q[:, sl], kh, NT_DIMS, preferred_element_type=F32) * (X_HEAD_DIM ** -0.5)
        p = jnp.exp(s - jnp.max(s, axis=-1, keepdims=True))
        denom = jnp.sum(p, axis=-1, keepdims=True)
        outs.append(jnp.dot(p.astype(BF16), vh, preferred_element_type=F32) / denom)
    oc = jnp.concatenate(outs, axis=-1).astype(BF16)
    o_ref[...] = x + jnp.dot(oc, wo_ref[...], preferred_element_type=F32)


def _cross_attention(x, g, kv, w_xq, w_xo, batch, seq, n_mem, tm):
    m, d = x.shape
    nt = seq // tm
    return pl.pallas_call(
        _cross_kernel,
        grid=(batch, nt),
        in_specs=[pl.BlockSpec((tm, d), lambda b, i: (b * nt + i, 0)),
                  pl.BlockSpec((1, d), lambda b, i: (0, 0)),
                  pl.BlockSpec((n_mem, 2 * X_WIDTH), lambda b, i: (b, 0)),
                  pl.BlockSpec((d, X_WIDTH), lambda b, i: (0, 0)),
                  pl.BlockSpec((X_WIDTH, d), lambda b, i: (0, 0))],
        out_specs=pl.BlockSpec((tm, d), lambda b, i: (b * nt + i, 0)),
        out_shape=jax.ShapeDtypeStruct((m, d), F32),
        compiler_params=_params("arbitrary", "arbitrary"),
        name="cross_attention",
    )(x, g.reshape(1, d), kv, w_xq, w_xo)


def _router_kernel(x_ref, g_ref, wr_ref, br_ref, h_ref, idx_ref, gate_ref, rank_ref, cnt_ref, base_ref):
    tm = x_ref.shape[0]

    @pl.when(pl.program_id(0) == 0)
    def _():
        base_ref[...] = jnp.zeros_like(base_ref)

    h = _rms(x_ref[...], g_ref[...])
    hh = h.astype(BF16)
    half = h.shape[1] // 2
    bits = lax.bitcast_convert_type(hh.astype(F32), U32)
    h_ref[...] = (bits[:, :half] >> 16) | (bits[:, half:] & jnp.uint32(0xFFFF0000))
    hl = (h - hh.astype(F32)).astype(BF16)
    w = wr_ref[...]
    wh = w.astype(BF16)
    wl = (w - wh.astype(F32)).astype(BF16)
    nt = functools.partial(lax.dot_general, dimension_numbers=NT_DIMS, preferred_element_type=F32)
    logits = nt(wh, hh) + nt(wh, hl) + nt(wl, hh) + br_ref[...]

    eio = lax.broadcasted_iota(I32, (N_EXPERTS, tm), 0).astype(F32)
    work = logits
    vals, onehots = [], []
    for k in range(TOP_K):
        mx = jnp.max(work, axis=0, keepdims=True)
        ix = jnp.min(jnp.where(work == mx, eio, float(N_EXPERTS)), axis=0, keepdims=True)
        sel = eio == ix
        vals.append(mx)
        onehots.append(sel)
        idx_ref[k:k + 1, :] = ix.astype(I32)
        work = jnp.where(sel, -jnp.inf, work)
    ex = [jnp.exp(v - vals[0]) for v in vals]
    denom = ex[0] + ex[1] + ex[2] + ex[3]
    for k in range(TOP_K):
        gate_ref[k:k + 1, :] = ex[k] / denom

    oh = [jnp.where(s, 1.0, 0.0) for s in onehots]
    oh_all = oh[0] + oh[1] + oh[2] + oh[3]
    r = lax.broadcasted_iota(I32, (tm, tm), 0)
    c = lax.broadcasted_iota(I32, (tm, tm), 1)
    upper = jnp.where(r < c, 1.0, 0.0).astype(BF16)
    before = jnp.dot(oh_all.astype(BF16), upper, preferred_element_type=F32) + base_ref[...]
    for k in range(TOP_K):
        rank_ref[k:k + 1, :] = jnp.sum(oh[k] * before, axis=0, keepdims=True).astype(I32)
    base_ref[...] = base_ref[...] + jnp.sum(oh_all, axis=1, keepdims=True)
    cnt_ref[...] = base_ref[...]


def _router(x, g, w_router_t, b_router, tm):
    m, d = x.shape
    row4 = pl.BlockSpec((TOP_K, tm), lambda i: (0, i))
    return pl.pallas_call(
        _router_kernel,
        grid=(m // tm,),
        in_specs=[pl.BlockSpec((tm, d), lambda i: (i, 0)),
                  pl.BlockSpec((1, d), lambda i: (0, 0)),
                  pl.BlockSpec((N_EXPERTS, d), lambda i: (0, 0)),
                  pl.BlockSpec((N_EXPERTS, 1), lambda i: (0, 0))],
        out_specs=[pl.BlockSpec((tm, d // 2), lambda i: (i, 0)), row4, row4, row4,
                   pl.BlockSpec((N_EXPERTS, 1), lambda i: (0, 0))],
        out_shape=[jax.ShapeDtypeStruct((m, d // 2), U32),
                   jax.ShapeDtypeStruct((TOP_K, m), I32),
                   jax.ShapeDtypeStruct((TOP_K, m), F32),
                   jax.ShapeDtypeStruct((TOP_K, m), I32),
                   jax.ShapeDtypeStruct((N_EXPERTS, 1), F32)],
        scratch_shapes=[pltpu.VMEM((N_EXPERTS, 1), F32)],
        compiler_params=_params("arbitrary"),
        name="router",
    )(x, g.reshape(1, d), w_router_t, b_router.reshape(N_EXPERTS, 1))


def _assign_rows_kernel(pstart_ref, idx_ref, rank_ref, dest_ref):
    idx = idx_ref[...]
    start = jnp.zeros_like(idx)
    for e in range(N_EXPERTS):
        start = jnp.where(idx == e, pstart_ref[e], start)
    dest_ref[...] = start + rank_ref[...]


def _assign_rows(pstart, idx, rank):
    full = pl.BlockSpec(idx.shape, lambda: (0, 0))
    return pl.pallas_call(
        _assign_rows_kernel,
        in_specs=[pl.BlockSpec(memory_space=pltpu.SMEM), full, full],
        out_specs=full,
        out_shape=jax.ShapeDtypeStruct(idx.shape, I32),
        name="assign_rows",
    )(pstart, idx, rank)


def _dispatch_kernel(seg_ref, dest_hbm, h_ref, xs_hbm, dsm, zrow, sem_idx, sem_row, *, m_tokens, p_rows):
    i = pl.program_id(0)
    tt = SMEM_STAGE

    def idx_copy(k):
        return pltpu.make_async_copy(dest_hbm.at[pl.ds(k * m_tokens + i * tt, tt)],
                                     dsm.at[pl.ds(k * tt, tt)], sem_idx)

    for k in range(TOP_K):
        idx_copy(k).start()
    for k in range(TOP_K):
        idx_copy(k).wait()

    def row_copy(t, dst_row):
        return pltpu.make_async_copy(h_ref.at[pl.ds(t, 1)], xs_hbm.at[pl.ds(dst_row, 1)], sem_row)

    def issue(t, carry):
        for k in range(TOP_K):
            row_copy(t, dsm[k * tt + t]).start()
        return carry

    lax.fori_loop(0, tt, issue, 0, unroll=8)

    for k in range(TOP_K):
        pltpu.make_async_copy(h_ref, xs_hbm.at[pl.ds(0, tt)], sem_row).wait()

    @pl.when(i == 0)
    def _():
        zrow[...] = jnp.zeros_like(zrow)

        def zero_copy(dst_row):
            return pltpu.make_async_copy(zrow.at[pl.ds(0, 1)], xs_hbm.at[pl.ds(dst_row, 1)], sem_row)

        def fill(lo, hi):
            def body(rw, carry):
                zero_copy(rw).start()
                return carry

            lax.fori_loop(lo, hi, body, 0)

            def wbody(rw, carry):
                zero_copy(0).wait()
                return carry

            lax.fori_loop(lo, hi, wbody, 0)

        def per_expert(e, carry):
            fill(seg_ref[0, e] + seg_ref[1, e], seg_ref[2, e])
            return carry

        lax.fori_loop(0, N_EXPERTS, per_expert, 0)
        fill(seg_ref[2, N_EXPERTS - 1], p_rows)


def _dispatch(seg, dest_flat, h, p_rows):
    m, d = h.shape
    return pl.pallas_call(
        functools.partial(_dispatch_kernel, m_tokens=m, p_rows=p_rows),
        grid=(m // SMEM_STAGE,),
        in_specs=[pl.BlockSpec(memory_space=pltpu.SMEM),
                  pl.BlockSpec(memory_space=pl.ANY),
                  pl.BlockSpec((SMEM_STAGE, d), lambda i: (i, 0))],
        out_specs=pl.BlockSpec(memory_space=pl.ANY),
        out_shape=jax.ShapeDtypeStruct((p_rows, d), h.dtype),
        scratch_shapes=[pltpu.SMEM((TOP_K * SMEM_STAGE,), I32),
                        pltpu.VMEM((8, d), h.dtype),
                        pltpu.SemaphoreType.DMA,
                        pltpu.SemaphoreType.DMA],
        compiler_params=_params("arbitrary"),
        name="dispatch",
    )(seg, dest_flat, h)


def _expert_kernel(te_ref, tv_ref, nu_ref, x_ref, w1g_ref, w1l_ref, b1g_ref, b1l_ref, w2_ref, b2_ref,
                   o_ref, xb_ref, acc_ref):
    i = pl.program_id(0)
    f = pl.program_id(1)
    nf = pl.num_programs(1)
    live = tv_ref[i] > 0

    @pl.when(live)
    def _():
        @pl.when(f == 0)
        def _():
            words = x_ref[...]
            half = words.shape[1]
            xb_ref[:, :half] = lax.bitcast_convert_type(words << 16, F32).astype(BF16)
            xb_ref[:, half:] = lax.bitcast_convert_type(words & jnp.uint32(0xFFFF0000), F32).astype(BF16)

        xb = xb_ref[...]
        glu = jnp.dot(xb, w1g_ref[0], preferred_element_type=F32) + b1g_ref[0]
        lin = jnp.dot(xb, w1l_ref[0], preferred_element_type=F32) + b1l_ref[0]
        glu = jnp.minimum(glu, SWIGLU_LIMIT)
        lin = jnp.clip(lin, -SWIGLU_LIMIT, SWIGLU_LIMIT)
        act = glu * _sigmoid(SWIGLU_ALPHA * glu) * (lin + 1.0)
        part = jnp.dot(act.astype(BF16), w2_ref[0], preferred_element_type=F32)

        @pl.when(f == 0)
        def _():
            acc_ref[...] = part + b2_ref[0]

        @pl.when(f > 0)
        def _():
            acc_ref[...] = acc_ref[...] + part

        @pl.when(f == nf - 1)
        def _():
            o_ref[...] = acc_ref[...]

    @pl.when(jnp.logical_and(jnp.logical_not(live), f == nf - 1))
    def _():
        o_ref[...] = jnp.zeros_like(o_ref)


def _expert_mlp(tile_e, tile_valid, n_used, xs, w1, b1, w2, b2, tmb, tf):
    p_rows = xs.shape[0]
    d = w1.shape[1]
    d_ff = w2.shape[1]
    nf = d_ff // tf
    n_tiles = p_rows // tmb

    def fsel(i, f, tv):
        return jnp.where(tv[i] > 0, f, nf - 1)

    grid_spec = pltpu.PrefetchScalarGridSpec(
        num_scalar_prefetch=3,
        grid=(n_tiles, nf),
        in_specs=[
            pl.BlockSpec((tmb, d // 2), lambda i, f, te, tv, nu: (jnp.minimum(i, nu[0] - 1), 0)),
            pl.BlockSpec((1, d, tf), lambda i, f, te, tv, nu: (te[i], 0, fsel(i, f, tv))),
            pl.BlockSpec((1, d, tf), lambda i, f, te, tv, nu: (te[i], 0, nf + fsel(i, f, tv))),
            pl.BlockSpec((1, 1, tf), lambda i, f, te, tv, nu: (te[i], 0, fsel(i, f, tv))),
            pl.BlockSpec((1, 1, tf), lambda i, f, te, tv, nu: (te[i], 0, nf + fsel(i, f, tv))),
            pl.BlockSpec((1, tf, d), lambda i, f, te, tv, nu: (te[i], fsel(i, f, tv), 0)),
            pl.BlockSpec((1, 1, d), lambda i, f, te, tv, nu: (te[i], 0, 0)),
        ],
        out_specs=pl.BlockSpec((tmb, d), lambda i, f, te, tv, nu: (i, 0)),
        scratch_shapes=[pltpu.VMEM((tmb, d), BF16), pltpu.VMEM((tmb, d), F32)],
    )
    return pl.pallas_call(
        _expert_kernel,
        grid_spec=grid_spec,
        out_shape=jax.ShapeDtypeStruct((p_rows, d), F32),
        compiler_params=_params("arbitrary", "arbitrary"),
        name="expert_mlp",
    )(tile_e, tile_valid, n_used, xs, w1, w1, b1, b1, w2, b2)


def _combine_kernel(dest_hbm, ys_hbm, x_ref, gate_ref, g_ref, o_ref, dsm, buf, sem_idx, sem_row, *, m_tokens, tc):
    i = pl.program_id(0)
    n = pl.num_programs(0)
    per_stage = SMEM_STAGE // tc

    def issue_tile(tile):
        slot = tile % 2
        off = (tile % per_stage) * tc

        @pl.when(tile % per_stage == 0)
        def _():
            base = (tile // per_stage) * SMEM_STAGE

            def idx_copy(k):
                return pltpu.make_async_copy(dest_hbm.at[pl.ds(k * m_tokens + base, SMEM_STAGE)],
                                             dsm.at[pl.ds(k * SMEM_STAGE, SMEM_STAGE)], sem_idx)

            for k in range(TOP_K):
                idx_copy(k).start()
            for k in range(TOP_K):
                idx_copy(k).wait()

        def issue(t, carry):
            for k in range(TOP_K):
                pltpu.make_async_copy(ys_hbm.at[pl.ds(dsm[k * SMEM_STAGE + off + t], 1)],
                                      buf.at[slot, k, pl.ds(t, 1)], sem_row.at[slot]).start()
            return carry

        lax.fori_loop(0, tc, issue, 0, unroll=8)

    @pl.when(i == 0)
    def _():
        issue_tile(i)

    @pl.when(i + 1 < n)
    def _():
        issue_tile(i + 1)

    slot = i % 2
    for k in range(TOP_K):
        pltpu.make_async_copy(ys_hbm.at[pl.ds(0, tc)], buf.at[slot, k], sem_row.at[slot]).wait()

    gate = gate_ref[...]
    y = x_ref[...]
    for k in range(TOP_K):
        y = y + buf[slot, k] * gate[:, k:k + 1]
    o_ref[...] = _rms(y, g_ref[...])


def _combine(dest_flat, ys, x, gate_t, g, tc):
    m, d = x.shape
    return pl.pallas_call(
        functools.partial(_combine_kernel, m_tokens=m, tc=tc),
        grid=(m // tc,),
        in_specs=[pl.BlockSpec(memory_space=pl.ANY),
                  pl.BlockSpec(memory_space=pl.ANY),
                  pl.BlockSpec((tc, d), lambda i: (i, 0)),
                  pl.BlockSpec((tc, TOP_K), lambda i: (i, 0)),
                  pl.BlockSpec((1, d), lambda i: (0, 0))],
        out_specs=pl.BlockSpec((tc, d), lambda i: (i, 0)),
        out_shape=jax.ShapeDtypeStruct((m, d), F32),
        scratch_shapes=[pltpu.SMEM((TOP_K * SMEM_STAGE,), I32),
                        pltpu.VMEM((2, TOP_K, tc, d), F32),
                        pltpu.SemaphoreType.DMA,
                        pltpu.SemaphoreType.DMA((2,))],
        compiler_params=_params("arbitrary"),
        name="combine",
    )(dest_flat, ys, x, gate_t, g.reshape(1, d))


def _rope_tables(seq):
    half = ATT_HEAD_DIM // 2
    inv = ROPE_THETA ** (-jnp.arange(0, ATT_HEAD_DIM, 2, dtype=F32) / ATT_HEAD_DIM)
    ang = jnp.arange(seq, dtype=F32)[:, None] * inv[None, :]
    cos = jnp.tile(jnp.cos(ang), (1, LANES // half))
    sin = jnp.sin(ang)
    sin_signed = jnp.tile(jnp.concatenate([-sin, sin], axis=-1), (1, LANES // ATT_HEAD_DIM))
    return cos, sin_signed


def _pick(n, prefs):
    for p in prefs:
        if n % p == 0:
            return p
    return n


def _trunk(x, mem, wts, moe_tile, ff_tile):
    batch, seq, d = x.shape
    n_mem = mem.shape[1]
    m = batch * seq
    x2d = x.reshape(m, d)
    tm = _pick(m, (512, 256, 128))

    n_in = wts["w_in"].shape[1]
    z = _norm_matmul(x2d, wts["norm_mix"], wts["w_in"], tm, _pick(n_in, (1664, 1024, 512, 256, 128)), F32,
                     "in_projection")

    cos, sin_signed = _rope_tables(seq)
    att = _window_attention(z, wts["att_sink"], cos, sin_signed, batch, seq, _pick(seq, (256, 128)))
    hg = _hgrn2(z, wts["lb"], wts["hg_norm"], batch, seq, _pick(seq, (256, 128, 64)))
    x1 = _out_projection(att, hg, x2d, wts["w_out"], tm)

    kv = _norm_matmul(mem.reshape(batch * n_mem, d), wts["norm_mem"], wts["w_xkv"], n_mem, 2 * X_WIDTH, BF16,
                      "memory_kv")
    x2 = _cross_attention(x1, wts["norm_cross"], kv, wts["w_xq"], wts["w_xo"], batch, seq, n_mem,
                          _pick(seq, (512, 256, 128)))

    h3, idx, gate, rank, cnt = _router(x2, wts["norm_ffn"], wts["w_router_t"], wts["b_router"], tm)

    counts = cnt[:, 0].astype(I32)
    padded = (counts + moe_tile - 1) // moe_tile * moe_tile
    pend = jnp.cumsum(padded)
    pstart = pend - padded
    dest = _assign_rows(pstart.astype(I32), idx, rank)
    n_tiles = -(-(m * TOP_K) // moe_tile) + N_EXPERTS
    p_rows = n_tiles * moe_tile
    tile_row = jnp.arange(n_tiles, dtype=I32) * moe_tile
    tile_e = jnp.minimum(jnp.sum((pend[None, :] <= tile_row[:, None]).astype(I32), axis=1), N_EXPERTS - 1)
    owner = tile_e[:, None] == jnp.arange(N_EXPERTS, dtype=I32)[None, :]
    seg_end = jnp.sum(jnp.where(owner, (pstart + counts)[None, :], 0), axis=1)
    tile_valid = jnp.clip(seg_end - tile_row, 0, moe_tile).astype(I32)
    n_used = (pend[-1:] // moe_tile).astype(I32)
    seg = jnp.stack([pstart, counts, pend]).astype(I32)
    dest_flat = dest.reshape(-1)

    xs = _dispatch(seg, dest_flat, h3, p_rows)
    ys = _expert_mlp(tile_e, tile_valid, n_used, xs, wts["w_moe1"], wts["b_moe1"], wts["w_moe2"], wts["b_moe2"],
                     moe_tile, ff_tile)
    y = _combine(dest_flat, ys, x2, gate.T, wts["norm_final"], _pick(m, (256, 128)))
    return y.reshape(batch, seq, d)


def kernel(x_prompt, x_sample, mem_prompt, mem_sample, norm_mix, w_in, att_sink, hg_lb_logits, hg_norm, w_out,
           norm_cross, norm_mem, w_xq, w_xkv, w_xo, norm_ffn, w_router, b_router, w_moe1, b_moe1, w_moe2, b_moe2,
           norm_final):
    assert w_in.shape[0] == 1, "the final norm is fused after the single layer"
    lb_all = jnp.cumsum(jax.nn.softmax(hg_lb_logits.astype(F32), axis=0), axis=0)
    d_ff = w_moe2.shape[2]
    wts = dict(
        norm_mix=norm_mix[0], w_in=w_in[0].astype(BF16), att_sink=att_sink[0], lb=lb_all[0],
        hg_norm=hg_norm[0], w_out=w_out[0].astype(BF16), norm_cross=norm_cross[0], norm_mem=norm_mem[0],
        w_xq=w_xq[0].astype(BF16), w_xkv=w_xkv[0].astype(BF16), w_xo=w_xo[0].astype(BF16),
        norm_ffn=norm_ffn[0], w_router_t=w_router[0].T, b_router=b_router[0],
        w_moe1=w_moe1[0].astype(BF16), b_moe1=b_moe1[0][:, None, :],
        w_moe2=w_moe2[0].astype(BF16), b_moe2=b_moe2[0][:, None, :],
        norm_final=norm_final,
    )
    ff_tile = _pick(d_ff, (1024, 512, 256, 128))
    return (_trunk(x_prompt, mem_prompt, wts, MOE_TILE, ff_tile),
            _trunk(x_sample, mem_sample, wts, MOE_TILE, ff_tile))
```

```python
import functools

import jax
import jax.numpy as jnp
from jax import lax
from jax.experimental import pallas as pl
from jax.experimental.pallas import tpu as pltpu

F32 = jnp.float32
BF16 = jnp.bfloat16
I32 = jnp.int32
U32 = jnp.uint32

EPS = 1e-5
ATT_HEADS = 16
ATT_KV_HEADS = 4
ATT_HEAD_DIM = 64
ATT_WIDTH = ATT_HEADS * ATT_HEAD_DIM
KV_WIDTH = ATT_KV_HEADS * ATT_HEAD_DIM
WINDOW = 128
ROPE_THETA = 10000.0
HG_HEADS = 8
HG_DIM = 128
HG_WIDTH = HG_HEADS * HG_DIM
HG_CHUNK = 64
X_HEADS = 4
X_HEAD_DIM = 128
X_WIDTH = X_HEADS * X_HEAD_DIM
N_EXPERTS = 32
TOP_K = 4
SWIGLU_ALPHA = 1.702
SWIGLU_LIMIT = 7.0

COL_Q_ATT = 0
COL_K_ATT = ATT_WIDTH
COL_V_ATT = ATT_WIDTH + KV_WIDTH
COL_Q_HG = ATT_WIDTH + 2 * KV_WIDTH
COL_F_FWD = COL_Q_HG + HG_WIDTH
COL_F_BWD = COL_F_FWD + HG_WIDTH
COL_I_HG = COL_F_BWD + HG_WIDTH
COL_G_HG = COL_I_HG + HG_WIDTH

LANES = 128
VMEM_LIMIT = 56 * 1024 * 1024
SMEM_STAGE = 1024
MOE_TILE = 512

NT_DIMS = (((1,), (1,)), ((), ()))
TN_DIMS = (((0,), (0,)), ((), ()))


def _params(*sem):
    return pltpu.CompilerParams(dimension_semantics=sem, vmem_limit_bytes=VMEM_LIMIT)


def _rms(x, g):
    ms = jnp.mean(x * x, axis=-1, keepdims=True)
    return x * lax.rsqrt(ms + EPS) * g


def _sigmoid(x):
    return 1.0 / (1.0 + jnp.exp(-x))


def _norm_matmul_kernel(x_ref, g_ref, w_ref, o_ref, xn_ref):
    @pl.when(pl.program_id(1) == 0)
    def _():
        xn_ref[...] = _rms(x_ref[...], g_ref[...]).astype(BF16)

    o_ref[...] = jnp.dot(xn_ref[...], w_ref[...], preferred_element_type=F32).astype(o_ref.dtype)


def _norm_matmul(x, g, w, tm, tn, out_dtype, name):
    m, d = x.shape
    n = w.shape[1]
    return pl.pallas_call(
        _norm_matmul_kernel,
        grid=(m // tm, n // tn),
        in_specs=[pl.BlockSpec((tm, d), lambda i, j: (i, 0)),
                  pl.BlockSpec((1, d), lambda i, j: (0, 0)),
                  pl.BlockSpec((d, tn), lambda i, j: (0, j))],
        out_specs=pl.BlockSpec((tm, tn), lambda i, j: (i, j)),
        out_shape=jax.ShapeDtypeStruct((m, n), out_dtype),
        scratch_shapes=[pltpu.VMEM((tm, d), BF16)],
        compiler_params=_params("arbitrary", "arbitrary"),
        name=name,
    )(x, g.reshape(1, d), w)


def _rope_pair(x, cos, sin_signed):
    lane = lax.broadcasted_iota(I32, (1, LANES), 1)
    first_half = (lane % ATT_HEAD_DIM) < (ATT_HEAD_DIM // 2)
    rot = jnp.where(first_half, pltpu.roll(x, LANES - ATT_HEAD_DIM // 2, 1), pltpu.roll(x, ATT_HEAD_DIM // 2, 1))
    return x * cos + rot * sin_signed


def _attn_kernel(sink_ref, q_ref, k_ref, v_ref, cq_ref, sq_ref, ck_ref, sk_ref, o_ref, kr_ref, vr_ref, *, seq, tq):
    qi = pl.program_id(1)
    win = tq + 2 * WINDOW

    @pl.when(qi == 0)
    def _():
        for j in range(KV_WIDTH // LANES):
            sl = slice(j * LANES, (j + 1) * LANES)
            kr_ref[:, sl] = _rope_pair(k_ref[:, sl], ck_ref[...], sk_ref[...]).astype(BF16)
        vr_ref[...] = v_ref[...].astype(BF16)

    q0 = qi * tq
    ks = pl.multiple_of(jnp.clip(q0 - WINDOW, 0, seq - win), WINDOW)
    kwin = kr_ref[pl.ds(ks, win), :]
    vwin = vr_ref[pl.ds(ks, win), :]
    qpos = q0 + lax.broadcasted_iota(I32, (tq, 1), 0)
    kpos = ks + lax.broadcasted_iota(I32, (1, win), 1)
    valid = jnp.abs(kpos - qpos) <= WINDOW
    cq = cq_ref[...]
    sq = sq_ref[...]
    group = ATT_HEADS // ATT_KV_HEADS
    for j in range(ATT_WIDTH // LANES):
        qs = (_rope_pair(q_ref[:, j * LANES:(j + 1) * LANES], cq, sq) * (ATT_HEAD_DIM ** -0.5)).astype(BF16)
        outs = []
        for hh in range(LANES // ATT_HEAD_DIM):
            h = j * (LANES // ATT_HEAD_DIM) + hh
            g = h // group
            qh = qs[:, hh * ATT_HEAD_DIM:(hh + 1) * ATT_HEAD_DIM]
            kh = kwin[:, g * ATT_HEAD_DIM:(g + 1) * ATT_HEAD_DIM]
            vh = vwin[:, g * ATT_HEAD_DIM:(g + 1) * ATT_HEAD_DIM]
            s = lax.dot_general(qh, kh, NT_DIMS, preferred_element_type=F32)
            s = jnp.where(valid, s, -1e30)
            sk = sink_ref[h]
            m = jnp.maximum(jnp.max(s, axis=-1, keepdims=True), sk)
            p = jnp.exp(s - m)
            denom = jnp.sum(p, axis=-1, keepdims=True) + jnp.exp(sk - m)
            o = jnp.dot(p.astype(BF16), vh, preferred_element_type=F32)
            outs.append(o / denom)
        o_ref[:, j * LANES:(j + 1) * LANES] = jnp.concatenate(outs, axis=-1).astype(o_ref.dtype)


def _window_attention(z, sink, cos, sin_signed, batch, seq, tq):
    m = z.shape[0]
    nq = seq // tq
    qblk = COL_Q_ATT // ATT_WIDTH
    kblk = COL_K_ATT // KV_WIDTH
    vblk = COL_V_ATT // KV_WIDTH
    return pl.pallas_call(
        functools.partial(_attn_kernel, seq=seq, tq=tq),
        grid=(batch, nq),
        in_specs=[pl.BlockSpec(memory_space=pltpu.SMEM),
                  pl.BlockSpec((tq, ATT_WIDTH), lambda b, i: (b * nq + i, qblk)),
                  pl.BlockSpec((seq, KV_WIDTH), lambda b, i: (b, kblk)),
                  pl.BlockSpec((seq, KV_WIDTH), lambda b, i: (b, vblk)),
                  pl.BlockSpec((tq, LANES), lambda b, i: (i, 0)),
                  pl.BlockSpec((tq, LANES), lambda b, i: (i, 0)),
                  pl.BlockSpec((seq, LANES), lambda b, i: (0, 0)),
                  pl.BlockSpec((seq, LANES), lambda b, i: (0, 0))],
        out_specs=pl.BlockSpec((tq, ATT_WIDTH), lambda b, i: (b * nq + i, 0)),
        out_shape=jax.ShapeDtypeStruct((m, ATT_WIDTH), BF16),
        scratch_shapes=[pltpu.VMEM((seq, KV_WIDTH), BF16), pltpu.VMEM((seq, KV_WIDTH), BF16)],
        compiler_params=_params("arbitrary", "arbitrary"),
        name="window_attention",
    )(sink, z, z, z, cos, sin_signed, cos, sin_signed)


def _split_cumsum(tri, x):
    hi = x.astype(BF16)
    r1 = x - hi.astype(F32)
    mid = r1.astype(BF16)
    lo = (r1 - mid.astype(F32)).astype(BF16)
    dot = functools.partial(jnp.dot, preferred_element_type=F32)
    return dot(tri, hi) + dot(tri, mid) + dot(tri, lo)


def _hgrn_group(zq, zf, v, lbv, state, mask, tri, forward):
    rows = zq.shape[0]
    c = HG_CHUNK
    nchunk = rows // c
    q = zq * _sigmoid(zq) * (HG_DIM ** -0.5)
    f = lbv + (1.0 - lbv) * _sigmoid(zf)
    logf = jnp.log(f)
    k = 1.0 - f
    b = _split_cumsum(tri, logf)
    ref_row = c // 2 - 1 if forward else c // 2
    last_row = c - 1 if forward else 0
    qe, ke, kd, qb, dec = [], [], [], [], []
    for n in range(nchunk):
        sl = slice(n * c, (n + 1) * c)
        bn, qn, kn = b[sl], q[sl], k[sl]
        bref = bn[ref_row:ref_row + 1]
        blast = bn[last_row:last_row + 1]
        qe.append(qn * jnp.exp(bn - bref))
        ke.append(kn * jnp.exp(bref - bn))
        kd.append((kn * jnp.exp(blast - bn)).astype(BF16))
        qb.append((qn * jnp.exp(bn)).astype(BF16))
        dec.append(jnp.exp(blast))
    qe = jnp.concatenate(qe, axis=0).astype(BF16)
    ke = jnp.concatenate(ke, axis=0).astype(BF16)
    vb = v.astype(BF16)
    a = lax.dot_general(qe, ke, NT_DIMS, preferred_element_type=F32)
    a = jnp.where(mask, a, 0.0).astype(BF16)
    o = jnp.dot(a, vb, preferred_element_type=F32)

    zero_blk = jnp.zeros((c, HG_DIM), BF16)

    def block_diag(blocks):
        return jnp.concatenate(
            [jnp.concatenate([blk if j == n else zero_blk for j in range(nchunk)], axis=1)
             for n, blk in enumerate(blocks)], axis=0)

    u_all = lax.dot_general(vb, block_diag(kd), TN_DIMS, preferred_element_type=F32)
    entering = [None] * nchunk
    order = range(nchunk) if forward else range(nchunk - 1, -1, -1)
    for n in order:
        entering[n] = state.astype(BF16)
        state = state * dec[n] + u_all[:, n * HG_DIM:(n + 1) * HG_DIM]
    o = o + lax.dot_general(block_diag(qb), jnp.concatenate(entering, axis=1), NT_DIMS,
                            preferred_element_type=F32)
    return o, state


def _hgrn_kernel(zq_ref, zff_ref, zfb_ref, zi_ref, zg_ref, lb_ref, gn_ref, o_ref, accf_ref, accb_ref, *, seq, rows):
    ngroups = seq // rows
    r = lax.broadcasted_iota(I32, (rows, rows), 0)
    cidx = lax.broadcasted_iota(I32, (rows, rows), 1)
    same = (r // HG_CHUNK) == (cidx // HG_CHUNK)
    mask_f = jnp.logical_and(same, cidx <= r)
    mask_b = jnp.logical_and(same, cidx >= r)
    tri_f = jnp.where(mask_f, 1.0, 0.0).astype(BF16)
    tri_b = jnp.where(mask_b, 1.0, 0.0).astype(BF16)
    lb_f = lb_ref[0:1, :]
    lb_b = lb_ref[1:2, :]
    zero_state = jnp.zeros((HG_DIM, HG_DIM), F32)

    def scan_body(it, states):
        sf, sb = states
        slf = pl.ds(pl.multiple_of(it * rows, rows), rows)
        slb = pl.ds(pl.multiple_of((ngroups - 1 - it) * rows, rows), rows)
        of, sf = _hgrn_group(zq_ref[slf, :], zff_ref[slf, :], zi_ref[slf, :], lb_f, sf, mask_f, tri_f, True)
        ob, sb = _hgrn_group(zq_ref[slb, :], zfb_ref[slb, :], zi_ref[slb, :], lb_b, sb, mask_b, tri_b, False)
        accf_ref[slf, :] = of
        accb_ref[slb, :] = ob
        return sf, sb

    lax.fori_loop(0, ngroups, scan_body, (zero_state, zero_state))

    def readout_body(gi, carry):
        sl = pl.ds(pl.multiple_of(gi * rows, rows), rows)
        tot = accf_ref[sl, :] + accb_ref[sl, :]
        zg = zg_ref[sl, :]
        y = _rms(tot, gn_ref[...]) * (zg * _sigmoid(zg))
        o_ref[sl, :] = y.astype(o_ref.dtype)
        return carry

    lax.fori_loop(0, ngroups, readout_body, 0)


def _hgrn2(z, lb, hg_norm, batch, seq, rows):
    m = z.shape[0]

    def zspec(col):
        base = col // HG_DIM
        return pl.BlockSpec((seq, HG_DIM), lambda b, h: (b, base + h))

    return pl.pallas_call(
        functools.partial(_hgrn_kernel, seq=seq, rows=rows),
        grid=(batch, HG_HEADS),
        in_specs=[zspec(COL_Q_HG), zspec(COL_F_FWD), zspec(COL_F_BWD), zspec(COL_I_HG), zspec(COL_G_HG),
                  pl.BlockSpec((2, HG_DIM), lambda b, h: (0, h)),
                  pl.BlockSpec((1, HG_DIM), lambda b, h: (0, 0))],
        out_specs=pl.BlockSpec((seq, HG_DIM), lambda b, h: (b, h)),
        out_shape=jax.ShapeDtypeStruct((m, HG_WIDTH), BF16),
        scratch_shapes=[pltpu.VMEM((seq, HG_DIM), F32), pltpu.VMEM((seq, HG_DIM), F32)],
        compiler_params=_params("arbitrary", "arbitrary"),
        name="hgrn2",
    )(z, z, z, z, z, lb, hg_norm.reshape(1, HG_DIM))


def _outproj_kernel(a_ref, h_ref, x_ref, wa_ref, wh_ref, o_ref):
    acc = jnp.dot(a_ref[...], wa_ref[...], preferred_element_type=F32)
    acc = acc + jnp.dot(h_ref[...], wh_ref[...], preferred_element_type=F32)
    o_ref[...] = x_ref[...] + acc


def _out_projection(att, hg, x, w_out, tm):
    m, d = x.shape
    return pl.pallas_call(
        _outproj_kernel,
        grid=(m // tm,),
        in_specs=[pl.BlockSpec((tm, ATT_WIDTH), lambda i: (i, 0)),
                  pl.BlockSpec((tm, HG_WIDTH), lambda i: (i, 0)),
                  pl.BlockSpec((tm, d), lambda i: (i, 0)),
                  pl.BlockSpec((ATT_WIDTH, d), lambda i: (0, 0)),
                  pl.BlockSpec((HG_WIDTH, d), lambda i: (ATT_WIDTH // HG_WIDTH, 0))],
        out_specs=pl.BlockSpec((tm, d), lambda i: (i, 0)),
        out_shape=jax.ShapeDtypeStruct((m, d), F32),
        compiler_params=_params("arbitrary"),
        name="out_projection",
    )(att, hg, x, w_out, w_out)


def _cross_kernel(x_ref, g_ref, kv_ref, wq_ref, wo_ref, o_ref):
    x = x_ref[...]
    h = _rms(x, g_ref[...]).astype(BF16)
    q = jnp.dot(h, wq_ref[...], preferred_element_type=F32).astype(BF16)
    outs = []
    for hd in range(X_HEADS):
        sl = slice(hd * X_HEAD_DIM, (hd + 1) * X_HEAD_DIM)
        kh = kv_ref[:, sl]
        vh = kv_ref[:, X_WIDTH + hd * X_HEAD_DIM:X_WIDTH + (hd + 1) * X_HEAD_DIM]
        s = lax.dot_general(q[:, sl], kh, NT_DIMS, preferred_element_type=F32) * (X_HEAD_DIM ** -0.5)
        p = jnp.exp(s - jnp.max(s, axis=-1, keepdims=True))
        denom = jnp.sum(p, axis=-1, keepdims=True)
        outs.append(jnp.dot(p.astype(BF16), vh, preferred_element_type=F32) / denom)
    oc = jnp.concatenate(outs, axis=-1).astype(BF16)
    o_ref[...] = x + jnp.dot(oc, wo_ref[...], preferred_element_type=F32)


def _cross_attention(x, g, kv, w_xq, w_xo, batch, seq, n_mem, tm):
    m, d = x.shape
    nt = seq // tm
    return pl.pallas_call(
        _cross_kernel,
        grid=(batch, nt),
        in_specs=[pl.BlockSpec((tm, d), lambda b, i: (b * nt + i, 0)),
                  pl.BlockSpec((1, d), lambda b, i: (0, 0)),
                  pl.BlockSpec((n_mem, 2 * X_WIDTH), lambda b, i: (b, 0)),
                  pl.BlockSpec((d, X_WIDTH), lambda b, i: (0, 0)),
                  pl.BlockSpec((X_WIDTH, d), lambda b, i: (0, 0))],
        out_specs=pl.BlockSpec((tm, d), lambda b, i: (b * nt + i, 0)),
        out_shape=jax.ShapeDtypeStruct((m, d), F32),
        compiler_params=_params("arbitrary", "arbitrary"),
        name="cross_attention",
    )(x, g.reshape(1, d), kv, w_xq, w_xo)


def _router_kernel(x_ref, g_ref, wr_ref, br_ref, h_ref, idx_ref, gate_ref, rank_ref, cnt_ref, base_ref):
    tm = x_ref.shape[0]

    @pl.when(pl.program_id(0) == 0)
    def _():
        base_ref[...] = jnp.zeros_like(base_ref)

    h = _rms(x_ref[...], g_ref[...])
    hh = h.astype(BF16)
    half = h.shape[1] // 2
    bits = lax.bitcast_convert_type(hh.astype(F32), U32)
    h_ref[...] = (bits[:, :half] >> 16) | (bits[:, half:] & jnp.uint32(0xFFFF0000))
    hl = (h - hh.astype(F32)).astype(BF16)
    w = wr_ref[...]
    wh = w.astype(BF16)
    wl = (w - wh.astype(F32)).astype(BF16)
    nt = functools.partial(lax.dot_general, dimension_numbers=NT_DIMS, preferred_element_type=F32)
    logits = nt(wh, hh) + nt(wh, hl) + nt(wl, hh) + br_ref[...]

    eio = lax.broadcasted_iota(I32, (N_EXPERTS, tm), 0).astype(F32)
    work = logits
    vals, onehots = [], []
    for k in range(TOP_K):
        mx = jnp.max(work, axis=0, keepdims=True)
        ix = jnp.min(jnp.where(work == mx, eio, float(N_EXPERTS)), axis=0, keepdims=True)
        sel = eio == ix
        vals.append(mx)
        onehots.append(sel)
        idx_ref[k:k + 1, :] = ix.astype(I32)
        work = jnp.where(sel, -jnp.inf, work)
    ex = [jnp.exp(v - vals[0]) for v in vals]
    denom = ex[0] + ex[1] + ex[2] + ex[3]
    for k in range(TOP_K):
        gate_ref[k:k + 1, :] = ex[k] / denom

    oh = [jnp.where(s, 1.0, 0.0) for s in onehots]
    oh_all = oh[0] + oh[1] + oh[2] + oh[3]
    r = lax.broadcasted_iota(I32, (tm, tm), 0)
    c = lax.broadcasted_iota(I32, (tm, tm), 1)
    upper = jnp.where(r < c, 1.0, 0.0).astype(BF16)
    before = jnp.dot(oh_all.astype(BF16), upper, preferred_element_type=F32) + base_ref[...]
    for k in range(TOP_K):
        rank_ref[k:k + 1, :] = jnp.sum(oh[k] * before, axis=0, keepdims=True).astype(I32)
    base_ref[...] = base_ref[...] + jnp.sum(oh_all, axis=1, keepdims=True)
    cnt_ref[...] = base_ref[...]


def _router(x, g, w_router_t, b_router, tm):
    m, d = x.shape
    row4 = pl.BlockSpec((TOP_K, tm), lambda i: (0, i))
    return pl.pallas_call(
        _router_kernel,
        grid=(m // tm,),
        in_specs=[pl.BlockSpec((tm, d), lambda i: (i, 0)),
                  pl.BlockSpec((1, d), lambda i: (0, 0)),
                  pl.BlockSpec((N_EXPERTS, d), lambda i: (0, 0)),
                  pl.BlockSpec((N_EXPERTS, 1), lambda i: (0, 0))],
        out_specs=[pl.BlockSpec((tm, d // 2), lambda i: (i, 0)), row4, row4, row4,
                   pl.BlockSpec((N_EXPERTS, 1), lambda i: (0, 0))],
        out_shape=[jax.ShapeDtypeStruct((m, d // 2), U32),
                   jax.ShapeDtypeStruct((TOP_K, m), I32),
                   jax.ShapeDtypeStruct((TOP_K, m), F32),
                   jax.ShapeDtypeStruct((TOP_K, m), I32),
                   jax.ShapeDtypeStruct((N_EXPERTS, 1), F32)],
        scratch_shapes=[pltpu.VMEM((N_EXPERTS, 1), F32)],
        compiler_params=_params("arbitrary"),
        name="router",
    )(x, g.reshape(1, d), w_router_t, b_router.reshape(N_EXPERTS, 1))


def _assign_rows_kernel(pstart_ref, idx_ref, rank_ref, dest_ref):
    idx = idx_ref[...]
    start = jnp.zeros_like(idx)
    for e in range(N_EXPERTS):
        start = jnp.where(idx == e, pstart_ref[e], start)
    dest_ref[...] = start + rank_ref[...]


def _assign_rows(pstart, idx, rank):
    full = pl.BlockSpec(idx.shape, lambda: (0, 0))
    return pl.pallas_call(
        _assign_rows_kernel,
        in_specs=[pl.BlockSpec(memory_space=pltpu.SMEM), full, full],
        out_specs=full,
        out_shape=jax.ShapeDtypeStruct(idx.shape, I32),
        name="assign_rows",
    )(pstart, idx, rank)


def _dispatch_kernel(seg_ref, dest_hbm, h_ref, xs_hbm, dsm, zrow, sem_idx, sem_row, *, m_tokens, p_rows):
    i = pl.program_id(0)
    tt = SMEM_STAGE

    def idx_copy(k):
        return pltpu.make_async_copy(dest_hbm.at[pl.ds(k * m_tokens + i * tt, tt)],
                                     dsm.at[pl.ds(k * tt, tt)], sem_idx)

    for k in range(TOP_K):
        idx_copy(k).start()
    for k in range(TOP_K):
        idx_copy(k).wait()

    def row_copy(t, dst_row):
        return pltpu.make_async_copy(h_ref.at[pl.ds(t, 1)], xs_hbm.at[pl.ds(dst_row, 1)], sem_row)

    def issue(t, carry):
        for k in range(TOP_K):
            row_copy(t, dsm[k * tt + t]).start(priority=k % 2)
        return carry

    lax.fori_loop(0, tt, issue, 0, unroll=8)

    for k in range(TOP_K):
        pltpu.make_async_copy(h_ref, xs_hbm.at[pl.ds(0, tt)], sem_row).wait()

    @pl.when(i == 0)
    def _():
        zrow[...] = jnp.zeros_like(zrow)

        def zero_copy(dst_row):
            return pltpu.make_async_copy(zrow.at[pl.ds(0, 1)], xs_hbm.at[pl.ds(dst_row, 1)], sem_row)

        def fill(lo, hi):
            def body(rw, carry):
                zero_copy(rw).start()
                return carry

            lax.fori_loop(lo, hi, body, 0)

            def wbody(rw, carry):
                zero_copy(0).wait()
                return carry

            lax.fori_loop(lo, hi, wbody, 0)

        def per_expert(e, carry):
            fill(seg_ref[0, e] + seg_ref[1, e], seg_ref[2, e])
            return carry

        lax.fori_loop(0, N_EXPERTS, per_expert, 0)
        fill(seg_ref[2, N_EXPERTS - 1], p_rows)


def _dispatch(seg, dest_flat, h, p_rows):
    m, d = h.shape
    return pl.pallas_call(
        functools.partial(_dispatch_kernel, m_tokens=m, p_rows=p_rows),
        grid=(m // SMEM_STAGE,),
        in_specs=[pl.BlockSpec(memory_space=pltpu.SMEM),
                  pl.BlockSpec(memory_space=pl.ANY),
                  pl.BlockSpec((SMEM_STAGE, d), lambda i: (i, 0))],
        out_specs=pl.BlockSpec(memory_space=pl.ANY),
        out_shape=jax.ShapeDtypeStruct((p_rows, d), h.dtype),
        scratch_shapes=[pltpu.SMEM((TOP_K * SMEM_STAGE,), I32),
                        pltpu.VMEM((8, d), h.dtype),
                        pltpu.SemaphoreType.DMA,
                        pltpu.SemaphoreType.DMA],
        compiler_params=_params("arbitrary"),
        name="dispatch",
    )(seg, dest_flat, h)


def _expert_kernel(te_ref, tv_ref, nu_ref, x_ref, w1g_ref, w1l_ref, b1g_ref, b1l_ref, w2_ref, b2_ref,
                   o_ref, xb_ref, acc_ref):
    i = pl.program_id(0)
    f = pl.program_id(1)
    nf = pl.num_programs(1)
    live = tv_ref[i] > 0

    @pl.when(live)
    def _():
        @pl.when(f == 0)
        def _():
            words = x_ref[...]
            half = words.shape[1]
            xb_ref[:, :half] = lax.bitcast_convert_type(words << 16, F32).astype(BF16)
            xb_ref[:, half:] = lax.bitcast_convert_type(words & jnp.uint32(0xFFFF0000), F32).astype(BF16)

        xb = xb_ref[...]
        glu = jnp.dot(xb, w1g_ref[0], preferred_element_type=F32) + b1g_ref[0]
        lin = jnp.dot(xb, w1l_ref[0], preferred_element_type=F32) + b1l_ref[0]
        glu = jnp.minimum(glu, SWIGLU_LIMIT)
        lin = jnp.clip(lin, -SWIGLU_LIMIT, SWIGLU_LIMIT)
        act = glu * _sigmoid(SWIGLU_ALPHA * glu) * (lin + 1.0)
        part = jnp.dot(act.astype(BF16), w2_ref[0], preferred_element_type=F32)

        @pl.when(f == 0)
        def _():
            acc_ref[...] = part + b2_ref[0]

        @pl.when(f > 0)
        def _():
            acc_ref[...] = acc_ref[...] + part

        @pl.when(f == nf - 1)
        def _():
            o_ref[...] = acc_ref[...]

    @pl.when(jnp.logical_and(jnp.logical_not(live), f == nf - 1))
    def _():
        o_ref[...] = jnp.zeros_like(o_ref)


def _expert_mlp(tile_e, tile_valid, n_used, xs, w1, b1, w2, b2, tmb, tf):
    p_rows = xs.shape[0]
    d = w1.shape[1]
    d_ff = w2.shape[1]
    nf = d_ff // tf
    n_tiles = p_rows // tmb

    def fsel(i, f, tv):
        return jnp.where(tv[i] > 0, f, nf - 1)

    grid_spec = pltpu.PrefetchScalarGridSpec(
        num_scalar_prefetch=3,
        grid=(n_tiles, nf),
        in_specs=[
            pl.BlockSpec((tmb, d // 2), lambda i, f, te, tv, nu: (jnp.minimum(i, nu[0] - 1), 0)),
            pl.BlockSpec((1, d, tf), lambda i, f, te, tv, nu: (te[i], 0, fsel(i, f, tv))),
            pl.BlockSpec((1, d, tf), lambda i, f, te, tv, nu: (te[i], 0, nf + fsel(i, f, tv))),
            pl.BlockSpec((1, 1, tf), lambda i, f, te, tv, nu: (te[i], 0, fsel(i, f, tv))),
            pl.BlockSpec((1, 1, tf), lambda i, f, te, tv, nu: (te[i], 0, nf + fsel(i, f, tv))),
            pl.BlockSpec((1, tf, d), lambda i, f, te, tv, nu: (te[i], fsel(i, f, tv), 0)),
            pl.BlockSpec((1, 1, d), lambda i, f, te, tv, nu: (te[i], 0, 0)),
        ],
        out_specs=pl.BlockSpec((tmb, d), lambda i, f, te, tv, nu: (i, 0)),
        scratch_shapes=[pltpu.VMEM((tmb, d), BF16), pltpu.VMEM((tmb, d), F32)],
    )
    return pl.pallas_call(
        _expert_kernel,
        grid_spec=grid_spec,
        out_shape=jax.ShapeDtypeStruct((p_rows, d), F32),
        compiler_params=_params("arbitrary", "arbitrary"),
        name="expert_mlp",
    )(tile_e, tile_valid, n_used, xs, w1, w1, b1, b1, w2, b2)


def _combine_kernel(dest_hbm, ys_hbm, x_ref, gate_ref, g_ref, o_ref, dsm, buf, sem_idx, sem_row, *, m_tokens, tc):
    i = pl.program_id(0)
    n = pl.num_programs(0)
    per_stage = SMEM_STAGE // tc

    def issue_tile(tile):
        slot = tile % 2
        off = (tile % per_stage) * tc

        @pl.when(tile % per_stage == 0)
        def _():
            base = (tile // per_stage) * SMEM_STAGE

            def idx_copy(k):
                return pltpu.make_async_copy(dest_hbm.at[pl.ds(k * m_tokens + base, SMEM_STAGE)],
                                             dsm.at[pl.ds(k * SMEM_STAGE, SMEM_STAGE)], sem_idx)

            for k in range(TOP_K):
                idx_copy(k).start()
            for k in range(TOP_K):
                idx_copy(k).wait()

        def issue(t, carry):
            for k in range(TOP_K):
                pltpu.make_async_copy(ys_hbm.at[pl.ds(dsm[k * SMEM_STAGE + off + t], 1)],
                                      buf.at[slot, k, pl.ds(t, 1)], sem_row.at[slot]).start(priority=k % 2)
            return carry

        lax.fori_loop(0, tc, issue, 0, unroll=8)

    @pl.when(i == 0)
    def _():
        issue_tile(i)

    @pl.when(i + 1 < n)
    def _():
        issue_tile(i + 1)

    slot = i % 2
    for k in range(TOP_K):
        pltpu.make_async_copy(ys_hbm.at[pl.ds(0, tc)], buf.at[slot, k], sem_row.at[slot]).wait()

    gate = gate_ref[...]
    y = x_ref[...]
    for k in range(TOP_K):
        y = y + buf[slot, k] * gate[:, k:k + 1]
    o_ref[...] = _rms(y, g_ref[...])


def _combine(dest_flat, ys, x, gate_t, g, tc):
    m, d = x.shape
    return pl.pallas_call(
        functools.partial(_combine_kernel, m_tokens=m, tc=tc),
        grid=(m // tc,),
        in_specs=[pl.BlockSpec(memory_space=pl.ANY),
                  pl.BlockSpec(memory_space=pl.ANY),
                  pl.BlockSpec((tc, d), lambda i: (i, 0)),
                  pl.BlockSpec((tc, TOP_K), lambda i: (i, 0)),
                  pl.BlockSpec((1, d), lambda i: (0, 0))],
        out_specs=pl.BlockSpec((tc, d), lambda i: (i, 0)),
        out_shape=jax.ShapeDtypeStruct((m, d), F32),
        scratch_shapes=[pltpu.SMEM((TOP_K * SMEM_STAGE,), I32),
                        pltpu.VMEM((2, TOP_K, tc, d), F32),
                        pltpu.SemaphoreType.DMA,
                        pltpu.SemaphoreType.DMA((2,))],
        compiler_params=_params("arbitrary"),
        name="combine",
    )(dest_flat, ys, x, gate_t, g.reshape(1, d))


def _rope_tables(seq):
    half = ATT_HEAD_DIM // 2
    inv = ROPE_THETA ** (-jnp.arange(0, ATT_HEAD_DIM, 2, dtype=F32) / ATT_HEAD_DIM)
    ang = jnp.arange(seq, dtype=F32)[:, None] * inv[None, :]
    cos = jnp.tile(jnp.cos(ang), (1, LANES // half))
    sin = jnp.sin(ang)
    sin_signed = jnp.tile(jnp.concatenate([-sin, sin], axis=-1), (1, LANES // ATT_HEAD_DIM))
    return cos, sin_signed


def _pick(n, prefs):
    for p in prefs:
        if n % p == 0:
            return p
    return n


def _trunk(x, mem, wts, moe_tile, ff_tile):
    batch, seq, d = x.shape
    n_mem = mem.shape[1]
    m = batch * seq
    x2d = x.reshape(m, d)
    tm = _pick(m, (512, 256, 128))

    n_in = wts["w_in"].shape[1]
    z = _norm_matmul(x2d, wts["norm_mix"], wts["w_in"], tm, _pick(n_in, (1664, 1024, 512, 256, 128)), F32,
                     "in_projection")

    cos, sin_signed = _rope_tables(seq)
    att = _window_attention(z, wts["att_sink"], cos, sin_signed, batch, seq, _pick(seq, (256, 128)))
    hg = _hgrn2(z, wts["lb"], wts["hg_norm"], batch, seq, _pick(seq, (256, 128, 64)))
    x1 = _out_projection(att, hg, x2d, wts["w_out"], tm)

    kv = _norm_matmul(mem.reshape(batch * n_mem, d), wts["norm_mem"], wts["w_xkv"], n_mem, 2 * X_WIDTH, BF16,
                      "memory_kv")
    x2 = _cross_attention(x1, wts["norm_cross"], kv, wts["w_xq"], wts["w_xo"], batch, seq, n_mem,
                          _pick(seq, (512, 256, 128)))

    h3, idx, gate, rank, cnt = _router(x2, wts["norm_ffn"], wts["w_router_t"], wts["b_router"], tm)

    counts = cnt[:, 0].astype(I32)
    padded = (counts + moe_tile - 1) // moe_tile * moe_tile
    pend = jnp.cumsum(padded)
    pstart = pend - padded
    dest = _assign_rows(pstart.astype(I32), idx, rank)
    n_tiles = -(-(m * TOP_K) // moe_tile) + N_EXPERTS
    p_rows = n_tiles * moe_tile
    tile_row = jnp.arange(n_tiles, dtype=I32) * moe_tile
    tile_e = jnp.minimum(jnp.sum((pend[None, :] <= tile_row[:, None]).astype(I32), axis=1), N_EXPERTS - 1)
    owner = tile_e[:, None] == jnp.arange(N_EXPERTS, dtype=I32)[None, :]
    seg_end = jnp.sum(jnp.where(owner, (pstart + counts)[None, :], 0), axis=1)
    tile_valid = jnp.clip(seg_end - tile_row, 0, moe_tile).astype(I32)
    n_used = (pend[-1:] // moe_tile).astype(I32)
    seg = jnp.stack([pstart, counts, pend]).astype(I32)
    dest_flat = dest.reshape(-1)

    xs = _dispatch(seg, dest_flat, h3, p_rows)
    ys = _expert_mlp(tile_e, tile_valid, n_used, xs, wts["w_moe1"], wts["b_moe1"], wts["w_moe2"], wts["b_moe2"],
                     moe_tile, ff_tile)
    y = _combine(dest_flat, ys, x2, gate.T, wts["norm_final"], _pick(m, (256, 128)))
    return y.reshape(batch, seq, d)


def kernel(x_prompt, x_sample, mem_prompt, mem_sample, norm_mix, w_in, att_sink, hg_lb_logits, hg_norm, w_out,
           norm_cross, norm_mem, w_xq, w_xkv, w_xo, norm_ffn, w_router, b_router, w_moe1, b_moe1, w_moe2, b_moe2,
           norm_final):
    assert w_in.shape[0] == 1, "the final norm is fused after the single layer"
    lb_all = jnp.cumsum(jax.nn.softmax(hg_lb_logits.astype(F32), axis=0), axis=0)
    d_ff = w_moe2.shape[2]
    wts = dict(
        norm_mix=norm_mix[0], w_in=w_in[0].astype(BF16), att_sink=att_sink[0], lb=lb_all[0],
        hg_norm=hg_norm[0], w_out=w_out[0].astype(BF16), norm_cross=norm_cross[0], norm_mem=norm_mem[0],
        w_xq=w_xq[0].astype(BF16), w_xkv=w_xkv[0].astype(BF16), w_xo=w_xo[0].astype(BF16),
        norm_ffn=norm_ffn[0], w_router_t=w_router[0].T, b_router=b_router[0],
        w_moe1=w_moe1[0].astype(BF16), b_moe1=b_moe1[0][:, None, :],
        w_moe2=w_moe2[0].astype(BF16), b_moe2=b_moe2[0][:, None, :],
        norm_final=norm_final,
    )
    ff_tile = _pick(d_ff, (1024, 512, 256, 128))
    return (_trunk(x_prompt, mem_prompt, wts, MOE_TILE, ff_tile),
            _trunk(x_sample, mem_sample, wts, MOE_TILE, ff_tile))
```

```python
import functools

import jax
import jax.numpy as jnp
from jax import lax
from jax.experimental import pallas as pl
from jax.experimental.pallas import tpu as pltpu

F32 = jnp.float32
BF16 = jnp.bfloat16
I32 = jnp.int32
U32 = jnp.uint32

EPS = 1e-5
ATT_HEADS = 16
ATT_KV_HEADS = 4
ATT_HEAD_DIM = 64
ATT_WIDTH = ATT_HEADS * ATT_HEAD_DIM
KV_WIDTH = ATT_KV_HEADS * ATT_HEAD_DIM
WINDOW = 128
ROPE_THETA = 10000.0
HG_HEADS = 8
HG_DIM = 128
HG_WIDTH = HG_HEADS * HG_DIM
HG_CHUNK = 64
X_HEADS = 4
X_HEAD_DIM = 128
X_WIDTH = X_HEADS * X_HEAD_DIM
N_EXPERTS = 32
TOP_K = 4
SWIGLU_ALPHA = 1.702
SWIGLU_LIMIT = 7.0

COL_Q_ATT = 0
COL_K_ATT = ATT_WIDTH
COL_V_ATT = ATT_WIDTH + KV_WIDTH
COL_Q_HG = ATT_WIDTH + 2 * KV_WIDTH
COL_F_FWD = COL_Q_HG + HG_WIDTH
COL_F_BWD = COL_F_FWD + HG_WIDTH
COL_I_HG = COL_F_BWD + HG_WIDTH
COL_G_HG = COL_I_HG + HG_WIDTH

LANES = 128
VMEM_LIMIT = 56 * 1024 * 1024
SMEM_STAGE = 1024
MOE_TILE = 512

NT_DIMS = (((1,), (1,)), ((), ()))
TN_DIMS = (((0,), (0,)), ((), ()))


def _params(*sem):
    return pltpu.CompilerParams(dimension_semantics=sem, vmem_limit_bytes=VMEM_LIMIT)


def _rms(x, g):
    ms = jnp.mean(x * x, axis=-1, keepdims=True)
    return x * lax.rsqrt(ms + EPS) * g


def _sigmoid(x):
    return 1.0 / (1.0 + jnp.exp(-x))


def _norm_matmul_kernel(x_ref, g_ref, w_ref, o_ref, xn_ref):
    @pl.when(pl.program_id(1) == 0)
    def _():
        xn_ref[...] = _rms(x_ref[...], g_ref[...]).astype(BF16)

    o_ref[...] = jnp.dot(xn_ref[...], w_ref[...], preferred_element_type=F32).astype(o_ref.dtype)


def _norm_matmul(x, g, w, tm, tn, out_dtype, name):
    m, d = x.shape
    n = w.shape[1]
    return pl.pallas_call(
        _norm_matmul_kernel,
        grid=(m // tm, n // tn),
        in_specs=[pl.BlockSpec((tm, d), lambda i, j: (i, 0)),
                  pl.BlockSpec((1, d), lambda i, j: (0, 0)),
                  pl.BlockSpec((d, tn), lambda i, j: (0, j))],
        out_specs=pl.BlockSpec((tm, tn), lambda i, j: (i, j)),
        out_shape=jax.ShapeDtypeStruct((m, n), out_dtype),
        scratch_shapes=[pltpu.VMEM((tm, d), BF16)],
        compiler_params=_params("arbitrary", "arbitrary"),
        name=name,
    )(x, g.reshape(1, d), w)


def _rope_pair(x, cos, sin_signed):
    lane = lax.broadcasted_iota(I32, (1, LANES), 1)
    first_half = (lane % ATT_HEAD_DIM) < (ATT_HEAD_DIM // 2)
    rot = jnp.where(first_half, pltpu.roll(x, LANES - ATT_HEAD_DIM // 2, 1), pltpu.roll(x, ATT_HEAD_DIM // 2, 1))
    return x * cos + rot * sin_signed


def _attn_kernel(sink_ref, q_ref, k_ref, v_ref, cq_ref, sq_ref, ck_ref, sk_ref, o_ref, klo_ref, khi_ref, vt_ref,
                 *, seq, tq):
    qi = pl.program_id(1)
    win = tq + 2 * WINDOW
    half_lanes = lax.broadcasted_iota(I32, (1, LANES), 1) < ATT_HEAD_DIM
    heads_per_slab = LANES // ATT_HEAD_DIM

    @pl.when(qi == 0)
    def _():
        for j in range(KV_WIDTH // LANES):
            slab = _rope_pair(k_ref[:, j * LANES:(j + 1) * LANES], ck_ref[...], sk_ref[...])
            swapped = pltpu.roll(slab, ATT_HEAD_DIM, 1)
            g0 = j * heads_per_slab
            g1 = g0 + 1
            klo_ref[:, g0 * LANES:(g0 + 1) * LANES] = jnp.where(half_lanes, slab, 0.0).astype(BF16)
            khi_ref[:, g0 * LANES:(g0 + 1) * LANES] = jnp.where(half_lanes, 0.0, swapped).astype(BF16)
            klo_ref[:, g1 * LANES:(g1 + 1) * LANES] = jnp.where(half_lanes, swapped, 0.0).astype(BF16)
            khi_ref[:, g1 * LANES:(g1 + 1) * LANES] = jnp.where(half_lanes, 0.0, slab).astype(BF16)
            vt_ref[j * LANES:(j + 1) * LANES, :] = v_ref[:, j * LANES:(j + 1) * LANES].T.astype(BF16)

    q0 = qi * tq
    ks = pl.multiple_of(jnp.clip(q0 - WINDOW, 0, seq - win), WINDOW)
    kpos = ks + lax.broadcasted_iota(I32, (win, 1), 0)
    qpos = q0 + lax.broadcasted_iota(I32, (1, tq), 1)
    valid = jnp.abs(kpos - qpos) <= WINDOW
    cq = cq_ref[...]
    sq = sq_ref[...]
    group = ATT_HEADS // ATT_KV_HEADS
    scores = []
    for j in range(ATT_WIDTH // LANES):
        qs = (_rope_pair(q_ref[:, j * LANES:(j + 1) * LANES], cq, sq) * (ATT_HEAD_DIM ** -0.5)).astype(BF16)
        for hh in range(heads_per_slab):
            g = (j * heads_per_slab + hh) // group
            k_ref_sel = klo_ref if hh == 0 else khi_ref
            ksel = k_ref_sel[pl.ds(ks, win), g * LANES:(g + 1) * LANES]
            scores.append(lax.dot_general(ksel, qs, NT_DIMS, preferred_element_type=F32))
    for j in range(ATT_WIDTH // LANES):
        outs = []
        for hh in range(heads_per_slab):
            h = j * heads_per_slab + hh
            g = h // group
            st = jnp.where(valid, scores[h], -1e30)
            sk = sink_ref[h]
            m = jnp.maximum(jnp.max(st, axis=0, keepdims=True), sk)
            p = jnp.exp(st - m)
            denom = jnp.sum(p, axis=0, keepdims=True) + jnp.exp(sk - m)
            vt = vt_ref[g * ATT_HEAD_DIM:(g + 1) * ATT_HEAD_DIM, pl.ds(ks, win)]
            ot = jnp.dot(vt, p.astype(BF16), preferred_element_type=F32)
            outs.append(ot / denom)
        o_ref[:, j * LANES:(j + 1) * LANES] = jnp.concatenate(outs, axis=0).T.astype(o_ref.dtype)


def _window_attention(z, sink, cos, sin_signed, batch, seq, tq):
    m = z.shape[0]
    nq = seq // tq
    qblk = COL_Q_ATT // ATT_WIDTH
    kblk = COL_K_ATT // KV_WIDTH
    vblk = COL_V_ATT // KV_WIDTH
    return pl.pallas_call(
        functools.partial(_attn_kernel, seq=seq, tq=tq),
        grid=(batch, nq),
        in_specs=[pl.BlockSpec(memory_space=pltpu.SMEM),
                  pl.BlockSpec((tq, ATT_WIDTH), lambda b, i: (b * nq + i, qblk)),
                  pl.BlockSpec((seq, KV_WIDTH), lambda b, i: (b, kblk)),
                  pl.BlockSpec((seq, KV_WIDTH), lambda b, i: (b, vblk)),
                  pl.BlockSpec((tq, LANES), lambda b, i: (i, 0)),
                  pl.BlockSpec((tq, LANES), lambda b, i: (i, 0)),
                  pl.BlockSpec((seq, LANES), lambda b, i: (0, 0)),
                  pl.BlockSpec((seq, LANES), lambda b, i: (0, 0))],
        out_specs=pl.BlockSpec((tq, ATT_WIDTH), lambda b, i: (b * nq + i, 0)),
        out_shape=jax.ShapeDtypeStruct((m, ATT_WIDTH), BF16),
        scratch_shapes=[pltpu.VMEM((seq, ATT_KV_HEADS * LANES), BF16),
                        pltpu.VMEM((seq, ATT_KV_HEADS * LANES), BF16),
                        pltpu.VMEM((KV_WIDTH, seq), BF16)],
        compiler_params=_params("arbitrary", "arbitrary"),
        name="window_attention",
    )(sink, z, z, z, cos, sin_signed, cos, sin_signed)


def _split_cumsum(tri, x):
    hi = x.astype(BF16)
    r1 = x - hi.astype(F32)
    mid = r1.astype(BF16)
    lo = (r1 - mid.astype(F32)).astype(BF16)
    dot = functools.partial(jnp.dot, preferred_element_type=F32)
    return dot(tri, hi) + dot(tri, mid) + dot(tri, lo)


def _run_interleaved(stage_generators):
    results = [None] * len(stage_generators)
    live = list(range(len(stage_generators)))
    while live:
        for i in list(live):
            try:
                next(stage_generators[i])
            except StopIteration as done:
                results[i] = done.value
                live.remove(i)
    return results


def _hgrn_group(zq, zf, v, lbv, state, mask, tri, forward):
    rows = zq.shape[0]
    c = HG_CHUNK
    nchunk = rows // c
    q = zq * _sigmoid(zq) * (HG_DIM ** -0.5)
    f = lbv + (1.0 - lbv) * _sigmoid(zf)
    logf = jnp.log(f)
    k = 1.0 - f
    yield
    b = _split_cumsum(tri, logf)
    yield
    ref_row = c // 2 - 1 if forward else c // 2
    last_row = c - 1 if forward else 0
    qe, ke, kd, qb, dec = [], [], [], [], []
    for n in range(nchunk):
        sl = slice(n * c, (n + 1) * c)
        bn, qn, kn = b[sl], q[sl], k[sl]
        bref = bn[ref_row:ref_row + 1]
        blast = bn[last_row:last_row + 1]
        qe.append(qn * jnp.exp(bn - bref))
        ke.append(kn * jnp.exp(bref - bn))
        kd.append((kn * jnp.exp(blast - bn)).astype(BF16))
        qb.append((qn * jnp.exp(bn)).astype(BF16))
        dec.append(jnp.exp(blast))
    qe = jnp.concatenate(qe, axis=0).astype(BF16)
    ke = jnp.concatenate(ke, axis=0).astype(BF16)
    vb = v.astype(BF16)
    a = lax.dot_general(qe, ke, NT_DIMS, preferred_element_type=F32)

    zero_blk = jnp.zeros((c, HG_DIM), BF16)

    def block_diag(blocks):
        return jnp.concatenate(
            [jnp.concatenate([blk if j == n else zero_blk for j in range(nchunk)], axis=1)
             for n, blk in enumerate(blocks)], axis=0)

    u_all = lax.dot_general(vb, block_diag(kd), TN_DIMS, preferred_element_type=F32)
    yield
    a = jnp.where(mask, a, 0.0).astype(BF16)
    o = jnp.dot(a, vb, preferred_element_type=F32)
    entering = [None] * nchunk
    order = range(nchunk) if forward else range(nchunk - 1, -1, -1)
    for n in order:
        entering[n] = state.astype(BF16)
        state = state * dec[n] + u_all[:, n * HG_DIM:(n + 1) * HG_DIM]
    o = o + lax.dot_general(block_diag(qb), jnp.concatenate(entering, axis=1), NT_DIMS,
                            preferred_element_type=F32)
    return o, state


def _hgrn_kernel(zq_ref, zff_ref, zfb_ref, zi_ref, zg_ref, lb_ref, gn_ref, o_ref, accf_ref, accb_ref, *, seq, rows):
    ngroups = seq // rows
    r = lax.broadcasted_iota(I32, (rows, rows), 0)
    cidx = lax.broadcasted_iota(I32, (rows, rows), 1)
    same = (r // HG_CHUNK) == (cidx // HG_CHUNK)
    mask_f = jnp.logical_and(same, cidx <= r)
    mask_b = jnp.logical_and(same, cidx >= r)
    tri_f = jnp.where(mask_f, 1.0, 0.0).astype(BF16)
    tri_b = jnp.where(mask_b, 1.0, 0.0).astype(BF16)
    lb_f = lb_ref[0:1, :]
    lb_b = lb_ref[1:2, :]
    zero_state = jnp.zeros((HG_DIM, HG_DIM), F32)

    def scan_body(it, states):
        sf, sb = states
        slf = pl.ds(pl.multiple_of(it * rows, rows), rows)
        slb = pl.ds(pl.multiple_of((ngroups - 1 - it) * rows, rows), rows)
        (of, sf), (ob, sb) = _run_interleaved([
            _hgrn_group(zq_ref[slf, :], zff_ref[slf, :], zi_ref[slf, :], lb_f, sf, mask_f, tri_f, True),
            _hgrn_group(zq_ref[slb, :], zfb_ref[slb, :], zi_ref[slb, :], lb_b, sb, mask_b, tri_b, False)])
        accf_ref[slf, :] = of
        accb_ref[slb, :] = ob
        return sf, sb

    lax.fori_loop(0, ngroups, scan_body, (zero_state, zero_state))

    def readout_body(gi, carry):
        sl = pl.ds(pl.multiple_of(gi * rows, rows), rows)
        tot = accf_ref[sl, :] + accb_ref[sl, :]
        zg = zg_ref[sl, :]
        y = _rms(tot, gn_ref[...]) * (zg * _sigmoid(zg))
        o_ref[sl, :] = y.astype(o_ref.dtype)
        return carry

    lax.fori_loop(0, ngroups, readout_body, 0)


def _hgrn2(z, lb, hg_norm, batch, seq, rows):
    m = z.shape[0]

    def zspec(col):
        base = col // HG_DIM
        return pl.BlockSpec((seq, HG_DIM), lambda b, h: (b, base + h))

    return pl.pallas_call(
        functools.partial(_hgrn_kernel, seq=seq, rows=rows),
        grid=(batch, HG_HEADS),
        in_specs=[zspec(COL_Q_HG), zspec(COL_F_FWD), zspec(COL_F_BWD), zspec(COL_I_HG), zspec(COL_G_HG),
                  pl.BlockSpec((2, HG_DIM), lambda b, h: (0, h)),
                  pl.BlockSpec((1, HG_DIM), lambda b, h: (0, 0))],
        out_specs=pl.BlockSpec((seq, HG_DIM), lambda b, h: (b, h)),
        out_shape=jax.ShapeDtypeStruct((m, HG_WIDTH), BF16),
        scratch_shapes=[pltpu.VMEM((seq, HG_DIM), F32), pltpu.VMEM((seq, HG_DIM), F32)],
        compiler_params=_params("arbitrary", "arbitrary"),
        name="hgrn2",
    )(z, z, z, z, z, lb, hg_norm.reshape(1, HG_DIM))


def _outproj_kernel(a_ref, h_ref, x_ref, wa_ref, wh_ref, o_ref):
    acc = jnp.dot(a_ref[...], wa_ref[...], preferred_element_type=F32)
    acc = acc + jnp.dot(h_ref[...], wh_ref[...], preferred_element_type=F32)
    o_ref[...] = x_ref[...] + acc


def _out_projection(att, hg, x, w_out, tm):
    m, d = x.shape
    return pl.pallas_call(
        _outproj_kernel,
        grid=(m // tm,),
        in_specs=[pl.BlockSpec((tm, ATT_WIDTH), lambda i: (i, 0)),
                  pl.BlockSpec((tm, HG_WIDTH), lambda i: (i, 0)),
                  pl.BlockSpec((tm, d), lambda i: (i, 0)),
                  pl.BlockSpec((ATT_WIDTH, d), lambda i: (0, 0)),
                  pl.BlockSpec((HG_WIDTH, d), lambda i: (ATT_WIDTH // HG_WIDTH, 0))],
        out_specs=pl.BlockSpec((tm, d), lambda i: (i, 0)),
        out_shape=jax.ShapeDtypeStruct((m, d), F32),
        compiler_params=_params("arbitrary"),
        name="out_projection",
    )(att, hg, x, w_out, w_out)


def _cross_kernel(x_ref, g_ref, kv_ref, wq_ref, wo_ref, o_ref):
    x = x_ref[...]
    h = _rms(x, g_ref[...]).astype(BF16)
    q = jnp.dot(h, wq_ref[...], preferred_element_type=F32).astype(BF16)
    scores = []
    for hd in range(X_HEADS):
        sl = slice(hd * X_HEAD_DIM, (hd + 1) * X_HEAD_DIM)
        scores.append(lax.dot_general(q[:, sl], kv_ref[:, sl], NT_DIMS, preferred_element_type=F32))
    outs = []
    for hd in range(X_HEADS):
        vh = kv_ref[:, X_WIDTH + hd * X_HEAD_DIM:X_WIDTH + (hd + 1) * X_HEAD_DIM]
        s = scores[hd] * (X_HEAD_DIM ** -0.5)
        p = jnp.exp(s - jnp.max(s, axis=-1, keepdims=True))
        denom = jnp.sum(p, axis=-1, keepdims=True)
        outs.append(jnp.dot(p.astype(BF16), vh, preferred_element_type=F32) / denom)
    oc = jnp.concatenate(outs, axis=-1).astype(BF16)
    o_ref[...] = x + jnp.dot(oc, wo_ref[...], preferred_element_type=F32)


def _cross_attention(x, g, kv, w_xq, w_xo, batch, seq, n_mem, tm):
    m, d = x.shape
    nt = seq // tm
    return pl.pallas_call(
        _cross_kernel,
        grid=(batch, nt),
        in_specs=[pl.BlockSpec((tm, d), lambda b, i: (b * nt + i, 0)),
                  pl.BlockSpec((1, d), lambda b, i: (0, 0)),
                  pl.BlockSpec((n_mem, 2 * X_WIDTH), lambda b, i: (b, 0)),
                  pl.BlockSpec((d, X_WIDTH), lambda b, i: (0, 0)),
                  pl.BlockSpec((X_WIDTH, d), lambda b, i: (0, 0))],
        out_specs=pl.BlockSpec((tm, d), lambda b, i: (b * nt + i, 0)),
        out_shape=jax.ShapeDtypeStruct((m, d), F32),
        compiler_params=_params("arbitrary", "arbitrary"),
        name="cross_attention",
    )(x, g.reshape(1, d), kv, w_xq, w_xo)


def _router_kernel(x_ref, g_ref, wr_ref, br_ref, h_ref, idx_ref, gate_ref, rank_ref, cnt_ref, base_ref):
    tm = x_ref.shape[0]

    @pl.when(pl.program_id(0) == 0)
    def _():
        base_ref[...] = jnp.zeros_like(base_ref)

    h = _rms(x_ref[...], g_ref[...])
    hh = h.astype(BF16)
    half = h.shape[1] // 2
    bits = lax.bitcast_convert_type(hh.astype(F32), U32)
    h_ref[...] = (bits[:, :half] >> 16) | (bits[:, half:] & jnp.uint32(0xFFFF0000))
    hl = (h - hh.astype(F32)).astype(BF16)
    w = wr_ref[...]
    wh = w.astype(BF16)
    wl = (w - wh.astype(F32)).astype(BF16)
    nt = functools.partial(lax.dot_general, dimension_numbers=NT_DIMS, preferred_element_type=F32)
    logits = nt(wh, hh) + nt(wh, hl) + nt(wl, hh) + br_ref[...]

    eio = lax.broadcasted_iota(I32, (N_EXPERTS, tm), 0).astype(F32)
    work = logits
    vals, onehots = [], []
    for k in range(TOP_K):
        mx = jnp.max(work, axis=0, keepdims=True)
        ix = jnp.min(jnp.where(work == mx, eio, float(N_EXPERTS)), axis=0, keepdims=True)
        sel = eio == ix
        vals.append(mx)
        onehots.append(sel)
        idx_ref[k:k + 1, :] = ix.astype(I32)
        work = jnp.where(sel, -jnp.inf, work)
    ex = [jnp.exp(v - vals[0]) for v in vals]
    denom = ex[0] + ex[1] + ex[2] + ex[3]
    for k in range(TOP_K):
        gate_ref[k:k + 1, :] = ex[k] / denom

    oh = [jnp.where(s, 1.0, 0.0) for s in onehots]
    oh_all = oh[0] + oh[1] + oh[2] + oh[3]
    r = lax.broadcasted_iota(I32, (tm, tm), 0)
    c = lax.broadcasted_iota(I32, (tm, tm), 1)
    upper = jnp.where(r < c, 1.0, 0.0).astype(BF16)
    before = jnp.dot(oh_all.astype(BF16), upper, preferred_element_type=F32) + base_ref[...]
    for k in range(TOP_K):
        rank_ref[k:k + 1, :] = jnp.sum(oh[k] * before, axis=0, keepdims=True).astype(I32)
    base_ref[...] = base_ref[...] + jnp.sum(oh_all, axis=1, keepdims=True)
    cnt_ref[...] = base_ref[...]


def _router(x, g, w_router_t, b_router, tm):
    m, d = x.shape
    row4 = pl.BlockSpec((TOP_K, tm), lambda i: (0, i))
    return pl.pallas_call(
        _router_kernel,
        grid=(m // tm,),
        in_specs=[pl.BlockSpec((tm, d), lambda i: (i, 0)),
                  pl.BlockSpec((1, d), lambda i: (0, 0)),
                  pl.BlockSpec((N_EXPERTS, d), lambda i: (0, 0)),
                  pl.BlockSpec((N_EXPERTS, 1), lambda i: (0, 0))],
        out_specs=[pl.BlockSpec((tm, d // 2), lambda i: (i, 0)), row4, row4, row4,
                   pl.BlockSpec((N_EXPERTS, 1), lambda i: (0, 0))],
        out_shape=[jax.ShapeDtypeStruct((m, d // 2), U32),
                   jax.ShapeDtypeStruct((TOP_K, m), I32),
                   jax.ShapeDtypeStruct((TOP_K, m), F32),
                   jax.ShapeDtypeStruct((TOP_K, m), I32),
                   jax.ShapeDtypeStruct((N_EXPERTS, 1), F32)],
        scratch_shapes=[pltpu.VMEM((N_EXPERTS, 1), F32)],
        compiler_params=_params("arbitrary"),
        name="router",
    )(x, g.reshape(1, d), w_router_t, b_router.reshape(N_EXPERTS, 1))


def _assign_rows_kernel(pstart_ref, idx_ref, rank_ref, dest_ref):
    idx = idx_ref[...]
    start = jnp.zeros_like(idx)
    for e in range(N_EXPERTS):
        start = jnp.where(idx == e, pstart_ref[e], start)
    dest_ref[...] = start + rank_ref[...]


def _assign_rows(pstart, idx, rank):
    full = pl.BlockSpec(idx.shape, lambda: (0, 0))
    return pl.pallas_call(
        _assign_rows_kernel,
        in_specs=[pl.BlockSpec(memory_space=pltpu.SMEM), full, full],
        out_specs=full,
        out_shape=jax.ShapeDtypeStruct(idx.shape, I32),
        name="assign_rows",
    )(pstart, idx, rank)


def _dispatch_kernel(seg_ref, dest_hbm, h_ref, xs_hbm, dsm, zrow, sem_idx, sem_row, *, m_tokens, p_rows):
    i = pl.program_id(0)
    tt = SMEM_STAGE

    def idx_copy(k):
        return pltpu.make_async_copy(dest_hbm.at[pl.ds(k * m_tokens + i * tt, tt)],
                                     dsm.at[pl.ds(k * tt, tt)], sem_idx)

    for k in range(TOP_K):
        idx_copy(k).start()
    for k in range(TOP_K):
        idx_copy(k).wait()

    def row_copy(t, dst_row):
        return pltpu.make_async_copy(h_ref.at[pl.ds(t, 1)], xs_hbm.at[pl.ds(dst_row, 1)], sem_row)

    def issue(t, carry):
        for k in range(TOP_K):
            row_copy(t, dsm[k * tt + t]).start(priority=k % 2)
        return carry

    lax.fori_loop(0, tt, issue, 0, unroll=8)

    for k in range(TOP_K):
        pltpu.make_async_copy(h_ref, xs_hbm.at[pl.ds(0, tt)], sem_row).wait()

    @pl.when(i == 0)
    def _():
        zrow[...] = jnp.zeros_like(zrow)

        def zero_copy(dst_row):
            return pltpu.make_async_copy(zrow.at[pl.ds(0, 1)], xs_hbm.at[pl.ds(dst_row, 1)], sem_row)

        def fill(lo, hi):
            def body(rw, carry):
                zero_copy(rw).start()
                return carry

            lax.fori_loop(lo, hi, body, 0)

            def wbody(rw, carry):
                zero_copy(0).wait()
                return carry

            lax.fori_loop(lo, hi, wbody, 0)

        def per_expert(e, carry):
            fill(seg_ref[0, e] + seg_ref[1, e], seg_ref[2, e])
            return carry

        lax.fori_loop(0, N_EXPERTS, per_expert, 0)
        fill(seg_ref[2, N_EXPERTS - 1], p_rows)


def _dispatch(seg, dest_flat, h, p_rows):
    m, d = h.shape
    return pl.pallas_call(
        functools.partial(_dispatch_kernel, m_tokens=m, p_rows=p_rows),
        grid=(m // SMEM_STAGE,),
        in_specs=[pl.BlockSpec(memory_space=pltpu.SMEM),
                  pl.BlockSpec(memory_space=pl.ANY),
                  pl.BlockSpec((SMEM_STAGE, d), lambda i: (i, 0))],
        out_specs=pl.BlockSpec(memory_space=pl.ANY),
        out_shape=jax.ShapeDtypeStruct((p_rows, d), h.dtype),
        scratch_shapes=[pltpu.SMEM((TOP_K * SMEM_STAGE,), I32),
                        pltpu.VMEM((8, d), h.dtype),
                        pltpu.SemaphoreType.DMA,
                        pltpu.SemaphoreType.DMA],
        compiler_params=_params("arbitrary"),
        name="dispatch",
    )(seg, dest_flat, h)


def _expert_kernel(te_ref, tv_ref, nu_ref, x_ref, w1g_ref, w1l_ref, b1g_ref, b1l_ref, w2_ref, b2_ref,
                   o_ref, xb_ref):
    i = pl.program_id(0)
    f = pl.program_id(1)
    nf = pl.num_programs(1)
    live = tv_ref[i] > 0

    @pl.when(live)
    def _():
        @pl.when(f == 0)
        def _():
            words = x_ref[...]
            half = words.shape[1]
            xb_ref[:, :half] = lax.bitcast_convert_type(words << 16, F32).astype(BF16)
            xb_ref[:, half:] = lax.bitcast_convert_type(words & jnp.uint32(0xFFFF0000), F32).astype(BF16)
            o_ref[...] = jnp.broadcast_to(b2_ref[0], o_ref.shape)

        xb = xb_ref[...]
        glu = jnp.dot(xb, w1g_ref[0], preferred_element_type=F32) + b1g_ref[0]
        lin = jnp.dot(xb, w1l_ref[0], preferred_element_type=F32) + b1l_ref[0]
        glu = jnp.minimum(glu, SWIGLU_LIMIT)
        lin = jnp.clip(lin, -SWIGLU_LIMIT, SWIGLU_LIMIT)
        act = glu * _sigmoid(SWIGLU_ALPHA * glu) * (lin + 1.0)
        o_ref[...] += jnp.dot(act.astype(BF16), w2_ref[0], preferred_element_type=F32)

    @pl.when(jnp.logical_and(jnp.logical_not(live), f == nf - 1))
    def _():
        o_ref[...] = jnp.zeros_like(o_ref)


def _expert_mlp(tile_e, tile_valid, n_used, xs, w1, b1, w2, b2, tmb, tf):
    p_rows = xs.shape[0]
    d = w1.shape[1]
    d_ff = w2.shape[1]
    nf = d_ff // tf
    n_tiles = p_rows // tmb

    def fsel(i, f, tv):
        return jnp.where(tv[i] > 0, f, nf - 1)

    grid_spec = pltpu.PrefetchScalarGridSpec(
        num_scalar_prefetch=3,
        grid=(n_tiles, nf),
        in_specs=[
            pl.BlockSpec((tmb, d // 2), lambda i, f, te, tv, nu: (jnp.minimum(i, nu[0] - 1), 0)),
            pl.BlockSpec((1, d, tf), lambda i, f, te, tv, nu: (te[i], 0, fsel(i, f, tv))),
            pl.BlockSpec((1, d, tf), lambda i, f, te, tv, nu: (te[i], 0, nf + fsel(i, f, tv))),
            pl.BlockSpec((1, 1, tf), lambda i, f, te, tv, nu: (te[i], 0, fsel(i, f, tv))),
            pl.BlockSpec((1, 1, tf), lambda i, f, te, tv, nu: (te[i], 0, nf + fsel(i, f, tv))),
            pl.BlockSpec((1, tf, d), lambda i, f, te, tv, nu: (te[i], fsel(i, f, tv), 0)),
            pl.BlockSpec((1, 1, d), lambda i, f, te, tv, nu: (te[i], 0, 0)),
        ],
        out_specs=pl.BlockSpec((tmb, d), lambda i, f, te, tv, nu: (i, 0)),
        scratch_shapes=[pltpu.VMEM((tmb, d), BF16)],
    )
    return pl.pallas_call(
        _expert_kernel,
        grid_spec=grid_spec,
        out_shape=jax.ShapeDtypeStruct((p_rows, d), F32),
        compiler_params=_params("arbitrary", "arbitrary"),
        name="expert_mlp",
    )(tile_e, tile_valid, n_used, xs, w1, w1, b1, b1, w2, b2)


def _combine_kernel(dest_hbm, ys_hbm, x_ref, gate_ref, g_ref, o_ref, dsm, buf, sem_idx, sem_row, *, m_tokens, tc):
    i = pl.program_id(0)
    n = pl.num_programs(0)
    per_stage = SMEM_STAGE // tc

    def issue_tile(tile):
        slot = tile % 2
        off = (tile % per_stage) * tc

        @pl.when(tile % per_stage == 0)
        def _():
            base = (tile // per_stage) * SMEM_STAGE

            def idx_copy(k):
                return pltpu.make_async_copy(dest_hbm.at[pl.ds(k * m_tokens + base, SMEM_STAGE)],
                                             dsm.at[pl.ds(k * SMEM_STAGE, SMEM_STAGE)], sem_idx)

            for k in range(TOP_K):
                idx_copy(k).start()
            for k in range(TOP_K):
                idx_copy(k).wait()

        def issue(t, carry):
            for k in range(TOP_K):
                pltpu.make_async_copy(ys_hbm.at[pl.ds(dsm[k * SMEM_STAGE + off + t], 1)],
                                      buf.at[slot, k, pl.ds(t, 1)], sem_row.at[slot]).start(priority=k % 2)
            return carry

        lax.fori_loop(0, tc, issue, 0, unroll=8)

    @pl.when(i == 0)
    def _():
        issue_tile(i)

    @pl.when(i + 1 < n)
    def _():
        issue_tile(i + 1)

    slot = i % 2
    for k in range(TOP_K):
        pltpu.make_async_copy(ys_hbm.at[pl.ds(0, tc)], buf.at[slot, k], sem_row.at[slot]).wait()

    gate = gate_ref[...]
    y = x_ref[...]
    for k in range(TOP_K):
        y = y + buf[slot, k] * gate[:, k:k + 1]
    o_ref[...] = _rms(y, g_ref[...])


def _combine(dest_flat, ys, x, gate_t, g, tc):
    m, d = x.shape
    return pl.pallas_call(
        functools.partial(_combine_kernel, m_tokens=m, tc=tc),
        grid=(m // tc,),
        in_specs=[pl.BlockSpec(memory_space=pl.ANY),
                  pl.BlockSpec(memory_space=pl.ANY),
                  pl.BlockSpec((tc, d), lambda i: (i, 0)),
                  pl.BlockSpec((tc, TOP_K), lambda i: (i, 0)),
                  pl.BlockSpec((1, d), lambda i: (0, 0))],
        out_specs=pl.BlockSpec((tc, d), lambda i: (i, 0)),
        out_shape=jax.ShapeDtypeStruct((m, d), F32),
        scratch_shapes=[pltpu.SMEM((TOP_K * SMEM_STAGE,), I32),
                        pltpu.VMEM((2, TOP_K, tc, d), F32),
                        pltpu.SemaphoreType.DMA,
                        pltpu.SemaphoreType.DMA((2,))],
        compiler_params=_params("arbitrary"),
        name="combine",
    )(dest_flat, ys, x, gate_t, g.reshape(1, d))


def _rope_tables(seq):
    half = ATT_HEAD_DIM // 2
    inv = ROPE_THETA ** (-jnp.arange(0, ATT_HEAD_DIM, 2, dtype=F32) / ATT_HEAD_DIM)
    ang = jnp.arange(seq, dtype=F32)[:, None] * inv[None, :]
    cos = jnp.tile(jnp.cos(ang), (1, LANES // half))
    sin = jnp.sin(ang)
    sin_signed = jnp.tile(jnp.concatenate([-sin, sin], axis=-1), (1, LANES // ATT_HEAD_DIM))
    return cos, sin_signed


def _pick(n, prefs):
    for p in prefs:
        if n % p == 0:
            return p
    return n


def _trunk(x, mem, wts, moe_tile, ff_tile):
    batch, seq, d = x.shape
    n_mem = mem.shape[1]
    m = batch * seq
    x2d = x.reshape(m, d)
    tm = _pick(m, (512, 256, 128))

    n_in = wts["w_in"].shape[1]
    z = _norm_matmul(x2d, wts["norm_mix"], wts["w_in"], tm, _pick(n_in, (1664, 1024, 512, 256, 128)), F32,
                     "in_projection")

    cos, sin_signed = _rope_tables(seq)
    att = _window_attention(z, wts["att_sink"], cos, sin_signed, batch, seq, _pick(seq, (256, 128)))
    hg = _hgrn2(z, wts["lb"], wts["hg_norm"], batch, seq, _pick(seq, (256, 128, 64)))
    x1 = _out_projection(att, hg, x2d, wts["w_out"], tm)

    kv = _norm_matmul(mem.reshape(batch * n_mem, d), wts["norm_mem"], wts["w_xkv"], n_mem, 2 * X_WIDTH, BF16,
                      "memory_kv")
    x2 = _cross_attention(x1, wts["norm_cross"], kv, wts["w_xq"], wts["w_xo"], batch, seq, n_mem,
                          _pick(seq, (512, 256, 128)))

    h3, idx, gate, rank, cnt = _router(x2, wts["norm_ffn"], wts["w_router_t"], wts["b_router"], tm)

    counts = cnt[:, 0].astype(I32)
    padded = (counts + moe_tile - 1) // moe_tile * moe_tile
    pend = jnp.cumsum(padded)
    pstart = pend - padded
    dest = _assign_rows(pstart.astype(I32), idx, rank)
    n_tiles = -(-(m * TOP_K) // moe_tile) + N_EXPERTS
    p_rows = n_tiles * moe_tile
    tile_row = jnp.arange(n_tiles, dtype=I32) * moe_tile
    tile_e = jnp.minimum(jnp.sum((pend[None, :] <= tile_row[:, None]).astype(I32), axis=1), N_EXPERTS - 1)
    owner = tile_e[:, None] == jnp.arange(N_EXPERTS, dtype=I32)[None, :]
    seg_end = jnp.sum(jnp.where(owner, (pstart + counts)[None, :], 0), axis=1)
    tile_valid = jnp.clip(seg_end - tile_row, 0, moe_tile).astype(I32)
    n_used = (pend[-1:] // moe_tile).astype(I32)
    seg = jnp.stack([pstart, counts, pend]).astype(I32)
    dest_flat = dest.reshape(-1)

    xs = _dispatch(seg, dest_flat, h3, p_rows)
    ys = _expert_mlp(tile_e, tile_valid, n_used, xs, wts["w_moe1"], wts["b_moe1"], wts["w_moe2"], wts["b_moe2"],
                     moe_tile, ff_tile)
    y = _combine(dest_flat, ys, x2, gate.T, wts["norm_final"], _pick(m, (256, 128)))
    return y.reshape(batch, seq, d)


def kernel(x_prompt, x_sample, mem_prompt, mem_sample, norm_mix, w_in, att_sink, hg_lb_logits, hg_norm, w_out,
           norm_cross, norm_mem, w_xq, w_xkv, w_xo, norm_ffn, w_router, b_router, w_moe1, b_moe1, w_moe2, b_moe2,
           norm_final):
    assert w_in.shape[0] == 1, "the final norm is fused after the single layer"
    lb_all = jnp.cumsum(jax.nn.softmax(hg_lb_logits.astype(F32), axis=0), axis=0)
    d_ff = w_moe2.shape[2]
    wts = dict(
        norm_mix=norm_mix[0], w_in=w_in[0].astype(BF16), att_sink=att_sink[0], lb=lb_all[0],
        hg_norm=hg_norm[0], w_out=w_out[0].astype(BF16), norm_cross=norm_cross[0], norm_mem=norm_mem[0],
        w_xq=w_xq[0].astype(BF16), w_xkv=w_xkv[0].astype(BF16), w_xo=w_xo[0].astype(BF16),
        norm_ffn=norm_ffn[0], w_router_t=w_router[0].T, b_router=b_router[0],
        w_moe1=w_moe1[0].astype(BF16), b_moe1=b_moe1[0][:, None, :],
        w_moe2=w_moe2[0].astype(BF16), b_moe2=b_moe2[0][:, None, :],
        norm_final=norm_final,
    )
    ff_tile = _pick(d_ff, (1024, 512, 256, 128))
    return (_trunk(x_prompt, mem_prompt, wts, MOE_TILE, ff_tile),
            _trunk(x_sample, mem_sample, wts, MOE_TILE, ff_tile))
```

```python
import functools

import jax
import jax.numpy as jnp
from jax import lax
from jax.experimental import pallas as pl
from jax.experimental.pallas import tpu as pltpu

F32 = jnp.float32
BF16 = jnp.bfloat16
I32 = jnp.int32
U32 = jnp.uint32

EPS = 1e-5
ATT_HEADS = 16
ATT_KV_HEADS = 4
ATT_HEAD_DIM = 64
ATT_WIDTH = ATT_HEADS * ATT_HEAD_DIM
KV_WIDTH = ATT_KV_HEADS * ATT_HEAD_DIM
WINDOW = 128
ROPE_THETA = 10000.0
HG_HEADS = 8
HG_DIM = 128
HG_WIDTH = HG_HEADS * HG_DIM
HG_CHUNK = 64
X_HEADS = 4
X_HEAD_DIM = 128
X_WIDTH = X_HEADS * X_HEAD_DIM
N_EXPERTS = 32
TOP_K = 4
SWIGLU_ALPHA = 1.702
SWIGLU_LIMIT = 7.0

COL_Q_ATT = 0
COL_K_ATT = ATT_WIDTH
COL_V_ATT = ATT_WIDTH + KV_WIDTH
COL_Q_HG = ATT_WIDTH + 2 * KV_WIDTH
COL_F_FWD = COL_Q_HG + HG_WIDTH
COL_F_BWD = COL_F_FWD + HG_WIDTH
COL_I_HG = COL_F_BWD + HG_WIDTH
COL_G_HG = COL_I_HG + HG_WIDTH

LANES = 128
VMEM_LIMIT = 56 * 1024 * 1024
SMEM_STAGE = 1024
MOE_TILE = 512

NT_DIMS = (((1,), (1,)), ((), ()))
TN_DIMS = (((0,), (0,)), ((), ()))


def _params(*sem):
    return pltpu.CompilerParams(dimension_semantics=sem, vmem_limit_bytes=VMEM_LIMIT)


def _rms(x, g):
    ms = jnp.mean(x * x, axis=-1, keepdims=True)
    return x * lax.rsqrt(ms + EPS) * g


def _sigmoid(x):
    return 1.0 / (1.0 + jnp.exp(-x))


def _norm_matmul_kernel(x_ref, g_ref, w_ref, o_ref, xn_ref):
    @pl.when(pl.program_id(1) == 0)
    def _():
        xn_ref[...] = _rms(x_ref[...], g_ref[...]).astype(BF16)

    o_ref[...] = jnp.dot(xn_ref[...], w_ref[...], preferred_element_type=F32).astype(o_ref.dtype)


def _norm_matmul(x, g, w, tm, tn, out_dtype, name):
    m, d = x.shape
    n = w.shape[1]
    return pl.pallas_call(
        _norm_matmul_kernel,
        grid=(m // tm, n // tn),
        in_specs=[pl.BlockSpec((tm, d), lambda i, j: (i, 0)),
                  pl.BlockSpec((1, d), lambda i, j: (0, 0)),
                  pl.BlockSpec((d, tn), lambda i, j: (0, j))],
        out_specs=pl.BlockSpec((tm, tn), lambda i, j: (i, j)),
        out_shape=jax.ShapeDtypeStruct((m, n), out_dtype),
        scratch_shapes=[pltpu.VMEM((tm, d), BF16)],
        compiler_params=_params("arbitrary", "arbitrary"),
        name=name,
    )(x, g.reshape(1, d), w)


def _rope_pair(x, cos, sin_signed):
    lane = lax.broadcasted_iota(I32, (1, LANES), 1)
    first_half = (lane % ATT_HEAD_DIM) < (ATT_HEAD_DIM // 2)
    rot = jnp.where(first_half, pltpu.roll(x, LANES - ATT_HEAD_DIM // 2, 1), pltpu.roll(x, ATT_HEAD_DIM // 2, 1))
    return x * cos + rot * sin_signed


def _attn_kernel(sink_ref, q_ref, k_ref, v_ref, cq_ref, sq_ref, ck_ref, sk_ref, o_ref, klo_ref, khi_ref, vt_ref,
                 *, seq, tq):
    qi = pl.program_id(1)
    win = tq + 2 * WINDOW
    half_lanes = lax.broadcasted_iota(I32, (1, LANES), 1) < ATT_HEAD_DIM
    heads_per_slab = LANES // ATT_HEAD_DIM

    @pl.when(qi == 0)
    def _():
        for j in range(KV_WIDTH // LANES):
            slab = _rope_pair(k_ref[:, j * LANES:(j + 1) * LANES], ck_ref[...], sk_ref[...])
            swapped = pltpu.roll(slab, ATT_HEAD_DIM, 1)
            g0 = j * heads_per_slab
            g1 = g0 + 1
            klo_ref[:, g0 * LANES:(g0 + 1) * LANES] = jnp.where(half_lanes, slab, 0.0).astype(BF16)
            khi_ref[:, g0 * LANES:(g0 + 1) * LANES] = jnp.where(half_lanes, 0.0, swapped).astype(BF16)
            klo_ref[:, g1 * LANES:(g1 + 1) * LANES] = jnp.where(half_lanes, swapped, 0.0).astype(BF16)
            khi_ref[:, g1 * LANES:(g1 + 1) * LANES] = jnp.where(half_lanes, 0.0, slab).astype(BF16)
            vt_ref[j * LANES:(j + 1) * LANES, :] = v_ref[:, j * LANES:(j + 1) * LANES].T.astype(BF16)

    q0 = qi * tq
    ks = pl.multiple_of(jnp.clip(q0 - WINDOW, 0, seq - win), WINDOW)
    kpos = ks + lax.broadcasted_iota(I32, (win, 1), 0)
    qpos = q0 + lax.broadcasted_iota(I32, (1, tq), 1)
    valid = jnp.abs(kpos - qpos) <= WINDOW
    cq = cq_ref[...]
    sq = sq_ref[...]
    group = ATT_HEADS // ATT_KV_HEADS
    scores = []
    for j in range(ATT_WIDTH // LANES):
        qs = (_rope_pair(q_ref[:, j * LANES:(j + 1) * LANES], cq, sq) * (ATT_HEAD_DIM ** -0.5)).astype(BF16)
        for hh in range(heads_per_slab):
            g = (j * heads_per_slab + hh) // group
            k_ref_sel = klo_ref if hh == 0 else khi_ref
            ksel = k_ref_sel[pl.ds(ks, win), g * LANES:(g + 1) * LANES]
            scores.append(lax.dot_general(ksel, qs, NT_DIMS, preferred_element_type=F32))
    for j in range(ATT_WIDTH // LANES):
        outs = []
        for hh in range(heads_per_slab):
            h = j * heads_per_slab + hh
            g = h // group
            st = jnp.where(valid, scores[h], -1e30)
            sk = sink_ref[h]
            m = jnp.maximum(jnp.max(st, axis=0, keepdims=True), sk)
            p = jnp.exp(st - m)
            denom = jnp.sum(p, axis=0, keepdims=True) + jnp.exp(sk - m)
            vt = vt_ref[g * ATT_HEAD_DIM:(g + 1) * ATT_HEAD_DIM, pl.ds(ks, win)]
            ot = jnp.dot(vt, p.astype(BF16), preferred_element_type=F32)
            outs.append(ot / denom)
        o_ref[:, j * LANES:(j + 1) * LANES] = jnp.concatenate(outs, axis=0).T.astype(o_ref.dtype)


def _window_attention(z, sink, cos, sin_signed, batch, seq, tq):
    m = z.shape[0]
    nq = seq // tq
    qblk = COL_Q_ATT // ATT_WIDTH
    kblk = COL_K_ATT // KV_WIDTH
    vblk = COL_V_ATT // KV_WIDTH
    return pl.pallas_call(
        functools.partial(_attn_kernel, seq=seq, tq=tq),
        grid=(batch, nq),
        in_specs=[pl.BlockSpec(memory_space=pltpu.SMEM),
                  pl.BlockSpec((tq, ATT_WIDTH), lambda b, i: (b * nq + i, qblk)),
                  pl.BlockSpec((seq, KV_WIDTH), lambda b, i: (b, kblk)),
                  pl.BlockSpec((seq, KV_WIDTH), lambda b, i: (b, vblk)),
                  pl.BlockSpec((tq, LANES), lambda b, i: (i, 0)),
                  pl.BlockSpec((tq, LANES), lambda b, i: (i, 0)),
                  pl.BlockSpec((seq, LANES), lambda b, i: (0, 0)),
                  pl.BlockSpec((seq, LANES), lambda b, i: (0, 0))],
        out_specs=pl.BlockSpec((tq, ATT_WIDTH), lambda b, i: (b * nq + i, 0)),
        out_shape=jax.ShapeDtypeStruct((m, ATT_WIDTH), BF16),
        scratch_shapes=[pltpu.VMEM((seq, ATT_KV_HEADS * LANES), BF16),
                        pltpu.VMEM((seq, ATT_KV_HEADS * LANES), BF16),
                        pltpu.VMEM((KV_WIDTH, seq), BF16)],
        compiler_params=_params("arbitrary", "arbitrary"),
        name="window_attention",
    )(sink, z, z, z, cos, sin_signed, cos, sin_signed)


def _split_cumsum(tri, x):
    hi = x.astype(BF16)
    r1 = x - hi.astype(F32)
    mid = r1.astype(BF16)
    lo = (r1 - mid.astype(F32)).astype(BF16)
    dot = functools.partial(jnp.dot, preferred_element_type=F32)
    return dot(tri, hi) + dot(tri, mid) + dot(tri, lo)


def _run_interleaved(stage_generators):
    results = [None] * len(stage_generators)
    live = list(range(len(stage_generators)))
    while live:
        for i in list(live):
            try:
                next(stage_generators[i])
            except StopIteration as done:
                results[i] = done.value
                live.remove(i)
    return results


def _hgrn_group(zq, zf, v, lbv, state, mask, tri, forward):
    rows = zq.shape[0]
    c = HG_CHUNK
    nchunk = rows // c
    q = zq * _sigmoid(zq) * (HG_DIM ** -0.5)
    f = lbv + (1.0 - lbv) * _sigmoid(zf)
    logf = jnp.log(f)
    k = 1.0 - f
    yield
    b = _split_cumsum(tri, logf)
    yield
    ref_row = c // 2 - 1 if forward else c // 2
    last_row = c - 1 if forward else 0
    qe, ke, kd, qb, dec = [], [], [], [], []
    for n in range(nchunk):
        sl = slice(n * c, (n + 1) * c)
        bn, qn, kn = b[sl], q[sl], k[sl]
        bref = bn[ref_row:ref_row + 1]
        blast = bn[last_row:last_row + 1]
        qe.append(qn * jnp.exp(bn - bref))
        ke.append(kn * jnp.exp(bref - bn))
        kd.append((kn * jnp.exp(blast - bn)).astype(BF16))
        qb.append((qn * jnp.exp(bn)).astype(BF16))
        dec.append(jnp.exp(blast))
    qe = jnp.concatenate(qe, axis=0).astype(BF16)
    ke = jnp.concatenate(ke, axis=0).astype(BF16)
    vb = v.astype(BF16)
    a = lax.dot_general(qe, ke, NT_DIMS, preferred_element_type=F32)

    zero_blk = jnp.zeros((c, HG_DIM), BF16)

    def block_diag(blocks):
        return jnp.concatenate(
            [jnp.concatenate([blk if j == n else zero_blk for j in range(nchunk)], axis=1)
             for n, blk in enumerate(blocks)], axis=0)

    u_all = lax.dot_general(vb, block_diag(kd), TN_DIMS, preferred_element_type=F32)
    yield
    a = jnp.where(mask, a, 0.0).astype(BF16)
    o = jnp.dot(a, vb, preferred_element_type=F32)
    entering = [None] * nchunk
    order = range(nchunk) if forward else range(nchunk - 1, -1, -1)
    for n in order:
        entering[n] = state.astype(BF16)
        state = state * dec[n] + u_all[:, n * HG_DIM:(n + 1) * HG_DIM]
    o = o + lax.dot_general(block_diag(qb), jnp.concatenate(entering, axis=1), NT_DIMS,
                            preferred_element_type=F32)
    return o, state


def _hgrn_kernel(zq_ref, zff_ref, zfb_ref, zi_ref, zg_ref, lb_ref, gn_ref, o_ref, accf_ref, accb_ref, *, seq, rows):
    ngroups = seq // rows
    r = lax.broadcasted_iota(I32, (rows, rows), 0)
    cidx = lax.broadcasted_iota(I32, (rows, rows), 1)
    same = (r // HG_CHUNK) == (cidx // HG_CHUNK)
    mask_f = jnp.logical_and(same, cidx <= r)
    mask_b = jnp.logical_and(same, cidx >= r)
    tri_f = jnp.where(mask_f, 1.0, 0.0).astype(BF16)
    tri_b = jnp.where(mask_b, 1.0, 0.0).astype(BF16)
    lb_f = lb_ref[0:1, :]
    lb_b = lb_ref[1:2, :]
    zero_state = jnp.zeros((HG_DIM, HG_DIM), F32)

    def scan_body(it, states):
        sf, sb = states
        slf = pl.ds(pl.multiple_of(it * rows, rows), rows)
        slb = pl.ds(pl.multiple_of((ngroups - 1 - it) * rows, rows), rows)
        (of, sf), (ob, sb) = _run_interleaved([
            _hgrn_group(zq_ref[slf, :], zff_ref[slf, :], zi_ref[slf, :], lb_f, sf, mask_f, tri_f, True),
            _hgrn_group(zq_ref[slb, :], zfb_ref[slb, :], zi_ref[slb, :], lb_b, sb, mask_b, tri_b, False)])
        accf_ref[slf, :] = of
        accb_ref[slb, :] = ob
        return sf, sb

    lax.fori_loop(0, ngroups, scan_body, (zero_state, zero_state))

    def readout_body(gi, carry):
        sl = pl.ds(pl.multiple_of(gi * rows, rows), rows)
        tot = accf_ref[sl, :] + accb_ref[sl, :]
        zg = zg_ref[sl, :]
        y = _rms(tot, gn_ref[...]) * (zg * _sigmoid(zg))
        o_ref[sl, :] = y.astype(o_ref.dtype)
        return carry

    lax.fori_loop(0, ngroups, readout_body, 0)


def _hgrn2(z, lb, hg_norm, batch, seq, rows):
    m = z.shape[0]

    def zspec(col):
        base = col // HG_DIM
        return pl.BlockSpec((seq, HG_DIM), lambda b, h: (b, base + h))

    return pl.pallas_call(
        functools.partial(_hgrn_kernel, seq=seq, rows=rows),
        grid=(batch, HG_HEADS),
        in_specs=[zspec(COL_Q_HG), zspec(COL_F_FWD), zspec(COL_F_BWD), zspec(COL_I_HG), zspec(COL_G_HG),
                  pl.BlockSpec((2, HG_DIM), lambda b, h: (0, h)),
                  pl.BlockSpec((1, HG_DIM), lambda b, h: (0, 0))],
        out_specs=pl.BlockSpec((seq, HG_DIM), lambda b, h: (b, h)),
        out_shape=jax.ShapeDtypeStruct((m, HG_WIDTH), BF16),
        scratch_shapes=[pltpu.VMEM((seq, HG_DIM), F32), pltpu.VMEM((seq, HG_DIM), F32)],
        compiler_params=_params("arbitrary", "arbitrary"),
        name="hgrn2",
    )(z, z, z, z, z, lb, hg_norm.reshape(1, HG_DIM))


def _out_projection_math(a_ref, hg_ref, x_ref, wa_ref, wh_ref):
    acc = jnp.dot(a_ref[...], wa_ref[...], preferred_element_type=F32)
    acc = acc + jnp.dot(hg_ref[...], wh_ref[...], preferred_element_type=F32)
    return x_ref[...] + acc


def _cross_attention_math(x, g_ref, kv_ref, wq_ref, wo_ref):
    h = _rms(x, g_ref[...]).astype(BF16)
    q = jnp.dot(h, wq_ref[...], preferred_element_type=F32).astype(BF16)
    scores = []
    for hd in range(X_HEADS):
        sl = slice(hd * X_HEAD_DIM, (hd + 1) * X_HEAD_DIM)
        scores.append(lax.dot_general(q[:, sl], kv_ref[:, sl], NT_DIMS, preferred_element_type=F32))
    outs = []
    for hd in range(X_HEADS):
        vh = kv_ref[:, X_WIDTH + hd * X_HEAD_DIM:X_WIDTH + (hd + 1) * X_HEAD_DIM]
        s = scores[hd] * (X_HEAD_DIM ** -0.5)
        p = jnp.exp(s - jnp.max(s, axis=-1, keepdims=True))
        denom = jnp.sum(p, axis=-1, keepdims=True)
        outs.append(jnp.dot(p.astype(BF16), vh, preferred_element_type=F32) / denom)
    oc = jnp.concatenate(outs, axis=-1).astype(BF16)
    return x + jnp.dot(oc, wo_ref[...], preferred_element_type=F32)


def _router_math(x, g_ref, wr_ref, br_ref, h_ref, idx_ref, gate_ref, rank_ref, cnt_ref, base_ref):
    tm = x.shape[0]
    h = _rms(x, g_ref[...])
    hh = h.astype(BF16)
    half = h.shape[1] // 2
    bits = lax.bitcast_convert_type(hh.astype(F32), U32)
    h_ref[...] = (bits[:, :half] >> 16) | (bits[:, half:] & jnp.uint32(0xFFFF0000))
    hl = (h - hh.astype(F32)).astype(BF16)
    w = wr_ref[...]
    wh = w.astype(BF16)
    wl = (w - wh.astype(F32)).astype(BF16)
    nt = functools.partial(lax.dot_general, dimension_numbers=NT_DIMS, preferred_element_type=F32)
    logits = nt(wh, hh) + nt(wh, hl) + nt(wl, hh) + br_ref[...]

    eio = lax.broadcasted_iota(I32, (N_EXPERTS, tm), 0).astype(F32)
    work = logits
    vals, onehots = [], []
    for k in range(TOP_K):
        mx = jnp.max(work, axis=0, keepdims=True)
        ix = jnp.min(jnp.where(work == mx, eio, float(N_EXPERTS)), axis=0, keepdims=True)
        sel = eio == ix
        vals.append(mx)
        onehots.append(sel)
        idx_ref[k:k + 1, :] = ix.astype(I32)
        work = jnp.where(sel, -jnp.inf, work)
    ex = [jnp.exp(v - vals[0]) for v in vals]
    denom = ex[0] + ex[1] + ex[2] + ex[3]
    for k in range(TOP_K):
        gate_ref[k:k + 1, :] = ex[k] / denom

    oh = [jnp.where(s, 1.0, 0.0) for s in onehots]
    oh_all = oh[0] + oh[1] + oh[2] + oh[3]
    r = lax.broadcasted_iota(I32, (tm, tm), 0)
    c = lax.broadcasted_iota(I32, (tm, tm), 1)
    upper = jnp.where(r < c, 1.0, 0.0).astype(BF16)
    before = jnp.dot(oh_all.astype(BF16), upper, preferred_element_type=F32) + base_ref[...]
    for k in range(TOP_K):
        rank_ref[k:k + 1, :] = jnp.sum(oh[k] * before, axis=0, keepdims=True).astype(I32)
    base_ref[...] = base_ref[...] + jnp.sum(oh_all, axis=1, keepdims=True)
    cnt_ref[...] = base_ref[...]


def _mix_tail_kernel(a_ref, hg_ref, x_ref, wa_ref, wh_ref, gx_ref, kv_ref, wq_ref, wo_ref, gr_ref, wr_ref, br_ref,
                     x2_ref, h_ref, idx_ref, gate_ref, rank_ref, cnt_ref, base_ref):
    @pl.when(jnp.logical_and(pl.program_id(0) == 0, pl.program_id(1) == 0))
    def _():
        base_ref[...] = jnp.zeros_like(base_ref)

    x1 = _out_projection_math(a_ref, hg_ref, x_ref, wa_ref, wh_ref)
    x2 = _cross_attention_math(x1, gx_ref, kv_ref, wq_ref, wo_ref)
    x2_ref[...] = x2
    _router_math(x2, gr_ref, wr_ref, br_ref, h_ref, idx_ref, gate_ref, rank_ref, cnt_ref, base_ref)


def _mix_tail(att, hg, x, w_out, g_cross, kv, w_xq, w_xo, g_ffn, w_router_t, b_router, batch, seq, n_mem, tm):
    m, d = x.shape
    nt = seq // tm

    def tile(width):
        return pl.BlockSpec((tm, width), lambda b, i: (b * nt + i, 0))

    def resident(shape, row_block=0):
        return pl.BlockSpec(shape, lambda b, i: (row_block, 0), pipeline_mode=pl.Buffered(1))

    row4 = pl.BlockSpec((TOP_K, tm), lambda b, i: (0, b * nt + i))
    return pl.pallas_call(
        _mix_tail_kernel,
        grid=(batch, nt),
        in_specs=[tile(ATT_WIDTH), tile(HG_WIDTH), tile(d),
                  resident((ATT_WIDTH, d)), resident((HG_WIDTH, d), ATT_WIDTH // HG_WIDTH),
                  resident((1, d)),
                  pl.BlockSpec((n_mem, 2 * X_WIDTH), lambda b, i: (b, 0)),
                  resident((d, X_WIDTH)), resident((X_WIDTH, d)),
                  resident((1, d)), resident((N_EXPERTS, d)), resident((N_EXPERTS, 1))],
        out_specs=[tile(d), tile(d // 2), row4, row4, row4,
                   pl.BlockSpec((N_EXPERTS, 1), lambda b, i: (0, 0))],
        out_shape=[jax.ShapeDtypeStruct((m, d), F32),
                   jax.ShapeDtypeStruct((m, d // 2), U32),
                   jax.ShapeDtypeStruct((TOP_K, m), I32),
                   jax.ShapeDtypeStruct((TOP_K, m), F32),
                   jax.ShapeDtypeStruct((TOP_K, m), I32),
                   jax.ShapeDtypeStruct((N_EXPERTS, 1), F32)],
        scratch_shapes=[pltpu.VMEM((N_EXPERTS, 1), F32)],
        compiler_params=_params("arbitrary", "arbitrary"),
        name="mix_tail",
    )(att, hg, x, w_out, w_out, g_cross.reshape(1, d), kv, w_xq, w_xo, g_ffn.reshape(1, d), w_router_t,
      b_router.reshape(N_EXPERTS, 1))


def _assign_rows_kernel(pstart_ref, idx_ref, rank_ref, dest_ref):
    idx = idx_ref[...]
    start = jnp.zeros_like(idx)
    for e in range(N_EXPERTS):
        start = jnp.where(idx == e, pstart_ref[e], start)
    dest_ref[...] = start + rank_ref[...]


def _assign_rows(pstart, idx, rank):
    full = pl.BlockSpec(idx.shape, lambda: (0, 0))
    return pl.pallas_call(
        _assign_rows_kernel,
        in_specs=[pl.BlockSpec(memory_space=pltpu.SMEM), full, full],
        out_specs=full,
        out_shape=jax.ShapeDtypeStruct(idx.shape, I32),
        name="assign_rows",
    )(pstart, idx, rank)


def _dispatch_kernel(seg_ref, dest_hbm, h_ref, xs_hbm, dsm, zrow, sem_idx, sem_row, *, m_tokens, p_rows):
    i = pl.program_id(0)
    tt = SMEM_STAGE

    def idx_copy(k):
        return pltpu.make_async_copy(dest_hbm.at[pl.ds(k * m_tokens + i * tt, tt)],
                                     dsm.at[pl.ds(k * tt, tt)], sem_idx)

    for k in range(TOP_K):
        idx_copy(k).start()
    for k in range(TOP_K):
        idx_copy(k).wait()

    def row_copy(t, dst_row):
        return pltpu.make_async_copy(h_ref.at[pl.ds(t, 1)], xs_hbm.at[pl.ds(dst_row, 1)], sem_row)

    def issue(t, carry):
        for k in range(TOP_K):
            row_copy(t, dsm[k * tt + t]).start(priority=k % 2)
        return carry

    lax.fori_loop(0, tt, issue, 0, unroll=8)

    for k in range(TOP_K):
        pltpu.make_async_copy(h_ref, xs_hbm.at[pl.ds(0, tt)], sem_row).wait()

    @pl.when(i == 0)
    def _():
        zrow[...] = jnp.zeros_like(zrow)

        def zero_copy(dst_row):
            return pltpu.make_async_copy(zrow.at[pl.ds(0, 1)], xs_hbm.at[pl.ds(dst_row, 1)], sem_row)

        def fill(lo, hi):
            def body(rw, carry):
                zero_copy(rw).start()
                return carry

            lax.fori_loop(lo, hi, body, 0)

            def wbody(rw, carry):
                zero_copy(0).wait()
                return carry

            lax.fori_loop(lo, hi, wbody, 0)

        def per_expert(e, carry):
            fill(seg_ref[0, e] + seg_ref[1, e], seg_ref[2, e])
            return carry

        lax.fori_loop(0, N_EXPERTS, per_expert, 0)
        fill(seg_ref[2, N_EXPERTS - 1], p_rows)


def _dispatch(seg, dest_flat, h, p_rows):
    m, d = h.shape
    return pl.pallas_call(
        functools.partial(_dispatch_kernel, m_tokens=m, p_rows=p_rows),
        grid=(m // SMEM_STAGE,),
        in_specs=[pl.BlockSpec(memory_space=pltpu.SMEM),
                  pl.BlockSpec(memory_space=pl.ANY),
                  pl.BlockSpec((SMEM_STAGE, d), lambda i: (i, 0))],
        out_specs=pl.BlockSpec(memory_space=pl.ANY),
        out_shape=jax.ShapeDtypeStruct((p_rows, d), h.dtype),
        scratch_shapes=[pltpu.SMEM((TOP_K * SMEM_STAGE,), I32),
                        pltpu.VMEM((8, d), h.dtype),
                        pltpu.SemaphoreType.DMA,
                        pltpu.SemaphoreType.DMA],
        compiler_params=_params("arbitrary"),
        name="dispatch",
    )(seg, dest_flat, h)


def _expert_kernel(te_ref, tv_ref, nu_ref, x_ref, w1g_ref, w1l_ref, b1g_ref, b1l_ref, w2_ref, b2_ref,
                   o_ref, xb_ref):
    i = pl.program_id(0)
    f = pl.program_id(1)
    nf = pl.num_programs(1)
    live = tv_ref[i] > 0

    @pl.when(live)
    def _():
        @pl.when(f == 0)
        def _():
            words = x_ref[...]
            half = words.shape[1]
            xb_ref[:, :half] = lax.bitcast_convert_type(words << 16, F32).astype(BF16)
            xb_ref[:, half:] = lax.bitcast_convert_type(words & jnp.uint32(0xFFFF0000), F32).astype(BF16)
            o_ref[...] = jnp.broadcast_to(b2_ref[0], o_ref.shape)

        xb = xb_ref[...]
        glu = jnp.dot(xb, w1g_ref[0], preferred_element_type=F32) + b1g_ref[0]
        lin = jnp.dot(xb, w1l_ref[0], preferred_element_type=F32) + b1l_ref[0]
        glu = jnp.minimum(glu, SWIGLU_LIMIT)
        lin = jnp.clip(lin, -SWIGLU_LIMIT, SWIGLU_LIMIT)
        act = glu * _sigmoid(SWIGLU_ALPHA * glu) * (lin + 1.0)
        o_ref[...] += jnp.dot(act.astype(BF16), w2_ref[0], preferred_element_type=F32)

    @pl.when(jnp.logical_and(jnp.logical_not(live), f == nf - 1))
    def _():
        o_ref[...] = jnp.zeros_like(o_ref)


def _expert_mlp(tile_e, tile_valid, n_used, xs, w1, b1, w2, b2, tmb, tf):
    p_rows = xs.shape[0]
    d = w1.shape[1]
    d_ff = w2.shape[1]
    nf = d_ff // tf
    n_tiles = p_rows // tmb

    def fsel(i, f, tv):
        return jnp.where(tv[i] > 0, f, nf - 1)

    grid_spec = pltpu.PrefetchScalarGridSpec(
        num_scalar_prefetch=3,
        grid=(n_tiles, nf),
        in_specs=[
            pl.BlockSpec((tmb, d // 2), lambda i, f, te, tv, nu: (jnp.minimum(i, nu[0] - 1), 0)),
            pl.BlockSpec((1, d, tf), lambda i, f, te, tv, nu: (te[i], 0, fsel(i, f, tv))),
            pl.BlockSpec((1, d, tf), lambda i, f, te, tv, nu: (te[i], 0, nf + fsel(i, f, tv))),
            pl.BlockSpec((1, 1, tf), lambda i, f, te, tv, nu: (te[i], 0, fsel(i, f, tv))),
            pl.BlockSpec((1, 1, tf), lambda i, f, te, tv, nu: (te[i], 0, nf + fsel(i, f, tv))),
            pl.BlockSpec((1, tf, d), lambda i, f, te, tv, nu: (te[i], fsel(i, f, tv), 0)),
            pl.BlockSpec((1, 1, d), lambda i, f, te, tv, nu: (te[i], 0, 0)),
        ],
        out_specs=pl.BlockSpec((tmb, d), lambda i, f, te, tv, nu: (i, 0)),
        scratch_shapes=[pltpu.VMEM((tmb, d), BF16)],
    )
    return pl.pallas_call(
        _expert_kernel,
        grid_spec=grid_spec,
        out_shape=jax.ShapeDtypeStruct((p_rows, d), F32),
        compiler_params=_params("arbitrary", "arbitrary"),
        name="expert_mlp",
    )(tile_e, tile_valid, n_used, xs, w1, w1, b1, b1, w2, b2)


def _combine_kernel(dest_hbm, ys_hbm, x_ref, gate_ref, g_ref, o_ref, dsm, buf, sem_idx, sem_row, *, m_tokens, tc):
    i = pl.program_id(0)
    n = pl.num_programs(0)
    per_stage = SMEM_STAGE // tc

    def issue_tile(tile):
        slot = tile % 2
        off = (tile % per_stage) * tc

        @pl.when(tile % per_stage == 0)
        def _():
            base = (tile // per_stage) * SMEM_STAGE

            def idx_copy(k):
                return pltpu.make_async_copy(dest_hbm.at[pl.ds(k * m_tokens + base, SMEM_STAGE)],
                                             dsm.at[pl.ds(k * SMEM_STAGE, SMEM_STAGE)], sem_idx)

            for k in range(TOP_K):
                idx_copy(k).start()
            for k in range(TOP_K):
                idx_copy(k).wait()

        def issue(t, carry):
            for k in range(TOP_K):
                pltpu.make_async_copy(ys_hbm.at[pl.ds(dsm[k * SMEM_STAGE + off + t], 1)],
                                      buf.at[slot, k, pl.ds(t, 1)], sem_row.at[slot]).start(priority=k % 2)
            return carry

        lax.fori_loop(0, tc, issue, 0, unroll=8)

    @pl.when(i == 0)
    def _():
        issue_tile(i)

    @pl.when(i + 1 < n)
    def _():
        issue_tile(i + 1)

    slot = i % 2
    for k in range(TOP_K):
        pltpu.make_async_copy(ys_hbm.at[pl.ds(0, tc)], buf.at[slot, k], sem_row.at[slot]).wait()

    gate = gate_ref[...]
    y = x_ref[...]
    for k in range(TOP_K):
        y = y + buf[slot, k] * gate[:, k:k + 1]
    o_ref[...] = _rms(y, g_ref[...])


def _combine(dest_flat, ys, x, gate_t, g, tc):
    m, d = x.shape
    return pl.pallas_call(
        functools.partial(_combine_kernel, m_tokens=m, tc=tc),
        grid=(m // tc,),
        in_specs=[pl.BlockSpec(memory_space=pl.ANY),
                  pl.BlockSpec(memory_space=pl.ANY),
                  pl.BlockSpec((tc, d), lambda i: (i, 0)),
                  pl.BlockSpec((tc, TOP_K), lambda i: (i, 0)),
                  pl.BlockSpec((1, d), lambda i: (0, 0))],
        out_specs=pl.BlockSpec((tc, d), lambda i: (i, 0)),
        out_shape=jax.ShapeDtypeStruct((m, d), F32),
        scratch_shapes=[pltpu.SMEM((TOP_K * SMEM_STAGE,), I32),
                        pltpu.VMEM((2, TOP_K, tc, d), F32),
                        pltpu.SemaphoreType.DMA,
                        pltpu.SemaphoreType.DMA((2,))],
        compiler_params=_params("arbitrary"),
        name="combine",
    )(dest_flat, ys, x, gate_t, g.reshape(1, d))


def _rope_tables(seq):
    half = ATT_HEAD_DIM // 2
    inv = ROPE_THETA ** (-jnp.arange(0, ATT_HEAD_DIM, 2, dtype=F32) / ATT_HEAD_DIM)
    ang = jnp.arange(seq, dtype=F32)[:, None] * inv[None, :]
    cos = jnp.tile(jnp.cos(ang), (1, LANES // half))
    sin = jnp.sin(ang)
    sin_signed = jnp.tile(jnp.concatenate([-sin, sin], axis=-1), (1, LANES // ATT_HEAD_DIM))
    return cos, sin_signed


def _pick(n, prefs):
    for p in prefs:
        if n % p == 0:
            return p
    return n


def _trunk(x, mem, wts, moe_tile, ff_tile):
    batch, seq, d = x.shape
    n_mem = mem.shape[1]
    m = batch * seq
    x2d = x.reshape(m, d)
    tm = _pick(m, (512, 256, 128))

    n_in = wts["w_in"].shape[1]
    z = _norm_matmul(x2d, wts["norm_mix"], wts["w_in"], _pick(m, (1024, 512, 256, 128)),
                     _pick(n_in, (1664, 1024, 512, 256, 128)), F32, "in_projection")

    cos, sin_signed = _rope_tables(seq)
    att = _window_attention(z, wts["att_sink"], cos, sin_signed, batch, seq, _pick(seq, (256, 128)))
    hg = _hgrn2(z, wts["lb"], wts["hg_norm"], batch, seq, _pick(seq, (256, 128, 64)))
    kv = _norm_matmul(mem.reshape(batch * n_mem, d), wts["norm_mem"], wts["w_xkv"], n_mem, 2 * X_WIDTH, BF16,
                      "memory_kv")
    x2, h3, idx, gate, rank, cnt = _mix_tail(att, hg, x2d, wts["w_out"], wts["norm_cross"], kv, wts["w_xq"],
                                             wts["w_xo"], wts["norm_ffn"], wts["w_router_t"], wts["b_router"],
                                             batch, seq, n_mem, _pick(seq, (512, 256, 128)))

    counts = cnt[:, 0].astype(I32)
    padded = (counts + moe_tile - 1) // moe_tile * moe_tile
    pend = jnp.cumsum(padded)
    pstart = pend - padded
    dest = _assign_rows(pstart.astype(I32), idx, rank)
    n_tiles = -(-(m * TOP_K) // moe_tile) + N_EXPERTS
    p_rows = n_tiles * moe_tile
    tile_row = jnp.arange(n_tiles, dtype=I32) * moe_tile
    tile_e = jnp.minimum(jnp.sum((pend[None, :] <= tile_row[:, None]).astype(I32), axis=1), N_EXPERTS - 1)
    owner = tile_e[:, None] == jnp.arange(N_EXPERTS, dtype=I32)[None, :]
    seg_end = jnp.sum(jnp.where(owner, (pstart + counts)[None, :], 0), axis=1)
    tile_valid = jnp.clip(seg_end - tile_row, 0, moe_tile).astype(I32)
    n_used = (pend[-1:] // moe_tile).astype(I32)
    seg = jnp.stack([pstart, counts, pend]).astype(I32)
    dest_flat = dest.reshape(-1)

    xs = _dispatch(seg, dest_flat, h3, p_rows)
    ys = _expert_mlp(tile_e, tile_valid, n_used, xs, wts["w_moe1"], wts["b_moe1"], wts["w_moe2"], wts["b_moe2"],
                     moe_tile, ff_tile)
    y = _combine(dest_flat, ys, x2, gate.T, wts["norm_final"], _pick(m, (256, 128)))
    return y.reshape(batch, seq, d)


def kernel(x_prompt, x_sample, mem_prompt, mem_sample, norm_mix, w_in, att_sink, hg_lb_logits, hg_norm, w_out,
           norm_cross, norm_mem, w_xq, w_xkv, w_xo, norm_ffn, w_router, b_router, w_moe1, b_moe1, w_moe2, b_moe2,
           norm_final):
    assert w_in.shape[0] == 1, "the final norm is fused after the single layer"
    lb_all = jnp.cumsum(jax.nn.softmax(hg_lb_logits.astype(F32), axis=0), axis=0)
    d_ff = w_moe2.shape[2]
    wts = dict(
        norm_mix=norm_mix[0], w_in=w_in[0].astype(BF16), att_sink=att_sink[0], lb=lb_all[0],
        hg_norm=hg_norm[0], w_out=w_out[0].astype(BF16), norm_cross=norm_cross[0], norm_mem=norm_mem[0],
        w_xq=w_xq[0].astype(BF16), w_xkv=w_xkv[0].astype(BF16), w_xo=w_xo[0].astype(BF16),
        norm_ffn=norm_ffn[0], w_router_t=w_router[0].T, b_router=b_router[0],
        w_moe1=w_moe1[0].astype(BF16), b_moe1=b_moe1[0][:, None, :],
        w_moe2=w_moe2[0].astype(BF16), b_moe2=b_moe2[0][:, None, :],
        norm_final=norm_final,
    )
    ff_tile = _pick(d_ff, (1024, 512, 256, 128))
    return (_trunk(x_prompt, mem_prompt, wts, MOE_TILE, ff_tile),
            _trunk(x_sample, mem_sample, wts, MOE_TILE, ff_tile))
```

```python
import functools

import jax
import jax.numpy as jnp
from jax import lax
from jax.experimental import pallas as pl
from jax.experimental.pallas import tpu as pltpu

F32 = jnp.float32
BF16 = jnp.bfloat16
I32 = jnp.int32
U32 = jnp.uint32

EPS = 1e-5
ATT_HEADS = 16
ATT_KV_HEADS = 4
ATT_HEAD_DIM = 64
ATT_WIDTH = ATT_HEADS * ATT_HEAD_DIM
KV_WIDTH = ATT_KV_HEADS * ATT_HEAD_DIM
WINDOW = 128
ROPE_THETA = 10000.0
HG_HEADS = 8
HG_DIM = 128
HG_WIDTH = HG_HEADS * HG_DIM
HG_CHUNK = 64
X_HEADS = 4
X_HEAD_DIM = 128
X_WIDTH = X_HEADS * X_HEAD_DIM
N_EXPERTS = 32
TOP_K = 4
SWIGLU_ALPHA = 1.702
SWIGLU_LIMIT = 7.0

COL_Q_ATT = 0
COL_K_ATT = ATT_WIDTH
COL_V_ATT = ATT_WIDTH + KV_WIDTH
COL_Q_HG = ATT_WIDTH + 2 * KV_WIDTH
COL_F_FWD = COL_Q_HG + HG_WIDTH
COL_F_BWD = COL_F_FWD + HG_WIDTH
COL_I_HG = COL_F_BWD + HG_WIDTH
COL_G_HG = COL_I_HG + HG_WIDTH

LANES = 128
VMEM_LIMIT = 56 * 1024 * 1024
SMEM_STAGE = 1024
MOE_TILE = 512

NT_DIMS = (((1,), (1,)), ((), ()))
TN_DIMS = (((0,), (0,)), ((), ()))


def _params(*sem):
    return pltpu.CompilerParams(dimension_semantics=sem, vmem_limit_bytes=VMEM_LIMIT)


def _rms(x, g):
    ms = jnp.mean(x * x, axis=-1, keepdims=True)
    return x * lax.rsqrt(ms + EPS) * g


def _sigmoid(x):
    return 1.0 / (1.0 + jnp.exp(-x))


def _norm_matmul_kernel(x_ref, g_ref, w_ref, o_ref, xn_ref):
    @pl.when(pl.program_id(1) == 0)
    def _():
        xn_ref[...] = _rms(x_ref[...], g_ref[...]).astype(BF16)

    o_ref[...] = jnp.dot(xn_ref[...], w_ref[...], preferred_element_type=F32).astype(o_ref.dtype)


def _norm_matmul(x, g, w, tm, tn, out_dtype, name):
    m, d = x.shape
    n = w.shape[1]
    return pl.pallas_call(
        _norm_matmul_kernel,
        grid=(m // tm, n // tn),
        in_specs=[pl.BlockSpec((tm, d), lambda i, j: (i, 0)),
                  pl.BlockSpec((1, d), lambda i, j: (0, 0)),
                  pl.BlockSpec((d, tn), lambda i, j: (0, j))],
        out_specs=pl.BlockSpec((tm, tn), lambda i, j: (i, j)),
        out_shape=jax.ShapeDtypeStruct((m, n), out_dtype),
        scratch_shapes=[pltpu.VMEM((tm, d), BF16)],
        compiler_params=_params("arbitrary", "arbitrary"),
        name=name,
    )(x, g.reshape(1, d), w)


def _rope_pair(x, cos, sin_signed):
    lane = lax.broadcasted_iota(I32, (1, LANES), 1)
    first_half = (lane % ATT_HEAD_DIM) < (ATT_HEAD_DIM // 2)
    rot = jnp.where(first_half, pltpu.roll(x, LANES - ATT_HEAD_DIM // 2, 1), pltpu.roll(x, ATT_HEAD_DIM // 2, 1))
    return x * cos + rot * sin_signed


def _attn_kernel(sink_ref, q_ref, k_ref, v_ref, cq_ref, sq_ref, ck_ref, sk_ref, o_ref, klo_ref, khi_ref, vt_ref,
                 *, seq, tq):
    qi = pl.program_id(1)
    win = tq + 2 * WINDOW
    half_lanes = lax.broadcasted_iota(I32, (1, LANES), 1) < ATT_HEAD_DIM
    heads_per_slab = LANES // ATT_HEAD_DIM

    @pl.when(qi == 0)
    def _():
        for j in range(KV_WIDTH // LANES):
            slab = _rope_pair(k_ref[:, j * LANES:(j + 1) * LANES], ck_ref[...], sk_ref[...])
            swapped = pltpu.roll(slab, ATT_HEAD_DIM, 1)
            g0 = j * heads_per_slab
            g1 = g0 + 1
            klo_ref[:, g0 * LANES:(g0 + 1) * LANES] = jnp.where(half_lanes, slab, 0.0).astype(BF16)
            khi_ref[:, g0 * LANES:(g0 + 1) * LANES] = jnp.where(half_lanes, 0.0, swapped).astype(BF16)
            klo_ref[:, g1 * LANES:(g1 + 1) * LANES] = jnp.where(half_lanes, swapped, 0.0).astype(BF16)
            khi_ref[:, g1 * LANES:(g1 + 1) * LANES] = jnp.where(half_lanes, 0.0, slab).astype(BF16)
            vt_ref[j * LANES:(j + 1) * LANES, :] = v_ref[:, j * LANES:(j + 1) * LANES].T.astype(BF16)

    q0 = qi * tq
    ks = pl.multiple_of(jnp.clip(q0 - WINDOW, 0, seq - win), WINDOW)
    kpos = ks + lax.broadcasted_iota(I32, (win, 1), 0)
    qpos = q0 + lax.broadcasted_iota(I32, (1, tq), 1)
    valid = jnp.abs(kpos - qpos) <= WINDOW
    cq = cq_ref[...]
    sq = sq_ref[...]
    group = ATT_HEADS // ATT_KV_HEADS
    scores = []
    for j in range(ATT_WIDTH // LANES):
        qs = (_rope_pair(q_ref[:, j * LANES:(j + 1) * LANES], cq, sq) * (ATT_HEAD_DIM ** -0.5)).astype(BF16)
        for hh in range(heads_per_slab):
            g = (j * heads_per_slab + hh) // group
            k_ref_sel = klo_ref if hh == 0 else khi_ref
            ksel = k_ref_sel[pl.ds(ks, win), g * LANES:(g + 1) * LANES]
            scores.append(lax.dot_general(ksel, qs, NT_DIMS, preferred_element_type=F32))
    for j in range(ATT_WIDTH // LANES):
        outs = []
        for hh in range(heads_per_slab):
            h = j * heads_per_slab + hh
            g = h // group
            st = jnp.where(valid, scores[h], -1e30)
            sk = sink_ref[h]
            m = jnp.maximum(jnp.max(st, axis=0, keepdims=True), sk)
            p = jnp.exp(st - m)
            denom = jnp.sum(p, axis=0, keepdims=True) + jnp.exp(sk - m)
            vt = vt_ref[g * ATT_HEAD_DIM:(g + 1) * ATT_HEAD_DIM, pl.ds(ks, win)]
            ot = jnp.dot(vt, p.astype(BF16), preferred_element_type=F32)
            outs.append(ot / denom)
        o_ref[:, j * LANES:(j + 1) * LANES] = jnp.concatenate(outs, axis=0).T.astype(o_ref.dtype)


def _window_attention(z, sink, cos, sin_signed, batch, seq, tq):
    m = z.shape[0]
    nq = seq // tq
    qblk = COL_Q_ATT // ATT_WIDTH
    kblk = COL_K_ATT // KV_WIDTH
    vblk = COL_V_ATT // KV_WIDTH
    return pl.pallas_call(
        functools.partial(_attn_kernel, seq=seq, tq=tq),
        grid=(batch, nq),
        in_specs=[pl.BlockSpec(memory_space=pltpu.SMEM),
                  pl.BlockSpec((tq, ATT_WIDTH), lambda b, i: (b * nq + i, qblk)),
                  pl.BlockSpec((seq, KV_WIDTH), lambda b, i: (b, kblk)),
                  pl.BlockSpec((seq, KV_WIDTH), lambda b, i: (b, vblk)),
                  pl.BlockSpec((tq, LANES), lambda b, i: (i, 0)),
                  pl.BlockSpec((tq, LANES), lambda b, i: (i, 0)),
                  pl.BlockSpec((seq, LANES), lambda b, i: (0, 0)),
                  pl.BlockSpec((seq, LANES), lambda b, i: (0, 0))],
        out_specs=pl.BlockSpec((tq, ATT_WIDTH), lambda b, i: (b * nq + i, 0)),
        out_shape=jax.ShapeDtypeStruct((m, ATT_WIDTH), BF16),
        scratch_shapes=[pltpu.VMEM((seq, ATT_KV_HEADS * LANES), BF16),
                        pltpu.VMEM((seq, ATT_KV_HEADS * LANES), BF16),
                        pltpu.VMEM((KV_WIDTH, seq), BF16)],
        compiler_params=_params("arbitrary", "arbitrary"),
        name="window_attention",
    )(sink, z, z, z, cos, sin_signed, cos, sin_signed)


def _split_cumsum(tri, x):
    hi = x.astype(BF16)
    r1 = x - hi.astype(F32)
    mid = r1.astype(BF16)
    lo = (r1 - mid.astype(F32)).astype(BF16)
    dot = functools.partial(jnp.dot, preferred_element_type=F32)
    return dot(tri, hi) + dot(tri, mid) + dot(tri, lo)


def _run_interleaved(stage_generators):
    results = [None] * len(stage_generators)
    live = list(range(len(stage_generators)))
    while live:
        for i in list(live):
            try:
                next(stage_generators[i])
            except StopIteration as done:
                results[i] = done.value
                live.remove(i)
    return results


def _hgrn_group(zq, zf, v, lbv, state, mask, tri, forward):
    rows = zq.shape[0]
    c = HG_CHUNK
    nchunk = rows // c
    q = zq * _sigmoid(zq) * (HG_DIM ** -0.5)
    f = lbv + (1.0 - lbv) * _sigmoid(zf)
    logf = jnp.log(f)
    k = 1.0 - f
    yield
    b = _split_cumsum(tri, logf)
    yield
    ref_row = c // 2 - 1 if forward else c // 2
    last_row = c - 1 if forward else 0
    qe, ke, kd, qb, dec = [], [], [], [], []
    for n in range(nchunk):
        sl = slice(n * c, (n + 1) * c)
        bn, qn, kn = b[sl], q[sl], k[sl]
        bref = bn[ref_row:ref_row + 1]
        blast = bn[last_row:last_row + 1]
        qe.append(qn * jnp.exp(bn - bref))
        ke.append(kn * jnp.exp(bref - bn))
        kd.append((kn * jnp.exp(blast - bn)).astype(BF16))
        qb.append((qn * jnp.exp(bn)).astype(BF16))
        dec.append(jnp.exp(blast))
    qe = jnp.concatenate(qe, axis=0).astype(BF16)
    ke = jnp.concatenate(ke, axis=0).astype(BF16)
    vb = v.astype(BF16)
    a = lax.dot_general(qe, ke, NT_DIMS, preferred_element_type=F32)

    zero_blk = jnp.zeros((c, HG_DIM), BF16)

    def block_diag(blocks):
        return jnp.concatenate(
            [jnp.concatenate([blk if j == n else zero_blk for j in range(nchunk)], axis=1)
             for n, blk in enumerate(blocks)], axis=0)

    u_all = lax.dot_general(vb, block_diag(kd), TN_DIMS, preferred_element_type=F32)
    yield
    a = jnp.where(mask, a, 0.0).astype(BF16)
    o = jnp.dot(a, vb, preferred_element_type=F32)
    entering = [None] * nchunk
    order = range(nchunk) if forward else range(nchunk - 1, -1, -1)
    for n in order:
        entering[n] = state.astype(BF16)
        state = state * dec[n] + u_all[:, n * HG_DIM:(n + 1) * HG_DIM]
    o = o + lax.dot_general(block_diag(qb), jnp.concatenate(entering, axis=1), NT_DIMS,
                            preferred_element_type=F32)
    return o, state


def _hgrn_kernel(zq_ref, zff_ref, zfb_ref, zi_ref, zg_ref, lb_ref, gn_ref, o_ref, accf_ref, accb_ref, *, seq, rows):
    ngroups = seq // rows
    r = lax.broadcasted_iota(I32, (rows, rows), 0)
    cidx = lax.broadcasted_iota(I32, (rows, rows), 1)
    same = (r // HG_CHUNK) == (cidx // HG_CHUNK)
    mask_f = jnp.logical_and(same, cidx <= r)
    mask_b = jnp.logical_and(same, cidx >= r)
    tri_f = jnp.where(mask_f, 1.0, 0.0).astype(BF16)
    tri_b = jnp.where(mask_b, 1.0, 0.0).astype(BF16)
    lb_f = lb_ref[0:1, :]
    lb_b = lb_ref[1:2, :]
    zero_state = jnp.zeros((HG_DIM, HG_DIM), F32)

    def scan_body(it, states):
        sf, sb = states
        slf = pl.ds(pl.multiple_of(it * rows, rows), rows)
        slb = pl.ds(pl.multiple_of((ngroups - 1 - it) * rows, rows), rows)
        (of, sf), (ob, sb) = _run_interleaved([
            _hgrn_group(zq_ref[slf, :], zff_ref[slf, :], zi_ref[slf, :], lb_f, sf, mask_f, tri_f, True),
            _hgrn_group(zq_ref[slb, :], zfb_ref[slb, :], zi_ref[slb, :], lb_b, sb, mask_b, tri_b, False)])
        accf_ref[slf, :] = of
        accb_ref[slb, :] = ob
        return sf, sb

    lax.fori_loop(0, ngroups, scan_body, (zero_state, zero_state))

    def readout_body(gi, carry):
        sl = pl.ds(pl.multiple_of(gi * rows, rows), rows)
        tot = accf_ref[sl, :] + accb_ref[sl, :]
        zg = zg_ref[sl, :]
        y = _rms(tot, gn_ref[...]) * (zg * _sigmoid(zg))
        o_ref[sl, :] = y.astype(o_ref.dtype)
        return carry

    lax.fori_loop(0, ngroups, readout_body, 0)


def _hgrn2(z, lb, hg_norm, batch, seq, rows):
    m = z.shape[0]

    def zspec(col):
        base = col // HG_DIM
        return pl.BlockSpec((seq, HG_DIM), lambda b, h: (b, base + h))

    return pl.pallas_call(
        functools.partial(_hgrn_kernel, seq=seq, rows=rows),
        grid=(batch, HG_HEADS),
        in_specs=[zspec(COL_Q_HG), zspec(COL_F_FWD), zspec(COL_F_BWD), zspec(COL_I_HG), zspec(COL_G_HG),
                  pl.BlockSpec((2, HG_DIM), lambda b, h: (0, h)),
                  pl.BlockSpec((1, HG_DIM), lambda b, h: (0, 0))],
        out_specs=pl.BlockSpec((seq, HG_DIM), lambda b, h: (b, h)),
        out_shape=jax.ShapeDtypeStruct((m, HG_WIDTH), BF16),
        scratch_shapes=[pltpu.VMEM((seq, HG_DIM), F32), pltpu.VMEM((seq, HG_DIM), F32)],
        compiler_params=_params("arbitrary", "arbitrary"),
        name="hgrn2",
    )(z, z, z, z, z, lb, hg_norm.reshape(1, HG_DIM))


def _out_projection_math(a_ref, hg_ref, x_ref, wa_ref, wh_ref):
    acc = jnp.dot(a_ref[...], wa_ref[...], preferred_element_type=F32)
    acc = acc + jnp.dot(hg_ref[...], wh_ref[...], preferred_element_type=F32)
    return x_ref[...] + acc


def _cross_attention_math(x, g_ref, kv_ref, wq_ref, wo_ref):
    h = _rms(x, g_ref[...]).astype(BF16)
    q = jnp.dot(h, wq_ref[...], preferred_element_type=F32).astype(BF16)
    scores = []
    for hd in range(X_HEADS):
        sl = slice(hd * X_HEAD_DIM, (hd + 1) * X_HEAD_DIM)
        scores.append(lax.dot_general(q[:, sl], kv_ref[:, sl], NT_DIMS, preferred_element_type=F32))
    outs = []
    for hd in range(X_HEADS):
        vh = kv_ref[:, X_WIDTH + hd * X_HEAD_DIM:X_WIDTH + (hd + 1) * X_HEAD_DIM]
        s = scores[hd] * (X_HEAD_DIM ** -0.5)
        p = jnp.exp(s - jnp.max(s, axis=-1, keepdims=True))
        denom = jnp.sum(p, axis=-1, keepdims=True)
        outs.append(jnp.dot(p.astype(BF16), vh, preferred_element_type=F32) / denom)
    oc = jnp.concatenate(outs, axis=-1).astype(BF16)
    return x + jnp.dot(oc, wo_ref[...], preferred_element_type=F32)


def _router_math(x, g_ref, wr_ref, br_ref, h_ref, idx_ref, gate_ref, rank_ref, cnt_ref, base_ref):
    tm = x.shape[0]
    h = _rms(x, g_ref[...])
    hh = h.astype(BF16)
    half = h.shape[1] // 2
    slabs = half // LANES
    bits = lax.bitcast_convert_type(hh.astype(F32), U32)
    packed = (bits[:, :half] >> 16) | (bits[:, half:] & jnp.uint32(0xFFFF0000))
    for j in range(slabs):
        h_ref[pl.ds(j, tm, stride=slabs), :] = packed[:, j * LANES:(j + 1) * LANES]
    hl = (h - hh.astype(F32)).astype(BF16)
    w = wr_ref[...]
    wh = w.astype(BF16)
    wl = (w - wh.astype(F32)).astype(BF16)
    nt = functools.partial(lax.dot_general, dimension_numbers=NT_DIMS, preferred_element_type=F32)
    logits = nt(wh, hh) + nt(wh, hl) + nt(wl, hh) + br_ref[...]

    eio = lax.broadcasted_iota(I32, (N_EXPERTS, tm), 0).astype(F32)
    work = logits
    vals, onehots = [], []
    for k in range(TOP_K):
        mx = jnp.max(work, axis=0, keepdims=True)
        ix = jnp.min(jnp.where(work == mx, eio, float(N_EXPERTS)), axis=0, keepdims=True)
        sel = eio == ix
        vals.append(mx)
        onehots.append(sel)
        idx_ref[k:k + 1, :] = ix.astype(I32)
        work = jnp.where(sel, -jnp.inf, work)
    ex = [jnp.exp(v - vals[0]) for v in vals]
    denom = ex[0] + ex[1] + ex[2] + ex[3]
    for k in range(TOP_K):
        gate_ref[k:k + 1, :] = ex[k] / denom

    oh = [jnp.where(s, 1.0, 0.0) for s in onehots]
    oh_all = oh[0] + oh[1] + oh[2] + oh[3]
    r = lax.broadcasted_iota(I32, (tm, tm), 0)
    c = lax.broadcasted_iota(I32, (tm, tm), 1)
    upper = jnp.where(r < c, 1.0, 0.0).astype(BF16)
    before = jnp.dot(oh_all.astype(BF16), upper, preferred_element_type=F32) + base_ref[...]
    for k in range(TOP_K):
        rank_ref[k:k + 1, :] = jnp.sum(oh[k] * before, axis=0, keepdims=True).astype(I32)
    base_ref[...] = base_ref[...] + jnp.sum(oh_all, axis=1, keepdims=True)
    cnt_ref[...] = base_ref[...]


def _mix_tail_kernel(a_ref, hg_ref, x_ref, wa_ref, wh_ref, gx_ref, kv_ref, wq_ref, wo_ref, gr_ref, wr_ref, br_ref,
                     x2_ref, h_ref, idx_ref, gate_ref, rank_ref, cnt_ref, base_ref):
    @pl.when(jnp.logical_and(pl.program_id(0) == 0, pl.program_id(1) == 0))
    def _():
        base_ref[...] = jnp.zeros_like(base_ref)

    x1 = _out_projection_math(a_ref, hg_ref, x_ref, wa_ref, wh_ref)
    x2 = _cross_attention_math(x1, gx_ref, kv_ref, wq_ref, wo_ref)
    x2_ref[...] = x2
    _router_math(x2, gr_ref, wr_ref, br_ref, h_ref, idx_ref, gate_ref, rank_ref, cnt_ref, base_ref)


def _mix_tail(att, hg, x, w_out, g_cross, kv, w_xq, w_xo, g_ffn, w_router_t, b_router, batch, seq, n_mem, tm):
    m, d = x.shape
    nt = seq // tm

    def tile(width):
        return pl.BlockSpec((tm, width), lambda b, i: (b * nt + i, 0))

    def resident(shape, row_block=0):
        return pl.BlockSpec(shape, lambda b, i: (row_block, 0), pipeline_mode=pl.Buffered(1))

    row4 = pl.BlockSpec((TOP_K, tm), lambda b, i: (0, b * nt + i))
    slabs = d // 2 // LANES
    return pl.pallas_call(
        _mix_tail_kernel,
        grid=(batch, nt),
        in_specs=[tile(ATT_WIDTH), tile(HG_WIDTH), tile(d),
                  resident((ATT_WIDTH, d)), resident((HG_WIDTH, d), ATT_WIDTH // HG_WIDTH),
                  resident((1, d)),
                  pl.BlockSpec((n_mem, 2 * X_WIDTH), lambda b, i: (b, 0)),
                  resident((d, X_WIDTH)), resident((X_WIDTH, d)),
                  resident((1, d)), resident((N_EXPERTS, d)), resident((N_EXPERTS, 1))],
        out_specs=[tile(d), pl.BlockSpec((tm * slabs, LANES), lambda b, i: (b * nt + i, 0)), row4, row4, row4,
                   pl.BlockSpec((N_EXPERTS, 1), lambda b, i: (0, 0))],
        out_shape=[jax.ShapeDtypeStruct((m, d), F32),
                   jax.ShapeDtypeStruct((m * slabs, LANES), U32),
                   jax.ShapeDtypeStruct((TOP_K, m), I32),
                   jax.ShapeDtypeStruct((TOP_K, m), F32),
                   jax.ShapeDtypeStruct((TOP_K, m), I32),
                   jax.ShapeDtypeStruct((N_EXPERTS, 1), F32)],
        scratch_shapes=[pltpu.VMEM((N_EXPERTS, 1), F32)],
        compiler_params=_params("arbitrary", "arbitrary"),
        name="mix_tail",
    )(att, hg, x, w_out, w_out, g_cross.reshape(1, d), kv, w_xq, w_xo, g_ffn.reshape(1, d), w_router_t,
      b_router.reshape(N_EXPERTS, 1))


def _assign_rows_kernel(pstart_ref, idx_ref, rank_ref, dest_ref):
    idx = idx_ref[...]
    start = jnp.zeros_like(idx)
    for e in range(N_EXPERTS):
        start = jnp.where(idx == e, pstart_ref[e], start)
    dest_ref[...] = start + rank_ref[...]


def _assign_rows(pstart, idx, rank):
    full = pl.BlockSpec(idx.shape, lambda: (0, 0))
    return pl.pallas_call(
        _assign_rows_kernel,
        in_specs=[pl.BlockSpec(memory_space=pltpu.SMEM), full, full],
        out_specs=full,
        out_shape=jax.ShapeDtypeStruct(idx.shape, I32),
        name="assign_rows",
    )(pstart, idx, rank)


def _dispatch_kernel(seg_ref, dest_hbm, h_ref, xs_hbm, dsm, zrow, sem_idx, sem_row, *, m_tokens, p_rows):
    i = pl.program_id(0)
    tt = SMEM_STAGE
    slabs = h_ref.shape[0] // tt

    def token_rows(ref, tok):
        return ref.at[pl.ds(pl.multiple_of(tok * slabs, slabs), slabs)]

    def idx_copy(k):
        return pltpu.make_async_copy(dest_hbm.at[pl.ds(k * m_tokens + i * tt, tt)],
                                     dsm.at[pl.ds(k * tt, tt)], sem_idx)

    for k in range(TOP_K):
        idx_copy(k).start()
    for k in range(TOP_K):
        idx_copy(k).wait()

    def row_copy(t, dst_row):
        return pltpu.make_async_copy(token_rows(h_ref, t), token_rows(xs_hbm, dst_row), sem_row)

    def issue(t, carry):
        for k in range(TOP_K):
            row_copy(t, dsm[k * tt + t]).start(priority=k % 2)
        return carry

    lax.fori_loop(0, tt, issue, 0, unroll=8)

    for k in range(TOP_K):
        pltpu.make_async_copy(h_ref, xs_hbm.at[pl.ds(0, tt * slabs)], sem_row).wait()

    @pl.when(i == 0)
    def _():
        zrow[...] = jnp.zeros_like(zrow)

        def zero_copy(dst_row):
            return pltpu.make_async_copy(zrow, token_rows(xs_hbm, dst_row), sem_row)

        def fill(lo, hi):
            def body(rw, carry):
                zero_copy(rw).start()
                return carry

            lax.fori_loop(lo, hi, body, 0)

            def wbody(rw, carry):
                zero_copy(0).wait()
                return carry

            lax.fori_loop(lo, hi, wbody, 0)

        def per_expert(e, carry):
            fill(seg_ref[0, e] + seg_ref[1, e], seg_ref[2, e])
            return carry

        lax.fori_loop(0, N_EXPERTS, per_expert, 0)
        fill(seg_ref[2, N_EXPERTS - 1], p_rows)


def _dispatch(seg, dest_flat, h, m, p_rows):
    slabs = h.shape[0] // m
    return pl.pallas_call(
        functools.partial(_dispatch_kernel, m_tokens=m, p_rows=p_rows),
        grid=(m // SMEM_STAGE,),
        in_specs=[pl.BlockSpec(memory_space=pltpu.SMEM),
                  pl.BlockSpec(memory_space=pl.ANY),
                  pl.BlockSpec((SMEM_STAGE * slabs, LANES), lambda i: (i, 0))],
        out_specs=pl.BlockSpec(memory_space=pl.ANY),
        out_shape=jax.ShapeDtypeStruct((p_rows * slabs, LANES), h.dtype),
        scratch_shapes=[pltpu.SMEM((TOP_K * SMEM_STAGE,), I32),
                        pltpu.VMEM((slabs, LANES), h.dtype),
                        pltpu.SemaphoreType.DMA,
                        pltpu.SemaphoreType.DMA],
        compiler_params=_params("arbitrary"),
        name="dispatch",
    )(seg, dest_flat, h)


def _expert_kernel(te_ref, tv_ref, nu_ref, x_ref, w1g_ref, w1l_ref, b1g_ref, b1l_ref, w2_ref, b2_ref,
                   o_ref, xb_ref):
    i = pl.program_id(0)
    f = pl.program_id(1)
    nf = pl.num_programs(1)
    live = tv_ref[i] > 0

    @pl.when(live)
    def _():
        @pl.when(f == 0)
        def _():
            tmb, d = xb_ref.shape
            half = d // 2
            slabs = half // LANES
            for j in range(slabs):
                words = x_ref[pl.ds(j, tmb, stride=slabs), :]
                lo = slice(j * LANES, (j + 1) * LANES)
                hi = slice(half + j * LANES, half + (j + 1) * LANES)
                xb_ref[:, lo] = lax.bitcast_convert_type(words << 16, F32).astype(BF16)
                xb_ref[:, hi] = lax.bitcast_convert_type(words & jnp.uint32(0xFFFF0000), F32).astype(BF16)
            o_ref[...] = jnp.broadcast_to(b2_ref[0], o_ref.shape)

        xb = xb_ref[...]
        glu = jnp.dot(xb, w1g_ref[0], preferred_element_type=F32) + b1g_ref[0]
        lin = jnp.dot(xb, w1l_ref[0], preferred_element_type=F32) + b1l_ref[0]
        glu = jnp.minimum(glu, SWIGLU_LIMIT)
        lin = jnp.clip(lin, -SWIGLU_LIMIT, SWIGLU_LIMIT)
        act = glu * _sigmoid(SWIGLU_ALPHA * glu) * (lin + 1.0)
        o_ref[...] += jnp.dot(act.astype(BF16), w2_ref[0], preferred_element_type=F32)

    @pl.when(jnp.logical_and(jnp.logical_not(live), f == nf - 1))
    def _():
        o_ref[...] = jnp.zeros_like(o_ref)


def _expert_mlp(tile_e, tile_valid, n_used, xs, w1, b1, w2, b2, tmb, tf):
    d = w1.shape[1]
    p_rows = xs.shape[0] // (d // 2 // LANES)
    d_ff = w2.shape[1]
    nf = d_ff // tf
    n_tiles = p_rows // tmb

    def fsel(i, f, tv):
        return jnp.where(tv[i] > 0, f, nf - 1)

    grid_spec = pltpu.PrefetchScalarGridSpec(
        num_scalar_prefetch=3,
        grid=(n_tiles, nf),
        in_specs=[
            pl.BlockSpec((tmb * (d // 2 // LANES), LANES), lambda i, f, te, tv, nu: (jnp.minimum(i, nu[0] - 1), 0)),
            pl.BlockSpec((1, d, tf), lambda i, f, te, tv, nu: (te[i], 0, fsel(i, f, tv))),
            pl.BlockSpec((1, d, tf), lambda i, f, te, tv, nu: (te[i], 0, nf + fsel(i, f, tv))),
            pl.BlockSpec((1, 1, tf), lambda i, f, te, tv, nu: (te[i], 0, fsel(i, f, tv))),
            pl.BlockSpec((1, 1, tf), lambda i, f, te, tv, nu: (te[i], 0, nf + fsel(i, f, tv))),
            pl.BlockSpec((1, tf, d), lambda i, f, te, tv, nu: (te[i], fsel(i, f, tv), 0)),
            pl.BlockSpec((1, 1, d), lambda i, f, te, tv, nu: (te[i], 0, 0)),
        ],
        out_specs=pl.BlockSpec((tmb, d), lambda i, f, te, tv, nu: (i, 0)),
        scratch_shapes=[pltpu.VMEM((tmb, d), BF16)],
    )
    return pl.pallas_call(
        _expert_kernel,
        grid_spec=grid_spec,
        out_shape=jax.ShapeDtypeStruct((p_rows, d), F32),
        compiler_params=_params("arbitrary", "arbitrary"),
        name="expert_mlp",
    )(tile_e, tile_valid, n_used, xs, w1, w1, b1, b1, w2, b2)


def _combine_kernel(dest_hbm, ys_hbm, x_ref, gate_ref, g_ref, o_ref, dsm, buf, sem_idx, sem_row, *, m_tokens, tc):
    i = pl.program_id(0)
    n = pl.num_programs(0)
    per_stage = SMEM_STAGE // tc

    def issue_tile(tile):
        slot = tile % 2
        off = (tile % per_stage) * tc

        @pl.when(tile % per_stage == 0)
        def _():
            base = (tile // per_stage) * SMEM_STAGE

            def idx_copy(k):
                return pltpu.make_async_copy(dest_hbm.at[pl.ds(k * m_tokens + base, SMEM_STAGE)],
                                             dsm.at[pl.ds(k * SMEM_STAGE, SMEM_STAGE)], sem_idx)

            for k in range(TOP_K):
                idx_copy(k).start()
            for k in range(TOP_K):
                idx_copy(k).wait()

        def issue(t, carry):
            for k in range(TOP_K):
                pltpu.make_async_copy(ys_hbm.at[pl.ds(dsm[k * SMEM_STAGE + off + t], 1)],
                                      buf.at[slot, k, pl.ds(t, 1)], sem_row.at[slot]).start(priority=k % 2)
            return carry

        lax.fori_loop(0, tc, issue, 0, unroll=8)

    @pl.when(i == 0)
    def _():
        issue_tile(i)

    @pl.when(i + 1 < n)
    def _():
        issue_tile(i + 1)

    slot = i % 2
    for k in range(TOP_K):
        pltpu.make_async_copy(ys_hbm.at[pl.ds(0, tc)], buf.at[slot, k], sem_row.at[slot]).wait()

    gate = gate_ref[...]
    y = x_ref[...]
    for k in range(TOP_K):
        y = y + buf[slot, k] * gate[:, k:k + 1]
    o_ref[...] = _rms(y, g_ref[...])


def _combine(dest_flat, ys, x, gate_t, g, tc):
    m, d = x.shape
    return pl.pallas_call(
        functools.partial(_combine_kernel, m_tokens=m, tc=tc),
        grid=(m // tc,),
        in_specs=[pl.BlockSpec(memory_space=pl.ANY),
                  pl.BlockSpec(memory_space=pl.ANY),
                  pl.BlockSpec((tc, d), lambda i: (i, 0)),
                  pl.BlockSpec((tc, TOP_K), lambda i: (i, 0)),
                  pl.BlockSpec((1, d), lambda i: (0, 0))],
        out_specs=pl.BlockSpec((tc, d), lambda i: (i, 0)),
        out_shape=jax.ShapeDtypeStruct((m, d), F32),
        scratch_shapes=[pltpu.SMEM((TOP_K * SMEM_STAGE,), I32),
                        pltpu.VMEM((2, TOP_K, tc, d), F32),
                        pltpu.SemaphoreType.DMA,
                        pltpu.SemaphoreType.DMA((2,))],
        compiler_params=_params("arbitrary"),
        name="combine",
    )(dest_flat, ys, x, gate_t, g.reshape(1, d))


def _rope_tables(seq):
    half = ATT_HEAD_DIM // 2
    inv = ROPE_THETA ** (-jnp.arange(0, ATT_HEAD_DIM, 2, dtype=F32) / ATT_HEAD_DIM)
    ang = jnp.arange(seq, dtype=F32)[:, None] * inv[None, :]
    cos = jnp.tile(jnp.cos(ang), (1, LANES // half))
    sin = jnp.sin(ang)
    sin_signed = jnp.tile(jnp.concatenate([-sin, sin], axis=-1), (1, LANES // ATT_HEAD_DIM))
    return cos, sin_signed


def _pick(n, prefs):
    for p in prefs:
        if n % p == 0:
            return p
    return n


def _trunk(x, mem, wts, moe_tile, ff_tile):
    batch, seq, d = x.shape
    n_mem = mem.shape[1]
    m = batch * seq
    x2d = x.reshape(m, d)
    tm = _pick(m, (512, 256, 128))

    n_in = wts["w_in"].shape[1]
    z = _norm_matmul(x2d, wts["norm_mix"], wts["w_in"], _pick(m, (1024, 512, 256, 128)),
                     _pick(n_in, (1664, 1024, 512, 256, 128)), F32, "in_projection")

    cos, sin_signed = _rope_tables(seq)
    att = _window_attention(z, wts["att_sink"], cos, sin_signed, batch, seq, _pick(seq, (256, 128)))
    hg = _hgrn2(z, wts["lb"], wts["hg_norm"], batch, seq, _pick(seq, (256, 128, 64)))
    kv = _norm_matmul(mem.reshape(batch * n_mem, d), wts["norm_mem"], wts["w_xkv"], n_mem, 2 * X_WIDTH, BF16,
                      "memory_kv")
    x2, h3, idx, gate, rank, cnt = _mix_tail(att, hg, x2d, wts["w_out"], wts["norm_cross"], kv, wts["w_xq"],
                                             wts["w_xo"], wts["norm_ffn"], wts["w_router_t"], wts["b_router"],
                                             batch, seq, n_mem, _pick(seq, (512, 256, 128)))

    counts = cnt[:, 0].astype(I32)
    padded = (counts + moe_tile - 1) // moe_tile * moe_tile
    pend = jnp.cumsum(padded)
    pstart = pend - padded
    dest = _assign_rows(pstart.astype(I32), idx, rank)
    n_tiles = -(-(m * TOP_K) // moe_tile) + N_EXPERTS
    p_rows = n_tiles * moe_tile
    tile_row = jnp.arange(n_tiles, dtype=I32) * moe_tile
    tile_e = jnp.minimum(jnp.sum((pend[None, :] <= tile_row[:, None]).astype(I32), axis=1), N_EXPERTS - 1)
    owner = tile_e[:, None] == jnp.arange(N_EXPERTS, dtype=I32)[None, :]
    seg_end = jnp.sum(jnp.where(owner, (pstart + counts)[None, :], 0), axis=1)
    tile_valid = jnp.clip(seg_end - tile_row, 0, moe_tile).astype(I32)
    n_used = (pend[-1:] // moe_tile).astype(I32)
    seg = jnp.stack([pstart, counts, pend]).astype(I32)
    dest_flat = dest.reshape(-1)

    xs = _dispatch(seg, dest_flat, h3, m, p_rows)
    ys = _expert_mlp(tile_e, tile_valid, n_used, xs, wts["w_moe1"], wts["b_moe1"], wts["w_moe2"], wts["b_moe2"],
                     moe_tile, ff_tile)
    y = _combine(dest_flat, ys, x2, gate.T, wts["norm_final"], _pick(m, (256, 128)))
    return y.reshape(batch, seq, d)


def kernel(x_prompt, x_sample, mem_prompt, mem_sample, norm_mix, w_in, att_sink, hg_lb_logits, hg_norm, w_out,
           norm_cross, norm_mem, w_xq, w_xkv, w_xo, norm_ffn, w_router, b_router, w_moe1, b_moe1, w_moe2, b_moe2,
           norm_final):
    assert w_in.shape[0] == 1, "the final norm is fused after the single layer"
    lb_all = jnp.cumsum(jax.nn.softmax(hg_lb_logits.astype(F32), axis=0), axis=0)
    d_ff = w_moe2.shape[2]
    wts = dict(
        norm_mix=norm_mix[0], w_in=w_in[0].astype(BF16), att_sink=att_sink[0], lb=lb_all[0],
        hg_norm=hg_norm[0], w_out=w_out[0].astype(BF16), norm_cross=norm_cross[0], norm_mem=norm_mem[0],
        w_xq=w_xq[0].astype(BF16), w_xkv=w_xkv[0].astype(BF16), w_xo=w_xo[0].astype(BF16),
        norm_ffn=norm_ffn[0], w_router_t=w_router[0].T, b_router=b_router[0],
        w_moe1=w_moe1[0].astype(BF16), b_moe1=b_moe1[0][:, None, :],
        w_moe2=w_moe2[0].astype(BF16), b_moe2=b_moe2[0][:, None, :],
        norm_final=norm_final,
    )
    ff_tile = _pick(d_ff, (1024, 512, 256, 128))
    return (_trunk(x_prompt, mem_prompt, wts, MOE_TILE, ff_tile),
            _trunk(x_sample, mem_sample, wts, MOE_TILE, ff_tile))
```

```python
import functools

import jax
import jax.numpy as jnp
from jax import lax
from jax.experimental import pallas as pl
from jax.experimental.pallas import tpu as pltpu

F32 = jnp.float32
BF16 = jnp.bfloat16
I32 = jnp.int32
U32 = jnp.uint32

EPS = 1e-5
ATT_HEADS = 16
ATT_KV_HEADS = 4
ATT_HEAD_DIM = 64
ATT_WIDTH = ATT_HEADS * ATT_HEAD_DIM
KV_WIDTH = ATT_KV_HEADS * ATT_HEAD_DIM
WINDOW = 128
ROPE_THETA = 10000.0
HG_HEADS = 8
HG_DIM = 128
HG_WIDTH = HG_HEADS * HG_DIM
HG_CHUNK = 64
X_HEADS = 4
X_HEAD_DIM = 128
X_WIDTH = X_HEADS * X_HEAD_DIM
N_EXPERTS = 32
TOP_K = 4
SWIGLU_ALPHA = 1.702
SWIGLU_LIMIT = 7.0

COL_Q_ATT = 0
COL_K_ATT = ATT_WIDTH
COL_V_ATT = ATT_WIDTH + KV_WIDTH
COL_Q_HG = ATT_WIDTH + 2 * KV_WIDTH
COL_F_FWD = COL_Q_HG + HG_WIDTH
COL_F_BWD = COL_F_FWD + HG_WIDTH
COL_I_HG = COL_F_BWD + HG_WIDTH
COL_G_HG = COL_I_HG + HG_WIDTH

LANES = 128
VMEM_LIMIT = 56 * 1024 * 1024
SMEM_STAGE = 1024
MOE_TILE = 512

NT_DIMS = (((1,), (1,)), ((), ()))
TN_DIMS = (((0,), (0,)), ((), ()))


def _params(*sem):
    return pltpu.CompilerParams(dimension_semantics=sem, vmem_limit_bytes=VMEM_LIMIT)


def _rms(x, g):
    ms = jnp.mean(x * x, axis=-1, keepdims=True)
    return x * lax.rsqrt(ms + EPS) * g


def _sigmoid(x):
    return 1.0 / (1.0 + jnp.exp(-x))


def _norm_matmul_kernel(x_ref, g_ref, w_ref, o_ref, xn_ref):
    @pl.when(pl.program_id(1) == 0)
    def _():
        xn_ref[...] = _rms(x_ref[...], g_ref[...]).astype(BF16)

    o_ref[...] = jnp.dot(xn_ref[...], w_ref[...], preferred_element_type=F32).astype(o_ref.dtype)


def _norm_matmul(x, g, w, tm, tn, out_dtype, name):
    m, d = x.shape
    n = w.shape[1]
    return pl.pallas_call(
        _norm_matmul_kernel,
        grid=(m // tm, n // tn),
        in_specs=[pl.BlockSpec((tm, d), lambda i, j: (i, 0)),
                  pl.BlockSpec((1, d), lambda i, j: (0, 0)),
                  pl.BlockSpec((d, tn), lambda i, j: (0, j))],
        out_specs=pl.BlockSpec((tm, tn), lambda i, j: (i, j)),
        out_shape=jax.ShapeDtypeStruct((m, n), out_dtype),
        scratch_shapes=[pltpu.VMEM((tm, d), BF16)],
        compiler_params=_params("arbitrary", "arbitrary"),
        name=name,
    )(x, g.reshape(1, d), w)


def _rope_pair(x, cos, sin_signed):
    lane = lax.broadcasted_iota(I32, (1, LANES), 1)
    first_half = (lane % ATT_HEAD_DIM) < (ATT_HEAD_DIM // 2)
    rot = jnp.where(first_half, pltpu.roll(x, LANES - ATT_HEAD_DIM // 2, 1), pltpu.roll(x, ATT_HEAD_DIM // 2, 1))
    return x * cos + rot * sin_signed


def _attn_kernel(sink_ref, q_ref, k_ref, v_ref, cq_ref, sq_ref, ck_ref, sk_ref, o_ref, klo_ref, khi_ref, vt_ref,
                 *, seq, tq):
    qi = pl.program_id(1)
    win = tq + 2 * WINDOW
    half_lanes = lax.broadcasted_iota(I32, (1, LANES), 1) < ATT_HEAD_DIM
    heads_per_slab = LANES // ATT_HEAD_DIM

    @pl.when(qi == 0)
    def _():
        for j in range(KV_WIDTH // LANES):
            slab = _rope_pair(k_ref[:, j * LANES:(j + 1) * LANES], ck_ref[...], sk_ref[...])
            swapped = pltpu.roll(slab, ATT_HEAD_DIM, 1)
            g0 = j * heads_per_slab
            g1 = g0 + 1
            klo_ref[:, g0 * LANES:(g0 + 1) * LANES] = jnp.where(half_lanes, slab, 0.0).astype(BF16)
            khi_ref[:, g0 * LANES:(g0 + 1) * LANES] = jnp.where(half_lanes, 0.0, swapped).astype(BF16)
            klo_ref[:, g1 * LANES:(g1 + 1) * LANES] = jnp.where(half_lanes, swapped, 0.0).astype(BF16)
            khi_ref[:, g1 * LANES:(g1 + 1) * LANES] = jnp.where(half_lanes, 0.0, slab).astype(BF16)
            vt_ref[j * LANES:(j + 1) * LANES, :] = v_ref[:, j * LANES:(j + 1) * LANES].T.astype(BF16)

    q0 = qi * tq
    ks = pl.multiple_of(jnp.clip(q0 - WINDOW, 0, seq - win), WINDOW)
    kpos = ks + lax.broadcasted_iota(I32, (win, 1), 0)
    qpos = q0 + lax.broadcasted_iota(I32, (1, tq), 1)
    valid = jnp.abs(kpos - qpos) <= WINDOW
    cq = cq_ref[...]
    sq = sq_ref[...]
    group = ATT_HEADS // ATT_KV_HEADS
    scores = []
    for j in range(ATT_WIDTH // LANES):
        qs = (_rope_pair(q_ref[:, j * LANES:(j + 1) * LANES], cq, sq) * (ATT_HEAD_DIM ** -0.5)).astype(BF16)
        for hh in range(heads_per_slab):
            g = (j * heads_per_slab + hh) // group
            k_ref_sel = klo_ref if hh == 0 else khi_ref
            ksel = k_ref_sel[pl.ds(ks, win), g * LANES:(g + 1) * LANES]
            scores.append(lax.dot_general(ksel, qs, NT_DIMS, preferred_element_type=F32))
    for j in range(ATT_WIDTH // LANES):
        outs = []
        for hh in range(heads_per_slab):
            h = j * heads_per_slab + hh
            g = h // group
            st = jnp.where(valid, scores[h], -1e30)
            sk = sink_ref[h]
            m = jnp.maximum(jnp.max(st, axis=0, keepdims=True), sk)
            p = jnp.exp(st - m)
            denom = jnp.sum(p, axis=0, keepdims=True) + jnp.exp(sk - m)
            vt = vt_ref[g * ATT_HEAD_DIM:(g + 1) * ATT_HEAD_DIM, pl.ds(ks, win)]
            ot = jnp.dot(vt, p.astype(BF16), preferred_element_type=F32)
            outs.append(ot / denom)
        o_ref[:, j * LANES:(j + 1) * LANES] = jnp.concatenate(outs, axis=0).T.astype(o_ref.dtype)


def _window_attention(z, sink, cos, sin_signed, batch, seq, tq):
    m = z.shape[0]
    nq = seq // tq
    qblk = COL_Q_ATT // ATT_WIDTH
    kblk = COL_K_ATT // KV_WIDTH
    vblk = COL_V_ATT // KV_WIDTH
    return pl.pallas_call(
        functools.partial(_attn_kernel, seq=seq, tq=tq),
        grid=(batch, nq),
        in_specs=[pl.BlockSpec(memory_space=pltpu.SMEM),
                  pl.BlockSpec((tq, ATT_WIDTH), lambda b, i: (b * nq + i, qblk)),
                  pl.BlockSpec((seq, KV_WIDTH), lambda b, i: (b, kblk)),
                  pl.BlockSpec((seq, KV_WIDTH), lambda b, i: (b, vblk)),
                  pl.BlockSpec((tq, LANES), lambda b, i: (i, 0)),
                  pl.BlockSpec((tq, LANES), lambda b, i: (i, 0)),
                  pl.BlockSpec((seq, LANES), lambda b, i: (0, 0)),
                  pl.BlockSpec((seq, LANES), lambda b, i: (0, 0))],
        out_specs=pl.BlockSpec((tq, ATT_WIDTH), lambda b, i: (b * nq + i, 0)),
        out_shape=jax.ShapeDtypeStruct((m, ATT_WIDTH), BF16),
        scratch_shapes=[pltpu.VMEM((seq, ATT_KV_HEADS * LANES), BF16),
                        pltpu.VMEM((seq, ATT_KV_HEADS * LANES), BF16),
                        pltpu.VMEM((KV_WIDTH, seq), BF16)],
        compiler_params=_params("arbitrary", "arbitrary"),
        name="window_attention",
    )(sink, z, z, z, cos, sin_signed, cos, sin_signed)


def _split_cumsum(tri, x):
    hi = x.astype(BF16)
    r1 = x - hi.astype(F32)
    mid = r1.astype(BF16)
    lo = (r1 - mid.astype(F32)).astype(BF16)
    dot = functools.partial(jnp.dot, preferred_element_type=F32)
    return dot(tri, hi) + dot(tri, mid) + dot(tri, lo)


def _run_interleaved(stage_generators):
    results = [None] * len(stage_generators)
    live = list(range(len(stage_generators)))
    while live:
        for i in list(live):
            try:
                next(stage_generators[i])
            except StopIteration as done:
                results[i] = done.value
                live.remove(i)
    return results


def _hgrn_group(zq, zf, v, lbv, state, mask, tri, forward):
    rows = zq.shape[0]
    c = HG_CHUNK
    nchunk = rows // c
    q = zq * _sigmoid(zq) * (HG_DIM ** -0.5)
    f = lbv + (1.0 - lbv) * _sigmoid(zf)
    logf = jnp.log(f)
    k = 1.0 - f
    yield
    b = _split_cumsum(tri, logf)
    yield
    ref_row = c // 2 - 1 if forward else c // 2
    last_row = c - 1 if forward else 0
    qe, ke, kd, qb, dec = [], [], [], [], []
    for n in range(nchunk):
        sl = slice(n * c, (n + 1) * c)
        bn, qn, kn = b[sl], q[sl], k[sl]
        bref = bn[ref_row:ref_row + 1]
        blast = bn[last_row:last_row + 1]
        qe.append(qn * jnp.exp(bn - bref))
        ke.append(kn * jnp.exp(bref - bn))
        kd.append((kn * jnp.exp(blast - bn)).astype(BF16))
        qb.append((qn * jnp.exp(bn)).astype(BF16))
        dec.append(jnp.exp(blast))
    qe = jnp.concatenate(qe, axis=0).astype(BF16)
    ke = jnp.concatenate(ke, axis=0).astype(BF16)
    vb = v.astype(BF16)
    a = lax.dot_general(qe, ke, NT_DIMS, preferred_element_type=F32)

    zero_blk = jnp.zeros((c, HG_DIM), BF16)

    def block_diag(blocks):
        return jnp.concatenate(
            [jnp.concatenate([blk if j == n else zero_blk for j in range(nchunk)], axis=1)
             for n, blk in enumerate(blocks)], axis=0)

    u_all = lax.dot_general(vb, block_diag(kd), TN_DIMS, preferred_element_type=F32)
    yield
    a = jnp.where(mask, a, 0.0).astype(BF16)
    o = jnp.dot(a, vb, preferred_element_type=F32)
    entering = [None] * nchunk
    order = range(nchunk) if forward else range(nchunk - 1, -1, -1)
    for n in order:
        entering[n] = state.astype(BF16)
        state = state * dec[n] + u_all[:, n * HG_DIM:(n + 1) * HG_DIM]
    o = o + lax.dot_general(block_diag(qb), jnp.concatenate(entering, axis=1), NT_DIMS,
                            preferred_element_type=F32)
    return o, state


def _hgrn_kernel(zq_ref, zff_ref, zfb_ref, zi_ref, zg_ref, lb_ref, gn_ref, o_ref, accf_ref, accb_ref, *, seq, rows):
    ngroups = seq // rows
    r = lax.broadcasted_iota(I32, (rows, rows), 0)
    cidx = lax.broadcasted_iota(I32, (rows, rows), 1)
    same = (r // HG_CHUNK) == (cidx // HG_CHUNK)
    mask_f = jnp.logical_and(same, cidx <= r)
    mask_b = jnp.logical_and(same, cidx >= r)
    tri_f = jnp.where(mask_f, 1.0, 0.0).astype(BF16)
    tri_b = jnp.where(mask_b, 1.0, 0.0).astype(BF16)
    lb_f = lb_ref[0:1, :]
    lb_b = lb_ref[1:2, :]
    zero_state = jnp.zeros((HG_DIM, HG_DIM), F32)

    def scan_body(it, states):
        sf, sb = states
        slf = pl.ds(pl.multiple_of(it * rows, rows), rows)
        slb = pl.ds(pl.multiple_of((ngroups - 1 - it) * rows, rows), rows)
        (of, sf), (ob, sb) = _run_interleaved([
            _hgrn_group(zq_ref[slf, :], zff_ref[slf, :], zi_ref[slf, :], lb_f, sf, mask_f, tri_f, True),
            _hgrn_group(zq_ref[slb, :], zfb_ref[slb, :], zi_ref[slb, :], lb_b, sb, mask_b, tri_b, False)])
        accf_ref[slf, :] = of
        accb_ref[slb, :] = ob
        return sf, sb

    lax.fori_loop(0, ngroups, scan_body, (zero_state, zero_state))

    def readout_body(gi, carry):
        sl = pl.ds(pl.multiple_of(gi * rows, rows), rows)
        tot = accf_ref[sl, :] + accb_ref[sl, :]
        zg = zg_ref[sl, :]
        y = _rms(tot, gn_ref[...]) * (zg * _sigmoid(zg))
        o_ref[sl, :] = y.astype(o_ref.dtype)
        return carry

    lax.fori_loop(0, ngroups, readout_body, 0)


def _hgrn2(z, lb, hg_norm, batch, seq, rows):
    m = z.shape[0]

    def zspec(col):
        base = col // HG_DIM
        return pl.BlockSpec((seq, HG_DIM), lambda b, h: (b, base + h))

    return pl.pallas_call(
        functools.partial(_hgrn_kernel, seq=seq, rows=rows),
        grid=(batch, HG_HEADS),
        in_specs=[zspec(COL_Q_HG), zspec(COL_F_FWD), zspec(COL_F_BWD), zspec(COL_I_HG), zspec(COL_G_HG),
                  pl.BlockSpec((2, HG_DIM), lambda b, h: (0, h)),
                  pl.BlockSpec((1, HG_DIM), lambda b, h: (0, 0))],
        out_specs=pl.BlockSpec((seq, HG_DIM), lambda b, h: (b, h)),
        out_shape=jax.ShapeDtypeStruct((m, HG_WIDTH), BF16),
        scratch_shapes=[pltpu.VMEM((seq, HG_DIM), F32), pltpu.VMEM((seq, HG_DIM), F32)],
        compiler_params=_params("arbitrary", "arbitrary"),
        name="hgrn2",
    )(z, z, z, z, z, lb, hg_norm.reshape(1, HG_DIM))


def _out_projection_math(a_ref, hg_ref, x_ref, wa_ref, wh_ref):
    acc = jnp.dot(a_ref[...], wa_ref[...], preferred_element_type=F32)
    acc = acc + jnp.dot(hg_ref[...], wh_ref[...], preferred_element_type=F32)
    return x_ref[...] + acc


def _cross_attention_math(x, g_ref, kv_ref, wq_ref, wo_ref):
    h = _rms(x, g_ref[...]).astype(BF16)
    q = jnp.dot(h, wq_ref[...], preferred_element_type=F32).astype(BF16)
    scores = []
    for hd in range(X_HEADS):
        sl = slice(hd * X_HEAD_DIM, (hd + 1) * X_HEAD_DIM)
        scores.append(lax.dot_general(q[:, sl], kv_ref[:, sl], NT_DIMS, preferred_element_type=F32))
    outs = []
    for hd in range(X_HEADS):
        vh = kv_ref[:, X_WIDTH + hd * X_HEAD_DIM:X_WIDTH + (hd + 1) * X_HEAD_DIM]
        s = scores[hd] * (X_HEAD_DIM ** -0.5)
        p = jnp.exp(s - jnp.max(s, axis=-1, keepdims=True))
        denom = jnp.sum(p, axis=-1, keepdims=True)
        outs.append(jnp.dot(p.astype(BF16), vh, preferred_element_type=F32) / denom)
    oc = jnp.concatenate(outs, axis=-1).astype(BF16)
    return x + jnp.dot(oc, wo_ref[...], preferred_element_type=F32)


def _router_math(x, g_ref, wr_ref, br_ref, h_ref, idx_ref, gate_ref, rank_ref, cnt_ref, base_ref):
    tm = x.shape[0]
    h = _rms(x, g_ref[...])
    hh = h.astype(BF16)
    half = h.shape[1] // 2
    slabs = half // LANES
    bits = lax.bitcast_convert_type(hh.astype(F32), U32)
    packed = (bits[:, :half] >> 16) | (bits[:, half:] & jnp.uint32(0xFFFF0000))
    for j in range(slabs):
        h_ref[pl.ds(j, tm, stride=slabs), :] = packed[:, j * LANES:(j + 1) * LANES]
    hl = (h - hh.astype(F32)).astype(BF16)
    w = wr_ref[...]
    wh = w.astype(BF16)
    wl = (w - wh.astype(F32)).astype(BF16)
    nt = functools.partial(lax.dot_general, dimension_numbers=NT_DIMS, preferred_element_type=F32)
    logits = nt(wh, hh) + nt(wh, hl) + nt(wl, hh) + br_ref[...]

    eio = lax.broadcasted_iota(I32, (N_EXPERTS, tm), 0).astype(F32)
    work = logits
    vals, onehots = [], []
    for k in range(TOP_K):
        mx = jnp.max(work, axis=0, keepdims=True)
        ix = jnp.min(jnp.where(work == mx, eio, float(N_EXPERTS)), axis=0, keepdims=True)
        sel = eio == ix
        vals.append(mx)
        onehots.append(sel)
        idx_ref[k:k + 1, :] = ix.astype(I32)
        work = jnp.where(sel, -jnp.inf, work)
    ex = [jnp.exp(v - vals[0]) for v in vals]
    denom = ex[0] + ex[1] + ex[2] + ex[3]
    for k in range(TOP_K):
        gate_ref[k:k + 1, :] = ex[k] / denom

    oh = [jnp.where(s, 1.0, 0.0) for s in onehots]
    oh_all = oh[0] + oh[1] + oh[2] + oh[3]
    r = lax.broadcasted_iota(I32, (tm, tm), 0)
    c = lax.broadcasted_iota(I32, (tm, tm), 1)
    upper = jnp.where(r < c, 1.0, 0.0).astype(BF16)
    before = jnp.dot(oh_all.astype(BF16), upper, preferred_element_type=F32) + base_ref[...]
    for k in range(TOP_K):
        rank_ref[k:k + 1, :] = jnp.sum(oh[k] * before, axis=0, keepdims=True).astype(I32)
    base_ref[...] = base_ref[...] + jnp.sum(oh_all, axis=1, keepdims=True)
    cnt_ref[...] = base_ref[...]


def _mix_tail_kernel(a_ref, hg_ref, x_ref, wa_ref, wh_ref, gx_ref, kv_ref, wq_ref, wo_ref, gr_ref, wr_ref, br_ref,
                     cnt0_ref, x2_ref, h_ref, idx_ref, gate_ref, rank_ref, cnt_ref, base_ref):
    @pl.when(jnp.logical_and(pl.program_id(0) == 0, pl.program_id(1) == 0))
    def _():
        base_ref[...] = cnt0_ref[...]

    x1 = _out_projection_math(a_ref, hg_ref, x_ref, wa_ref, wh_ref)
    x2 = _cross_attention_math(x1, gx_ref, kv_ref, wq_ref, wo_ref)
    x2_ref[...] = x2
    _router_math(x2, gr_ref, wr_ref, br_ref, h_ref, idx_ref, gate_ref, rank_ref, cnt_ref, base_ref)


def _mix_tail(att, hg, x, w_out, g_cross, kv, w_xq, w_xo, g_ffn, w_router_t, b_router, cnt0, batch, seq, n_mem, tm):
    m, d = x.shape
    nt = seq // tm

    def tile(width):
        return pl.BlockSpec((tm, width), lambda b, i: (b * nt + i, 0))

    def resident(shape, row_block=0):
        return pl.BlockSpec(shape, lambda b, i: (row_block, 0), pipeline_mode=pl.Buffered(1))

    row4 = pl.BlockSpec((TOP_K, tm), lambda b, i: (0, b * nt + i))
    slabs = d // 2 // LANES
    return pl.pallas_call(
        _mix_tail_kernel,
        grid=(batch, nt),
        in_specs=[tile(ATT_WIDTH), tile(HG_WIDTH), tile(d),
                  resident((ATT_WIDTH, d)), resident((HG_WIDTH, d), ATT_WIDTH // HG_WIDTH),
                  resident((1, d)),
                  pl.BlockSpec((n_mem, 2 * X_WIDTH), lambda b, i: (b, 0)),
                  resident((d, X_WIDTH)), resident((X_WIDTH, d)),
                  resident((1, d)), resident((N_EXPERTS, d)), resident((N_EXPERTS, 1)),
                  resident((N_EXPERTS, 1))],
        out_specs=[tile(d), pl.BlockSpec((tm * slabs, LANES), lambda b, i: (b * nt + i, 0)), row4, row4, row4,
                   pl.BlockSpec((N_EXPERTS, 1), lambda b, i: (0, 0))],
        out_shape=[jax.ShapeDtypeStruct((m, d), F32),
                   jax.ShapeDtypeStruct((m * slabs, LANES), U32),
                   jax.ShapeDtypeStruct((TOP_K, m), I32),
                   jax.ShapeDtypeStruct((TOP_K, m), F32),
                   jax.ShapeDtypeStruct((TOP_K, m), I32),
                   jax.ShapeDtypeStruct((N_EXPERTS, 1), F32)],
        scratch_shapes=[pltpu.VMEM((N_EXPERTS, 1), F32)],
        compiler_params=_params("arbitrary", "arbitrary"),
        name="mix_tail",
    )(att, hg, x, w_out, w_out, g_cross.reshape(1, d), kv, w_xq, w_xo, g_ffn.reshape(1, d), w_router_t,
      b_router.reshape(N_EXPERTS, 1), cnt0)


def _assign_rows_kernel(pstart_ref, idx_ref, rank_ref, dest_ref):
    idx = idx_ref[...]
    start = jnp.zeros_like(idx)
    for e in range(N_EXPERTS):
        start = jnp.where(idx == e, pstart_ref[e], start)
    dest_ref[...] = start + rank_ref[...]


def _assign_rows(pstart, idx, rank):
    full = pl.BlockSpec(idx.shape, lambda: (0, 0))
    return pl.pallas_call(
        _assign_rows_kernel,
        in_specs=[pl.BlockSpec(memory_space=pltpu.SMEM), full, full],
        out_specs=full,
        out_shape=jax.ShapeDtypeStruct(idx.shape, I32),
        name="assign_rows",
    )(pstart, idx, rank)


def _dispatch_kernel(seg_ref, dest_hbm, h_ref, xs_prev_hbm, xs_hbm, dsm, zrow, sem_idx, sem_row,
                     *, m_tokens, p_rows, fill_padding):
    del xs_prev_hbm
    i = pl.program_id(0)
    tt = SMEM_STAGE
    slabs = h_ref.shape[0] // tt

    def token_rows(ref, tok):
        return ref.at[pl.ds(pl.multiple_of(tok * slabs, slabs), slabs)]

    def idx_copy(k):
        return pltpu.make_async_copy(dest_hbm.at[pl.ds(k * m_tokens + i * tt, tt)],
                                     dsm.at[pl.ds(k * tt, tt)], sem_idx)

    for k in range(TOP_K):
        idx_copy(k).start()
    for k in range(TOP_K):
        idx_copy(k).wait()

    def row_copy(t, dst_row):
        return pltpu.make_async_copy(token_rows(h_ref, t), token_rows(xs_hbm, dst_row), sem_row)

    def issue(t, carry):
        for k in range(TOP_K):
            row_copy(t, dsm[k * tt + t]).start(priority=k % 2)
        return carry

    lax.fori_loop(0, tt, issue, 0, unroll=8)

    for k in range(TOP_K):
        pltpu.make_async_copy(h_ref, xs_hbm.at[pl.ds(0, tt * slabs)], sem_row).wait()

    @pl.when(jnp.logical_and(i == 0, jnp.bool_(fill_padding)))
    def _():
        zrow[...] = jnp.zeros_like(zrow)

        def zero_copy(dst_row):
            return pltpu.make_async_copy(zrow, token_rows(xs_hbm, dst_row), sem_row)

        def fill(lo, hi):
            def body(rw, carry):
                zero_copy(rw).start()
                return carry

            lax.fori_loop(lo, hi, body, 0)

            def wbody(rw, carry):
                zero_copy(0).wait()
                return carry

            lax.fori_loop(lo, hi, wbody, 0)

        def per_expert(e, carry):
            fill(seg_ref[0, e] + seg_ref[1, e], seg_ref[2, e])
            return carry

        lax.fori_loop(0, N_EXPERTS, per_expert, 0)
        fill(seg_ref[2, N_EXPERTS - 1], p_rows)


def _dispatch(seg, dest_flat, h, xs_prev, m, p_rows, fill_padding):
    slabs = h.shape[0] // m
    return pl.pallas_call(
        functools.partial(_dispatch_kernel, m_tokens=m, p_rows=p_rows, fill_padding=fill_padding),
        grid=(m // SMEM_STAGE,),
        in_specs=[pl.BlockSpec(memory_space=pltpu.SMEM),
                  pl.BlockSpec(memory_space=pl.ANY),
                  pl.BlockSpec((SMEM_STAGE * slabs, LANES), lambda i: (i, 0)),
                  pl.BlockSpec(memory_space=pl.ANY)],
        out_specs=pl.BlockSpec(memory_space=pl.ANY),
        out_shape=jax.ShapeDtypeStruct((p_rows * slabs, LANES), h.dtype),
        input_output_aliases={3: 0},
        scratch_shapes=[pltpu.SMEM((TOP_K * SMEM_STAGE,), I32),
                        pltpu.VMEM((slabs, LANES), h.dtype),
                        pltpu.SemaphoreType.DMA,
                        pltpu.SemaphoreType.DMA],
        compiler_params=_params("arbitrary"),
        name="dispatch",
    )(seg, dest_flat, h, xs_prev)


def _expert_kernel(te_ref, tv_ref, nu_ref, x_ref, w1g_ref, w1l_ref, b1g_ref, b1l_ref, w2_ref, b2_ref,
                   o_ref, xb_ref):
    i = pl.program_id(0)
    f = pl.program_id(1)
    nf = pl.num_programs(1)
    live = tv_ref[i] > 0

    @pl.when(live)
    def _():
        @pl.when(f == 0)
        def _():
            o_ref[...] = jnp.broadcast_to(b2_ref[0], o_ref.shape)

        tmb, d = xb_ref.shape
        half = d // 2
        slabs = half // LANES
        for j in range(slabs):
            words = x_ref[pl.ds(j, tmb, stride=slabs), :]
            lo = slice(j * LANES, (j + 1) * LANES)
            hi = slice(half + j * LANES, half + (j + 1) * LANES)
            xb_ref[:, lo] = lax.bitcast_convert_type(words << 16, F32).astype(BF16)
            xb_ref[:, hi] = lax.bitcast_convert_type(words & jnp.uint32(0xFFFF0000), F32).astype(BF16)

        xb = xb_ref[...]
        glu = jnp.dot(xb, w1g_ref[0], preferred_element_type=F32) + b1g_ref[0]
        lin = jnp.dot(xb, w1l_ref[0], preferred_element_type=F32) + b1l_ref[0]
        glu = jnp.minimum(glu, SWIGLU_LIMIT)
        lin = jnp.clip(lin, -SWIGLU_LIMIT, SWIGLU_LIMIT)
        act = glu * _sigmoid(SWIGLU_ALPHA * glu) * (lin + 1.0)
        o_ref[...] += jnp.dot(act.astype(BF16), w2_ref[0], preferred_element_type=F32)

    @pl.when(jnp.logical_and(jnp.logical_not(live), f == nf - 1))
    def _():
        o_ref[...] = jnp.zeros_like(o_ref)


def _expert_mlp(tile_e, tile_valid, n_used, xs, w1, b1, w2, b2, tmb, tf):
    d = w1.shape[1]
    p_rows = xs.shape[0] // (d // 2 // LANES)
    d_ff = w2.shape[1]
    nf = d_ff // tf
    n_tiles = p_rows // tmb

    def fsel(i, f, tv):
        return jnp.where(tv[i] > 0, f, nf - 1)

    grid_spec = pltpu.PrefetchScalarGridSpec(
        num_scalar_prefetch=3,
        grid=(n_tiles, nf),
        in_specs=[
            pl.BlockSpec((tmb * (d // 2 // LANES), LANES), lambda i, f, te, tv, nu: (jnp.minimum(i, nu[0] - 1), 0)),
            pl.BlockSpec((1, d, tf), lambda i, f, te, tv, nu: (te[i], 0, fsel(i, f, tv))),
            pl.BlockSpec((1, d, tf), lambda i, f, te, tv, nu: (te[i], 0, nf + fsel(i, f, tv))),
            pl.BlockSpec((1, 1, tf), lambda i, f, te, tv, nu: (te[i], 0, fsel(i, f, tv))),
            pl.BlockSpec((1, 1, tf), lambda i, f, te, tv, nu: (te[i], 0, nf + fsel(i, f, tv))),
            pl.BlockSpec((1, tf, d), lambda i, f, te, tv, nu: (te[i], fsel(i, f, tv), 0)),
            pl.BlockSpec((1, 1, d), lambda i, f, te, tv, nu: (te[i], 0, 0)),
        ],
        out_specs=pl.BlockSpec((tmb, d), lambda i, f, te, tv, nu: (i, 0)),
        scratch_shapes=[pltpu.VMEM((tmb, d), BF16)],
    )
    return pl.pallas_call(
        _expert_kernel,
        grid_spec=grid_spec,
        out_shape=jax.ShapeDtypeStruct((p_rows, d), F32),
        compiler_params=_params("arbitrary", "arbitrary"),
        name="expert_mlp",
    )(tile_e, tile_valid, n_used, xs, w1, w1, b1, b1, w2, b2)


def _combine_kernel(dest_hbm, ys_hbm, x_ref, gate_ref, g_ref, o_ref, dsm, buf, sem_idx, sem_row, *, m_tokens, tc):
    i = pl.program_id(0)
    n = pl.num_programs(0)
    per_stage = SMEM_STAGE // tc

    def issue_tile(tile):
        slot = tile % 2
        off = (tile % per_stage) * tc

        @pl.when(tile % per_stage == 0)
        def _():
            base = (tile // per_stage) * SMEM_STAGE

            def idx_copy(k):
                return pltpu.make_async_copy(dest_hbm.at[pl.ds(k * m_tokens + base, SMEM_STAGE)],
                                             dsm.at[pl.ds(k * SMEM_STAGE, SMEM_STAGE)], sem_idx)

            for k in range(TOP_K):
                idx_copy(k).start()
            for k in range(TOP_K):
                idx_copy(k).wait()

        def issue(t, carry):
            for k in range(TOP_K):
                pltpu.make_async_copy(ys_hbm.at[pl.ds(dsm[k * SMEM_STAGE + off + t], 1)],
                                      buf.at[slot, k, pl.ds(t, 1)], sem_row.at[slot]).start(priority=k % 2)
            return carry

        lax.fori_loop(0, tc, issue, 0, unroll=8)

    @pl.when(i == 0)
    def _():
        issue_tile(i)

    @pl.when(i + 1 < n)
    def _():
        issue_tile(i + 1)

    slot = i % 2
    for k in range(TOP_K):
        pltpu.make_async_copy(ys_hbm.at[pl.ds(0, tc)], buf.at[slot, k], sem_row.at[slot]).wait()

    gate = gate_ref[...]
    y = x_ref[...]
    for k in range(TOP_K):
        y = y + buf[slot, k] * gate[:, k:k + 1]
    o_ref[...] = _rms(y, g_ref[...])


def _combine(dest_flat, ys, x, gate_t, g, tc):
    m, d = x.shape
    return pl.pallas_call(
        functools.partial(_combine_kernel, m_tokens=m, tc=tc),
        grid=(m // tc,),
        in_specs=[pl.BlockSpec(memory_space=pl.ANY),
                  pl.BlockSpec(memory_space=pl.ANY),
                  pl.BlockSpec((tc, d), lambda i: (i, 0)),
                  pl.BlockSpec((tc, TOP_K), lambda i: (i, 0)),
                  pl.BlockSpec((1, d), lambda i: (0, 0))],
        out_specs=pl.BlockSpec((tc, d), lambda i: (i, 0)),
        out_shape=jax.ShapeDtypeStruct((m, d), F32),
        scratch_shapes=[pltpu.SMEM((TOP_K * SMEM_STAGE,), I32),
                        pltpu.VMEM((2, TOP_K, tc, d), F32),
                        pltpu.SemaphoreType.DMA,
                        pltpu.SemaphoreType.DMA((2,))],
        compiler_params=_params("arbitrary"),
        name="combine",
    )(dest_flat, ys, x, gate_t, g.reshape(1, d))


def _rope_tables(seq):
    half = ATT_HEAD_DIM // 2
    inv = ROPE_THETA ** (-jnp.arange(0, ATT_HEAD_DIM, 2, dtype=F32) / ATT_HEAD_DIM)
    ang = jnp.arange(seq, dtype=F32)[:, None] * inv[None, :]
    cos = jnp.tile(jnp.cos(ang), (1, LANES // half))
    sin = jnp.sin(ang)
    sin_signed = jnp.tile(jnp.concatenate([-sin, sin], axis=-1), (1, LANES // ATT_HEAD_DIM))
    return cos, sin_signed


def _pick(n, prefs):
    for p in prefs:
        if n % p == 0:
            return p
    return n


def _uninitialized_kernel(o_ref):
    del o_ref


def _uninitialized(shape, dtype):
    return pl.pallas_call(
        _uninitialized_kernel,
        out_specs=pl.BlockSpec(memory_space=pl.ANY),
        out_shape=jax.ShapeDtypeStruct(shape, dtype),
        name="uninitialized",
    )()


def _trunk_front(x, mem, wts, cnt0):
    batch, seq, d = x.shape
    n_mem = mem.shape[1]
    m = batch * seq
    x2d = x.reshape(m, d)

    n_in = wts["w_in"].shape[1]
    z = _norm_matmul(x2d, wts["norm_mix"], wts["w_in"], _pick(m, (1024, 512, 256, 128)),
                     _pick(n_in, (1664, 1024, 512, 256, 128)), F32, "in_projection")

    cos, sin_signed = _rope_tables(seq)
    att = _window_attention(z, wts["att_sink"], cos, sin_signed, batch, seq, _pick(seq, (256, 128)))
    hg = _hgrn2(z, wts["lb"], wts["hg_norm"], batch, seq, _pick(seq, (256, 128, 64)))
    kv = _norm_matmul(mem.reshape(batch * n_mem, d), wts["norm_mem"], wts["w_xkv"], n_mem, 2 * X_WIDTH, BF16,
                      "memory_kv")
    return _mix_tail(att, hg, x2d, wts["w_out"], wts["norm_cross"], kv, wts["w_xq"], wts["w_xo"], wts["norm_ffn"],
                     wts["w_router_t"], wts["b_router"], cnt0, batch, seq, n_mem, _pick(seq, (512, 256, 128)))


def _moe_and_final_norm(fronts, shapes, wts, moe_tile, ff_tile):
    counts = fronts[-1][5][:, 0].astype(I32)
    padded = (counts + moe_tile - 1) // moe_tile * moe_tile
    pend = jnp.cumsum(padded)
    pstart = pend - padded
    m_total = sum(f[0].shape[0] for f in fronts)
    n_tiles = -(-(m_total * TOP_K) // moe_tile) + N_EXPERTS
    p_rows = n_tiles * moe_tile
    tile_row = jnp.arange(n_tiles, dtype=I32) * moe_tile
    tile_e = jnp.minimum(jnp.sum((pend[None, :] <= tile_row[:, None]).astype(I32), axis=1), N_EXPERTS - 1)
    owner = tile_e[:, None] == jnp.arange(N_EXPERTS, dtype=I32)[None, :]
    seg_end = jnp.sum(jnp.where(owner, (pstart + counts)[None, :], 0), axis=1)
    tile_valid = jnp.clip(seg_end - tile_row, 0, moe_tile).astype(I32)
    n_used = (pend[-1:] // moe_tile).astype(I32)
    seg = jnp.stack([pstart, counts, pend]).astype(I32)

    h_rows, lanes = fronts[0][1].shape
    slabs = h_rows // fronts[0][0].shape[0]
    xs = _uninitialized((p_rows * slabs, lanes), fronts[0][1].dtype)
    dests = []
    for n, (x2, h3, idx, gate, rank, cnt) in enumerate(fronts):
        dest_flat = _assign_rows(pstart.astype(I32), idx, rank).reshape(-1)
        dests.append(dest_flat)
        xs = _dispatch(seg, dest_flat, h3, xs, x2.shape[0], p_rows, fill_padding=(n == 0))
    ys = _expert_mlp(tile_e, tile_valid, n_used, xs, wts["w_moe1"], wts["b_moe1"], wts["w_moe2"], wts["b_moe2"],
                     moe_tile, ff_tile)
    outs = []
    for (x2, h3, idx, gate, rank, cnt), dest_flat, shape in zip(fronts, dests, shapes):
        y = _combine(dest_flat, ys, x2, gate.T, wts["norm_final"], _pick(x2.shape[0], (256, 128)))
        outs.append(y.reshape(shape))
    return tuple(outs)


def kernel(x_prompt, x_sample, mem_prompt, mem_sample, norm_mix, w_in, att_sink, hg_lb_logits, hg_norm, w_out,
           norm_cross, norm_mem, w_xq, w_xkv, w_xo, norm_ffn, w_router, b_router, w_moe1, b_moe1, w_moe2, b_moe2,
           norm_final):
    assert w_in.shape[0] == 1, "the final norm is fused after the single layer"
    lb_all = jnp.cumsum(jax.nn.softmax(hg_lb_logits.astype(F32), axis=0), axis=0)
    d_ff = w_moe2.shape[2]
    wts = dict(
        norm_mix=norm_mix[0], w_in=w_in[0].astype(BF16), att_sink=att_sink[0], lb=lb_all[0],
        hg_norm=hg_norm[0], w_out=w_out[0].astype(BF16), norm_cross=norm_cross[0], norm_mem=norm_mem[0],
        w_xq=w_xq[0].astype(BF16), w_xkv=w_xkv[0].astype(BF16), w_xo=w_xo[0].astype(BF16),
        norm_ffn=norm_ffn[0], w_router_t=w_router[0].T, b_router=b_router[0],
        w_moe1=w_moe1[0].astype(BF16), b_moe1=b_moe1[0][:, None, :],
        w_moe2=w_moe2[0].astype(BF16), b_moe2=b_moe2[0][:, None, :],
        norm_final=norm_final,
    )
    ff_tile = _pick(d_ff, (1024, 512, 256, 128))
    fronts = []
    cnt = jnp.zeros((N_EXPERTS, 1), F32)
    for x, mem in ((x_prompt, mem_prompt), (x_sample, mem_sample)):
        front = _trunk_front(x, mem, wts, cnt)
        cnt = front[5]
        fronts.append(front)
    return _moe_and_final_norm(fronts, (x_prompt.shape, x_sample.shape), wts, MOE_TILE, ff_tile)
```

```python
import functools

import jax
import jax.numpy as jnp
from jax import lax
from jax.experimental import pallas as pl
from jax.experimental.pallas import tpu as pltpu

F32 = jnp.float32
BF16 = jnp.bfloat16
I32 = jnp.int32
U32 = jnp.uint32

EPS = 1e-5
ATT_HEADS = 16
ATT_KV_HEADS = 4
ATT_HEAD_DIM = 64
ATT_WIDTH = ATT_HEADS * ATT_HEAD_DIM
KV_WIDTH = ATT_KV_HEADS * ATT_HEAD_DIM
WINDOW = 128
ROPE_THETA = 10000.0
HG_HEADS = 8
HG_DIM = 128
HG_WIDTH = HG_HEADS * HG_DIM
HG_CHUNK = 64
X_HEADS = 4
X_HEAD_DIM = 128
X_WIDTH = X_HEADS * X_HEAD_DIM
N_EXPERTS = 32
TOP_K = 4
SWIGLU_ALPHA = 1.702
SWIGLU_LIMIT = 7.0

COL_Q_ATT = 0
COL_K_ATT = ATT_WIDTH
COL_V_ATT = ATT_WIDTH + KV_WIDTH
COL_Q_HG = ATT_WIDTH + 2 * KV_WIDTH
COL_F_FWD = COL_Q_HG + HG_WIDTH
COL_F_BWD = COL_F_FWD + HG_WIDTH
COL_I_HG = COL_F_BWD + HG_WIDTH
COL_G_HG = COL_I_HG + HG_WIDTH

LANES = 128
VMEM_LIMIT = 56 * 1024 * 1024
SMEM_STAGE = 1024
MOE_TILE = 512

NT_DIMS = (((1,), (1,)), ((), ()))
TN_DIMS = (((0,), (0,)), ((), ()))


def _params(*sem):
    return pltpu.CompilerParams(dimension_semantics=sem, vmem_limit_bytes=VMEM_LIMIT)


def _rms(x, g):
    ms = jnp.mean(x * x, axis=-1, keepdims=True)
    return x * lax.rsqrt(ms + EPS) * g


def _sigmoid(x):
    return 1.0 / (1.0 + jnp.exp(-x))


def _norm_matmul_kernel(x_ref, g_ref, w_ref, o_ref, xn_ref):
    @pl.when(pl.program_id(1) == 0)
    def _():
        xn_ref[...] = _rms(x_ref[...], g_ref[...]).astype(BF16)

    o_ref[...] = jnp.dot(xn_ref[...], w_ref[...], preferred_element_type=F32).astype(o_ref.dtype)


def _norm_matmul(x, g, w, tm, tn, out_dtype, name):
    m, d = x.shape
    n = w.shape[1]
    return pl.pallas_call(
        _norm_matmul_kernel,
        grid=(m // tm, n // tn),
        in_specs=[pl.BlockSpec((tm, d), lambda i, j: (i, 0)),
                  pl.BlockSpec((1, d), lambda i, j: (0, 0)),
                  pl.BlockSpec((d, tn), lambda i, j: (0, j))],
        out_specs=pl.BlockSpec((tm, tn), lambda i, j: (i, j)),
        out_shape=jax.ShapeDtypeStruct((m, n), out_dtype),
        scratch_shapes=[pltpu.VMEM((tm, d), BF16)],
        compiler_params=_params("arbitrary", "arbitrary"),
        name=name,
    )(x, g.reshape(1, d), w)


def _rope_pair(x, cos, sin_signed):
    lane = lax.broadcasted_iota(I32, (1, LANES), 1)
    first_half = (lane % ATT_HEAD_DIM) < (ATT_HEAD_DIM // 2)
    rot = jnp.where(first_half, pltpu.roll(x, LANES - ATT_HEAD_DIM // 2, 1), pltpu.roll(x, ATT_HEAD_DIM // 2, 1))
    return x * cos + rot * sin_signed


def _attn_kernel(sink_ref, q_ref, k_ref, v_ref, cq_ref, sq_ref, ck_ref, sk_ref, o_ref, klo_ref, khi_ref, vt_ref,
                 *, seq, tq):
    qi = pl.program_id(1)
    win = tq + 2 * WINDOW
    half_lanes = lax.broadcasted_iota(I32, (1, LANES), 1) < ATT_HEAD_DIM
    heads_per_slab = LANES // ATT_HEAD_DIM

    @pl.when(qi == 0)
    def _():
        for j in range(KV_WIDTH // LANES):
            slab = _rope_pair(k_ref[:, j * LANES:(j + 1) * LANES], ck_ref[...], sk_ref[...])
            swapped = pltpu.roll(slab, ATT_HEAD_DIM, 1)
            g0 = j * heads_per_slab
            g1 = g0 + 1
            klo_ref[:, g0 * LANES:(g0 + 1) * LANES] = jnp.where(half_lanes, slab, 0.0).astype(BF16)
            khi_ref[:, g0 * LANES:(g0 + 1) * LANES] = jnp.where(half_lanes, 0.0, swapped).astype(BF16)
            klo_ref[:, g1 * LANES:(g1 + 1) * LANES] = jnp.where(half_lanes, swapped, 0.0).astype(BF16)
            khi_ref[:, g1 * LANES:(g1 + 1) * LANES] = jnp.where(half_lanes, 0.0, slab).astype(BF16)
            vt_ref[j * LANES:(j + 1) * LANES, :] = v_ref[:, j * LANES:(j + 1) * LANES].T.astype(BF16)

    q0 = qi * tq
    ks = pl.multiple_of(jnp.clip(q0 - WINDOW, 0, seq - win), WINDOW)
    kpos = ks + lax.broadcasted_iota(I32, (win, 1), 0)
    qpos = q0 + lax.broadcasted_iota(I32, (1, tq), 1)
    valid = jnp.abs(kpos - qpos) <= WINDOW
    cq = cq_ref[...]
    sq = sq_ref[...]
    group = ATT_HEADS // ATT_KV_HEADS
    scores = []
    for j in range(ATT_WIDTH // LANES):
        qs = (_rope_pair(q_ref[:, j * LANES:(j + 1) * LANES], cq, sq) * (ATT_HEAD_DIM ** -0.5)).astype(BF16)
        for hh in range(heads_per_slab):
            g = (j * heads_per_slab + hh) // group
            k_ref_sel = klo_ref if hh == 0 else khi_ref
            ksel = k_ref_sel[pl.ds(ks, win), g * LANES:(g + 1) * LANES]
            scores.append(lax.dot_general(ksel, qs, NT_DIMS, preferred_element_type=F32))
    for j in range(ATT_WIDTH // LANES):
        outs = []
        for hh in range(heads_per_slab):
            h = j * heads_per_slab + hh
            g = h // group
            st = jnp.where(valid, scores[h], -1e30)
            sk = sink_ref[h]
            m = jnp.maximum(jnp.max(st, axis=0, keepdims=True), sk)
            p = jnp.exp(st - m)
            denom = jnp.sum(p, axis=0, keepdims=True) + jnp.exp(sk - m)
            vt = vt_ref[g * ATT_HEAD_DIM:(g + 1) * ATT_HEAD_DIM, pl.ds(ks, win)]
            ot = jnp.dot(vt, p.astype(BF16), preferred_element_type=F32)
            outs.append(ot / denom)
        o_ref[:, j * LANES:(j + 1) * LANES] = jnp.concatenate(outs, axis=0).T.astype(o_ref.dtype)


def _window_attention(z, sink, cos, sin_signed, batch, seq, tq):
    m = z.shape[0]
    nq = seq // tq
    qblk = COL_Q_ATT // ATT_WIDTH
    kblk = COL_K_ATT // KV_WIDTH
    vblk = COL_V_ATT // KV_WIDTH
    return pl.pallas_call(
        functools.partial(_attn_kernel, seq=seq, tq=tq),
        grid=(batch, nq),
        in_specs=[pl.BlockSpec(memory_space=pltpu.SMEM),
                  pl.BlockSpec((tq, ATT_WIDTH), lambda b, i: (b * nq + i, qblk)),
                  pl.BlockSpec((seq, KV_WIDTH), lambda b, i: (b, kblk)),
                  pl.BlockSpec((seq, KV_WIDTH), lambda b, i: (b, vblk)),
                  pl.BlockSpec((tq, LANES), lambda b, i: (i, 0)),
                  pl.BlockSpec((tq, LANES), lambda b, i: (i, 0)),
                  pl.BlockSpec((seq, LANES), lambda b, i: (0, 0)),
                  pl.BlockSpec((seq, LANES), lambda b, i: (0, 0))],
        out_specs=pl.BlockSpec((tq, ATT_WIDTH), lambda b, i: (b * nq + i, 0)),
        out_shape=jax.ShapeDtypeStruct((m, ATT_WIDTH), BF16),
        scratch_shapes=[pltpu.VMEM((seq, ATT_KV_HEADS * LANES), BF16),
                        pltpu.VMEM((seq, ATT_KV_HEADS * LANES), BF16),
                        pltpu.VMEM((KV_WIDTH, seq), BF16)],
        compiler_params=_params("arbitrary", "arbitrary"),
        name="window_attention",
    )(sink, z, z, z, cos, sin_signed, cos, sin_signed)


def _split_cumsum(tri, x):
    hi = x.astype(BF16)
    r1 = x - hi.astype(F32)
    mid = r1.astype(BF16)
    lo = (r1 - mid.astype(F32)).astype(BF16)
    dot = functools.partial(jnp.dot, preferred_element_type=F32)
    return dot(tri, hi) + dot(tri, mid) + dot(tri, lo)


def _run_interleaved(stage_generators):
    results = [None] * len(stage_generators)
    live = list(range(len(stage_generators)))
    while live:
        for i in list(live):
            try:
                next(stage_generators[i])
            except StopIteration as done:
                results[i] = done.value
                live.remove(i)
    return results


def _hgrn_group(zq, zf, v, lbv, state, mask, tri, forward):
    rows = zq.shape[0]
    c = HG_CHUNK
    nchunk = rows // c
    q = zq * _sigmoid(zq) * (HG_DIM ** -0.5)
    f = lbv + (1.0 - lbv) * _sigmoid(zf)
    logf = jnp.log(f)
    k = 1.0 - f
    yield
    b = _split_cumsum(tri, logf)
    yield
    ref_row = c // 2 - 1 if forward else c // 2
    last_row = c - 1 if forward else 0
    qe, ke, kd, qb, dec = [], [], [], [], []
    for n in range(nchunk):
        sl = slice(n * c, (n + 1) * c)
        bn, qn, kn = b[sl], q[sl], k[sl]
        bref = bn[ref_row:ref_row + 1]
        blast = bn[last_row:last_row + 1]
        qe.append(qn * jnp.exp(bn - bref))
        ke.append(kn * jnp.exp(bref - bn))
        kd.append((kn * jnp.exp(blast - bn)).astype(BF16))
        qb.append((qn * jnp.exp(bn)).astype(BF16))
        dec.append(jnp.exp(blast))
    qe = jnp.concatenate(qe, axis=0).astype(BF16)
    ke = jnp.concatenate(ke, axis=0).astype(BF16)
    vb = v.astype(BF16)
    a = lax.dot_general(qe, ke, NT_DIMS, preferred_element_type=F32)

    zero_blk = jnp.zeros((c, HG_DIM), BF16)

    def block_diag(blocks):
        return jnp.concatenate(
            [jnp.concatenate([blk if j == n else zero_blk for j in range(nchunk)], axis=1)
             for n, blk in enumerate(blocks)], axis=0)

    u_all = lax.dot_general(vb, block_diag(kd), TN_DIMS, preferred_element_type=F32)
    yield
    a = jnp.where(mask, a, 0.0).astype(BF16)
    o = jnp.dot(a, vb, preferred_element_type=F32)
    entering = [None] * nchunk
    order = range(nchunk) if forward else range(nchunk - 1, -1, -1)
    for n in order:
        entering[n] = state.astype(BF16)
        state = state * dec[n] + u_all[:, n * HG_DIM:(n + 1) * HG_DIM]
    o = o + lax.dot_general(block_diag(qb), jnp.concatenate(entering, axis=1), NT_DIMS,
                            preferred_element_type=F32)
    return o, state


def _hgrn_kernel(zq_ref, zff_ref, zfb_ref, zi_ref, zg_ref, lb_ref, gn_ref, o_ref, accf_ref, accb_ref, *, seq, rows):
    ngroups = seq // rows
    r = lax.broadcasted_iota(I32, (rows, rows), 0)
    cidx = lax.broadcasted_iota(I32, (rows, rows), 1)
    same = (r // HG_CHUNK) == (cidx // HG_CHUNK)
    mask_f = jnp.logical_and(same, cidx <= r)
    mask_b = jnp.logical_and(same, cidx >= r)
    tri_f = jnp.where(mask_f, 1.0, 0.0).astype(BF16)
    tri_b = jnp.where(mask_b, 1.0, 0.0).astype(BF16)
    lb_f = lb_ref[0:1, :]
    lb_b = lb_ref[1:2, :]
    zero_state = jnp.zeros((HG_DIM, HG_DIM), F32)

    def scan_body(it, states):
        sf, sb = states
        slf = pl.ds(pl.multiple_of(it * rows, rows), rows)
        slb = pl.ds(pl.multiple_of((ngroups - 1 - it) * rows, rows), rows)
        (of, sf), (ob, sb) = _run_interleaved([
            _hgrn_group(zq_ref[slf, :], zff_ref[slf, :], zi_ref[slf, :], lb_f, sf, mask_f, tri_f, True),
            _hgrn_group(zq_ref[slb, :], zfb_ref[slb, :], zi_ref[slb, :], lb_b, sb, mask_b, tri_b, False)])
        accf_ref[slf, :] = of
        accb_ref[slb, :] = ob
        return sf, sb

    lax.fori_loop(0, ngroups, scan_body, (zero_state, zero_state))

    def readout_body(gi, carry):
        sl = pl.ds(pl.multiple_of(gi * rows, rows), rows)
        tot = accf_ref[sl, :] + accb_ref[sl, :]
        zg = zg_ref[sl, :]
        y = _rms(tot, gn_ref[...]) * (zg * _sigmoid(zg))
        o_ref[sl, :] = y.astype(o_ref.dtype)
        return carry

    lax.fori_loop(0, ngroups, readout_body, 0)


def _hgrn2(z, lb, hg_norm, batch, seq, rows):
    m = z.shape[0]

    def zspec(col):
        base = col // HG_DIM
        return pl.BlockSpec((seq, HG_DIM), lambda b, h: (b, base + h))

    return pl.pallas_call(
        functools.partial(_hgrn_kernel, seq=seq, rows=rows),
        grid=(batch, HG_HEADS),
        in_specs=[zspec(COL_Q_HG), zspec(COL_F_FWD), zspec(COL_F_BWD), zspec(COL_I_HG), zspec(COL_G_HG),
                  pl.BlockSpec((2, HG_DIM), lambda b, h: (0, h)),
                  pl.BlockSpec((1, HG_DIM), lambda b, h: (0, 0))],
        out_specs=pl.BlockSpec((seq, HG_DIM), lambda b, h: (b, h)),
        out_shape=jax.ShapeDtypeStruct((m, HG_WIDTH), BF16),
        scratch_shapes=[pltpu.VMEM((seq, HG_DIM), F32), pltpu.VMEM((seq, HG_DIM), F32)],
        compiler_params=_params("arbitrary", "arbitrary"),
        name="hgrn2",
    )(z, z, z, z, z, lb, hg_norm.reshape(1, HG_DIM))


def _out_projection_math(a_ref, hg_ref, x_ref, wa_ref, wh_ref):
    acc = jnp.dot(a_ref[...], wa_ref[...], preferred_element_type=F32)
    acc = acc + jnp.dot(hg_ref[...], wh_ref[...], preferred_element_type=F32)
    return x_ref[...] + acc


def _cross_attention_math(x, g_ref, kv_ref, wq_ref, wo_ref):
    h = _rms(x, g_ref[...]).astype(BF16)
    q = jnp.dot(h, wq_ref[...], preferred_element_type=F32).astype(BF16)
    scores = []
    for hd in range(X_HEADS):
        sl = slice(hd * X_HEAD_DIM, (hd + 1) * X_HEAD_DIM)
        scores.append(lax.dot_general(q[:, sl], kv_ref[:, sl], NT_DIMS, preferred_element_type=F32))
    outs = []
    for hd in range(X_HEADS):
        vh = kv_ref[:, X_WIDTH + hd * X_HEAD_DIM:X_WIDTH + (hd + 1) * X_HEAD_DIM]
        s = scores[hd] * (X_HEAD_DIM ** -0.5)
        p = jnp.exp(s - jnp.max(s, axis=-1, keepdims=True))
        denom = jnp.sum(p, axis=-1, keepdims=True)
        outs.append(jnp.dot(p.astype(BF16), vh, preferred_element_type=F32) / denom)
    oc = jnp.concatenate(outs, axis=-1).astype(BF16)
    return x + jnp.dot(oc, wo_ref[...], preferred_element_type=F32)


def _router_math(x, g_ref, wr_ref, br_ref, h_ref, idx_ref, gate_ref, rank_ref, cnt_ref, base_ref):
    tm = x.shape[0]
    h = _rms(x, g_ref[...])
    hh = h.astype(BF16)
    half = h.shape[1] // 2
    slabs = half // LANES
    bits = lax.bitcast_convert_type(hh.astype(F32), U32)
    packed = (bits[:, :half] >> 16) | (bits[:, half:] & jnp.uint32(0xFFFF0000))
    for j in range(slabs):
        h_ref[pl.ds(j, tm, stride=slabs), :] = packed[:, j * LANES:(j + 1) * LANES]
    hl = (h - hh.astype(F32)).astype(BF16)
    w = wr_ref[...]
    wh = w.astype(BF16)
    wl = (w - wh.astype(F32)).astype(BF16)
    nt = functools.partial(lax.dot_general, dimension_numbers=NT_DIMS, preferred_element_type=F32)
    logits = nt(wh, hh) + nt(wh, hl) + nt(wl, hh) + br_ref[...]

    eio = lax.broadcasted_iota(I32, (N_EXPERTS, tm), 0).astype(F32)
    work = logits
    vals, onehots = [], []
    for k in range(TOP_K):
        mx = jnp.max(work, axis=0, keepdims=True)
        ix = jnp.min(jnp.where(work == mx, eio, float(N_EXPERTS)), axis=0, keepdims=True)
        sel = eio == ix
        vals.append(mx)
        onehots.append(sel)
        idx_ref[k:k + 1, :] = ix.astype(I32)
        work = jnp.where(sel, -jnp.inf, work)
    ex = [jnp.exp(v - vals[0]) for v in vals]
    denom = ex[0] + ex[1] + ex[2] + ex[3]
    for k in range(TOP_K):
        gate_ref[k:k + 1, :] = ex[k] / denom

    oh = [jnp.where(s, 1.0, 0.0) for s in onehots]
    oh_all = oh[0] + oh[1] + oh[2] + oh[3]
    r = lax.broadcasted_iota(I32, (tm, tm), 0)
    c = lax.broadcasted_iota(I32, (tm, tm), 1)
    upper = jnp.where(r < c, 1.0, 0.0).astype(BF16)
    before = jnp.dot(oh_all.astype(BF16), upper, preferred_element_type=F32) + base_ref[...]
    for k in range(TOP_K):
        rank_ref[k:k + 1, :] = jnp.sum(oh[k] * before, axis=0, keepdims=True).astype(I32)
    base_ref[...] = base_ref[...] + jnp.sum(oh_all, axis=1, keepdims=True)
    cnt_ref[...] = base_ref[...]


def _mix_tail_kernel(a_ref, hg_ref, x_ref, wa_ref, wh_ref, gx_ref, kv_ref, wq_ref, wo_ref, gr_ref, wr_ref, br_ref,
                     cnt0_ref, x2_ref, h_ref, idx_ref, gate_ref, rank_ref, cnt_ref, base_ref):
    @pl.when(jnp.logical_and(pl.program_id(0) == 0, pl.program_id(1) == 0))
    def _():
        base_ref[...] = cnt0_ref[...]

    x1 = _out_projection_math(a_ref, hg_ref, x_ref, wa_ref, wh_ref)
    x2 = _cross_attention_math(x1, gx_ref, kv_ref, wq_ref, wo_ref)
    x2_ref[...] = x2
    _router_math(x2, gr_ref, wr_ref, br_ref, h_ref, idx_ref, gate_ref, rank_ref, cnt_ref, base_ref)


def _mix_tail(att, hg, x, w_out, g_cross, kv, w_xq, w_xo, g_ffn, w_router_t, b_router, cnt0, batch, seq, n_mem, tm):
    m, d = x.shape
    nt = seq // tm

    def tile(width):
        return pl.BlockSpec((tm, width), lambda b, i: (b * nt + i, 0))

    def resident(shape, row_block=0):
        return pl.BlockSpec(shape, lambda b, i: (row_block, 0), pipeline_mode=pl.Buffered(1))

    row4 = pl.BlockSpec((TOP_K, tm), lambda b, i: (0, b * nt + i))
    slabs = d // 2 // LANES
    return pl.pallas_call(
        _mix_tail_kernel,
        grid=(batch, nt),
        in_specs=[tile(ATT_WIDTH), tile(HG_WIDTH), tile(d),
                  resident((ATT_WIDTH, d)), resident((HG_WIDTH, d), ATT_WIDTH // HG_WIDTH),
                  resident((1, d)),
                  pl.BlockSpec((n_mem, 2 * X_WIDTH), lambda b, i: (b, 0)),
                  resident((d, X_WIDTH)), resident((X_WIDTH, d)),
                  resident((1, d)), resident((N_EXPERTS, d)), resident((N_EXPERTS, 1)),
                  resident((N_EXPERTS, 1))],
        out_specs=[tile(d), pl.BlockSpec((tm * slabs, LANES), lambda b, i: (b * nt + i, 0)), row4, row4, row4,
                   pl.BlockSpec((N_EXPERTS, 1), lambda b, i: (0, 0))],
        out_shape=[jax.ShapeDtypeStruct((m, d), F32),
                   jax.ShapeDtypeStruct((m * slabs, LANES), U32),
                   jax.ShapeDtypeStruct((TOP_K, m), I32),
                   jax.ShapeDtypeStruct((TOP_K, m), F32),
                   jax.ShapeDtypeStruct((TOP_K, m), I32),
                   jax.ShapeDtypeStruct((N_EXPERTS, 1), F32)],
        scratch_shapes=[pltpu.VMEM((N_EXPERTS, 1), F32)],
        compiler_params=_params("arbitrary", "arbitrary"),
        name="mix_tail",
    )(att, hg, x, w_out, w_out, g_cross.reshape(1, d), kv, w_xq, w_xo, g_ffn.reshape(1, d), w_router_t,
      b_router.reshape(N_EXPERTS, 1), cnt0)


def _assign_rows_kernel(pstart_ref, idx_ref, rank_ref, dest_ref):
    idx = idx_ref[...]
    start = jnp.zeros_like(idx)
    for e in range(N_EXPERTS):
        start = jnp.where(idx == e, pstart_ref[e], start)
    dest_ref[...] = start + rank_ref[...]


def _assign_rows(pstart, idx, rank):
    full = pl.BlockSpec(idx.shape, lambda: (0, 0))
    return pl.pallas_call(
        _assign_rows_kernel,
        in_specs=[pl.BlockSpec(memory_space=pltpu.SMEM), full, full],
        out_specs=full,
        out_shape=jax.ShapeDtypeStruct(idx.shape, I32),
        name="assign_rows",
    )(pstart, idx, rank)


def _dispatch_kernel(seg_ref, dest_hbm, h_ref, xs_prev_hbm, xs_hbm, dsm, zrow, sem_idx, sem_row,
                     *, m_tokens, p_rows, fill_padding):
    del xs_prev_hbm
    i = pl.program_id(0)
    tt = SMEM_STAGE
    slabs = h_ref.shape[0] // tt

    def token_rows(ref, tok):
        return ref.at[pl.ds(pl.multiple_of(tok * slabs, slabs), slabs)]

    def idx_copy(k):
        return pltpu.make_async_copy(dest_hbm.at[pl.ds(k * m_tokens + i * tt, tt)],
                                     dsm.at[pl.ds(k * tt, tt)], sem_idx)

    for k in range(TOP_K):
        idx_copy(k).start()
    for k in range(TOP_K):
        idx_copy(k).wait()

    def row_copy(t, dst_row):
        return pltpu.make_async_copy(token_rows(h_ref, t), token_rows(xs_hbm, dst_row), sem_row)

    def issue(t, carry):
        for k in range(TOP_K):
            row_copy(t, dsm[k * tt + t]).start(priority=k % 2)
        return carry

    lax.fori_loop(0, tt, issue, 0, unroll=8)

    for k in range(TOP_K):
        pltpu.make_async_copy(h_ref, xs_hbm.at[pl.ds(0, tt * slabs)], sem_row).wait()

    @pl.when(jnp.logical_and(i == 0, jnp.bool_(fill_padding)))
    def _():
        zrow[...] = jnp.zeros_like(zrow)
        max_chunk = zrow.shape[0] // slabs

        def zero_rows(dst_row, nrows):
            copy = pltpu.make_async_copy(
                zrow.at[pl.ds(0, nrows * slabs)],
                xs_hbm.at[pl.ds(pl.multiple_of(dst_row * slabs, slabs), nrows * slabs)], sem_row)
            copy.start()
            copy.wait()

        def per_expert(e, carry):
            lo = seg_ref[0, e] + seg_ref[1, e]
            pad = seg_ref[2, e] - lo
            chunk = max_chunk
            while chunk >= 1:
                take = pad & chunk

                @pl.when(take != 0)
                def _():
                    zero_rows(lo, chunk)

                lo = lo + take
                chunk //= 2
            return carry

        lax.fori_loop(0, N_EXPERTS, per_expert, 0)

        def tail(c, carry):
            zero_rows(c * max_chunk, max_chunk)
            return carry

        lax.fori_loop(seg_ref[2, N_EXPERTS - 1] // max_chunk, p_rows // max_chunk, tail, 0)


def _dispatch(seg, dest_flat, h, xs_prev, m, p_rows, fill_padding):
    slabs = h.shape[0] // m
    return pl.pallas_call(
        functools.partial(_dispatch_kernel, m_tokens=m, p_rows=p_rows, fill_padding=fill_padding),
        grid=(m // SMEM_STAGE,),
        in_specs=[pl.BlockSpec(memory_space=pltpu.SMEM),
                  pl.BlockSpec(memory_space=pl.ANY),
                  pl.BlockSpec((SMEM_STAGE * slabs, LANES), lambda i: (i, 0)),
                  pl.BlockSpec(memory_space=pl.ANY)],
        out_specs=pl.BlockSpec(memory_space=pl.ANY),
        out_shape=jax.ShapeDtypeStruct((p_rows * slabs, LANES), h.dtype),
        input_output_aliases={3: 0},
        scratch_shapes=[pltpu.SMEM((TOP_K * SMEM_STAGE,), I32),
                        pltpu.VMEM((MOE_TILE // 2 * slabs, LANES), h.dtype),
                        pltpu.SemaphoreType.DMA,
                        pltpu.SemaphoreType.DMA],
        compiler_params=_params("arbitrary"),
        name="dispatch",
    )(seg, dest_flat, h, xs_prev)


def _expert_kernel(te_ref, tv_ref, nu_ref, x_ref, w1g_ref, w1l_ref, b1g_ref, b1l_ref, w2_ref, b2_ref,
                   o_ref, xb_ref):
    i = pl.program_id(0)
    f = pl.program_id(1)
    nf = pl.num_programs(1)
    live = tv_ref[i] > 0

    @pl.when(live)
    def _():
        @pl.when(f == 0)
        def _():
            o_ref[...] = jnp.broadcast_to(b2_ref[0], o_ref.shape)

        tmb, d = xb_ref.shape
        half = d // 2
        slabs = half // LANES
        for j in range(slabs):
            words = x_ref[pl.ds(j, tmb, stride=slabs), :]
            lo = slice(j * LANES, (j + 1) * LANES)
            hi = slice(half + j * LANES, half + (j + 1) * LANES)
            xb_ref[:, lo] = lax.bitcast_convert_type(words << 16, F32).astype(BF16)
            xb_ref[:, hi] = lax.bitcast_convert_type(words & jnp.uint32(0xFFFF0000), F32).astype(BF16)

        xb = xb_ref[...]
        glu = jnp.dot(xb, w1g_ref[0], preferred_element_type=F32) + b1g_ref[0]
        lin = jnp.dot(xb, w1l_ref[0], preferred_element_type=F32) + b1l_ref[0]
        glu = jnp.minimum(glu, SWIGLU_LIMIT)
        lin = jnp.clip(lin, -SWIGLU_LIMIT, SWIGLU_LIMIT)
        act = glu * _sigmoid(SWIGLU_ALPHA * glu) * (lin + 1.0)
        o_ref[...] += jnp.dot(act.astype(BF16), w2_ref[0], preferred_element_type=F32)

    @pl.when(jnp.logical_and(jnp.logical_not(live), f == nf - 1))
    def _():
        o_ref[...] = jnp.zeros_like(o_ref)


def _expert_mlp(tile_e, tile_valid, n_used, xs, w1, b1, w2, b2, tmb, tf):
    d = w1.shape[1]
    p_rows = xs.shape[0] // (d // 2 // LANES)
    d_ff = w2.shape[1]
    nf = d_ff // tf
    n_tiles = p_rows // tmb

    def fsel(i, f, tv):
        return jnp.where(tv[i] > 0, f, nf - 1)

    grid_spec = pltpu.PrefetchScalarGridSpec(
        num_scalar_prefetch=3,
        grid=(n_tiles, nf),
        in_specs=[
            pl.BlockSpec((tmb * (d // 2 // LANES), LANES), lambda i, f, te, tv, nu: (jnp.minimum(i, nu[0] - 1), 0)),
            pl.BlockSpec((1, d, tf), lambda i, f, te, tv, nu: (te[i], 0, fsel(i, f, tv))),
            pl.BlockSpec((1, d, tf), lambda i, f, te, tv, nu: (te[i], 0, nf + fsel(i, f, tv))),
            pl.BlockSpec((1, 1, tf), lambda i, f, te, tv, nu: (te[i], 0, fsel(i, f, tv))),
            pl.BlockSpec((1, 1, tf), lambda i, f, te, tv, nu: (te[i], 0, nf + fsel(i, f, tv))),
            pl.BlockSpec((1, tf, d), lambda i, f, te, tv, nu: (te[i], fsel(i, f, tv), 0)),
            pl.BlockSpec((1, 1, d), lambda i, f, te, tv, nu: (te[i], 0, 0)),
        ],
        out_specs=pl.BlockSpec((tmb, d), lambda i, f, te, tv, nu: (i, 0)),
        scratch_shapes=[pltpu.VMEM((tmb, d), BF16)],
    )
    return pl.pallas_call(
        _expert_kernel,
        grid_spec=grid_spec,
        out_shape=jax.ShapeDtypeStruct((p_rows, d), F32),
        compiler_params=_params("arbitrary", "arbitrary"),
        name="expert_mlp",
    )(tile_e, tile_valid, n_used, xs, w1, w1, b1, b1, w2, b2)


def _combine_kernel(dest_hbm, ys_hbm, x_ref, gate_ref, g_ref, o_ref, dsm, buf, sem_idx, sem_row, *, m_tokens, tc):
    i = pl.program_id(0)
    n = pl.num_programs(0)
    per_stage = SMEM_STAGE // tc

    def issue_tile(tile):
        slot = tile % 2
        off = (tile % per_stage) * tc

        @pl.when(tile % per_stage == 0)
        def _():
            base = (tile // per_stage) * SMEM_STAGE

            def idx_copy(k):
                return pltpu.make_async_copy(dest_hbm.at[pl.ds(k * m_tokens + base, SMEM_STAGE)],
                                             dsm.at[pl.ds(k * SMEM_STAGE, SMEM_STAGE)], sem_idx)

            for k in range(TOP_K):
                idx_copy(k).start()
            for k in range(TOP_K):
                idx_copy(k).wait()

        def issue(t, carry):
            for k in range(TOP_K):
                pltpu.make_async_copy(ys_hbm.at[pl.ds(dsm[k * SMEM_STAGE + off + t], 1)],
                                      buf.at[slot, k, pl.ds(t, 1)], sem_row.at[slot]).start(priority=k % 2)
            return carry

        lax.fori_loop(0, tc, issue, 0, unroll=8)

    @pl.when(i == 0)
    def _():
        issue_tile(i)

    @pl.when(i + 1 < n)
    def _():
        issue_tile(i + 1)

    slot = i % 2
    for k in range(TOP_K):
        pltpu.make_async_copy(ys_hbm.at[pl.ds(0, tc)], buf.at[slot, k], sem_row.at[slot]).wait()

    gate = gate_ref[...]
    y = x_ref[...]
    for k in range(TOP_K):
        y = y + buf[slot, k] * gate[:, k:k + 1]
    o_ref[...] = _rms(y, g_ref[...])


def _combine(dest_flat, ys, x, gate_t, g, tc):
    m, d = x.shape
    return pl.pallas_call(
        functools.partial(_combine_kernel, m_tokens=m, tc=tc),
        grid=(m // tc,),
        in_specs=[pl.BlockSpec(memory_space=pl.ANY),
                  pl.BlockSpec(memory_space=pl.ANY),
                  pl.BlockSpec((tc, d), lambda i: (i, 0)),
                  pl.BlockSpec((tc, TOP_K), lambda i: (i, 0)),
                  pl.BlockSpec((1, d), lambda i: (0, 0))],
        out_specs=pl.BlockSpec((tc, d), lambda i: (i, 0)),
        out_shape=jax.ShapeDtypeStruct((m, d), F32),
        scratch_shapes=[pltpu.SMEM((TOP_K * SMEM_STAGE,), I32),
                        pltpu.VMEM((2, TOP_K, tc, d), F32),
                        pltpu.SemaphoreType.DMA,
                        pltpu.SemaphoreType.DMA((2,))],
        compiler_params=_params("arbitrary"),
        name="combine",
    )(dest_flat, ys, x, gate_t, g.reshape(1, d))


def _rope_tables(seq):
    half = ATT_HEAD_DIM // 2
    inv = ROPE_THETA ** (-jnp.arange(0, ATT_HEAD_DIM, 2, dtype=F32) / ATT_HEAD_DIM)
    ang = jnp.arange(seq, dtype=F32)[:, None] * inv[None, :]
    cos = jnp.tile(jnp.cos(ang), (1, LANES // half))
    sin = jnp.sin(ang)
    sin_signed = jnp.tile(jnp.concatenate([-sin, sin], axis=-1), (1, LANES // ATT_HEAD_DIM))
    return cos, sin_signed


def _pick(n, prefs):
    for p in prefs:
        if n % p == 0:
            return p
    return n


def _uninitialized_kernel(o_ref):
    del o_ref


def _uninitialized(shape, dtype):
    return pl.pallas_call(
        _uninitialized_kernel,
        out_specs=pl.BlockSpec(memory_space=pl.ANY),
        out_shape=jax.ShapeDtypeStruct(shape, dtype),
        name="uninitialized",
    )()


def _trunk_front(x, mem, wts, cnt0):
    batch, seq, d = x.shape
    n_mem = mem.shape[1]
    m = batch * seq
    x2d = x.reshape(m, d)

    n_in = wts["w_in"].shape[1]
    z = _norm_matmul(x2d, wts["norm_mix"], wts["w_in"], _pick(m, (1024, 512, 256, 128)),
                     _pick(n_in, (1664, 1024, 512, 256, 128)), F32, "in_projection")

    cos, sin_signed = _rope_tables(seq)
    att = _window_attention(z, wts["att_sink"], cos, sin_signed, batch, seq, _pick(seq, (256, 128)))
    hg = _hgrn2(z, wts["lb"], wts["hg_norm"], batch, seq, _pick(seq, (256, 128, 64)))
    kv = _norm_matmul(mem.reshape(batch * n_mem, d), wts["norm_mem"], wts["w_xkv"], n_mem, 2 * X_WIDTH, BF16,
                      "memory_kv")
    return _mix_tail(att, hg, x2d, wts["w_out"], wts["norm_cross"], kv, wts["w_xq"], wts["w_xo"], wts["norm_ffn"],
                     wts["w_router_t"], wts["b_router"], cnt0, batch, seq, n_mem, _pick(seq, (512, 256, 128)))


def _moe_and_final_norm(fronts, shapes, wts, moe_tile, ff_tile):
    counts = fronts[-1][5][:, 0].astype(I32)
    padded = (counts + moe_tile - 1) // moe_tile * moe_tile
    pend = jnp.cumsum(padded)
    pstart = pend - padded
    m_total = sum(f[0].shape[0] for f in fronts)
    n_tiles = -(-(m_total * TOP_K) // moe_tile) + N_EXPERTS
    p_rows = n_tiles * moe_tile
    tile_row = jnp.arange(n_tiles, dtype=I32) * moe_tile
    tile_e = jnp.minimum(jnp.sum((pend[None, :] <= tile_row[:, None]).astype(I32), axis=1), N_EXPERTS - 1)
    owner = tile_e[:, None] == jnp.arange(N_EXPERTS, dtype=I32)[None, :]
    seg_end = jnp.sum(jnp.where(owner, (pstart + counts)[None, :], 0), axis=1)
    tile_valid = jnp.clip(seg_end - tile_row, 0, moe_tile).astype(I32)
    n_used = (pend[-1:] // moe_tile).astype(I32)
    seg = jnp.stack([pstart, counts, pend]).astype(I32)

    h_rows, lanes = fronts[0][1].shape
    slabs = h_rows // fronts[0][0].shape[0]
    xs = _uninitialized((p_rows * slabs, lanes), fronts[0][1].dtype)
    dests = []
    for n, (x2, h3, idx, gate, rank, cnt) in enumerate(fronts):
        dest_flat = _assign_rows(pstart.astype(I32), idx, rank).reshape(-1)
        dests.append(dest_flat)
        xs = _dispatch(seg, dest_flat, h3, xs, x2.shape[0], p_rows, fill_padding=(n == 0))
    ys = _expert_mlp(tile_e, tile_valid, n_used, xs, wts["w_moe1"], wts["b_moe1"], wts["w_moe2"], wts["b_moe2"],
                     moe_tile, ff_tile)
    outs = []
    for (x2, h3, idx, gate, rank, cnt), dest_flat, shape in zip(fronts, dests, shapes):
        y = _combine(dest_flat, ys, x2, gate.T, wts["norm_final"], _pick(x2.shape[0], (256, 128)))
        outs.append(y.reshape(shape))
    return tuple(outs)


def kernel(x_prompt, x_sample, mem_prompt, mem_sample, norm_mix, w_in, att_sink, hg_lb_logits, hg_norm, w_out,
           norm_cross, norm_mem, w_xq, w_xkv, w_xo, norm_ffn, w_router, b_router, w_moe1, b_moe1, w_moe2, b_moe2,
           norm_final):
    assert w_in.shape[0] == 1, "the final norm is fused after the single layer"
    lb_all = jnp.cumsum(jax.nn.softmax(hg_lb_logits.astype(F32), axis=0), axis=0)
    d_ff = w_moe2.shape[2]
    wts = dict(
        norm_mix=norm_mix[0], w_in=w_in[0].astype(BF16), att_sink=att_sink[0], lb=lb_all[0],
        hg_norm=hg_norm[0], w_out=w_out[0].astype(BF16), norm_cross=norm_cross[0], norm_mem=norm_mem[0],
        w_xq=w_xq[0].astype(BF16), w_xkv=w_xkv[0].astype(BF16), w_xo=w_xo[0].astype(BF16),
        norm_ffn=norm_ffn[0], w_router_t=w_router[0].T, b_router=b_router[0],
        w_moe1=w_moe1[0].astype(BF16), b_moe1=b_moe1[0][:, None, :],
        w_moe2=w_moe2[0].astype(BF16), b_moe2=b_moe2[0][:, None, :],
        norm_final=norm_final,
    )
    ff_tile = _pick(d_ff, (1024, 512, 256, 128))
    fronts = []
    cnt = jnp.zeros((N_EXPERTS, 1), F32)
    for x, mem in ((x_prompt, mem_prompt), (x_sample, mem_sample)):
        front = _trunk_front(x, mem, wts, cnt)
        cnt = front[5]
        fronts.append(front)
    return _moe_and_final_norm(fronts, (x_prompt.shape, x_sample.shape), wts, MOE_TILE, ff_tile)
```

```python
import functools

import jax
import jax.numpy as jnp
from jax import lax
from jax.experimental import pallas as pl
from jax.experimental.pallas import tpu as pltpu

F32 = jnp.float32
BF16 = jnp.bfloat16
I32 = jnp.int32
U32 = jnp.uint32

EPS = 1e-5
ATT_HEADS = 16
ATT_KV_HEADS = 4
ATT_HEAD_DIM = 64
ATT_WIDTH = ATT_HEADS * ATT_HEAD_DIM
KV_WIDTH = ATT_KV_HEADS * ATT_HEAD_DIM
WINDOW = 128
ROPE_THETA = 10000.0
HG_HEADS = 8
HG_DIM = 128
HG_WIDTH = HG_HEADS * HG_DIM
HG_CHUNK = 64
X_HEADS = 4
X_HEAD_DIM = 128
X_WIDTH = X_HEADS * X_HEAD_DIM
N_EXPERTS = 32
TOP_K = 4
SWIGLU_ALPHA = 1.702
SWIGLU_LIMIT = 7.0

COL_Q_ATT = 0
COL_K_ATT = ATT_WIDTH
COL_V_ATT = ATT_WIDTH + KV_WIDTH
COL_Q_HG = ATT_WIDTH + 2 * KV_WIDTH
COL_F_FWD = COL_Q_HG + HG_WIDTH
COL_F_BWD = COL_F_FWD + HG_WIDTH
COL_I_HG = COL_F_BWD + HG_WIDTH
COL_G_HG = COL_I_HG + HG_WIDTH

LANES = 128
VMEM_LIMIT = 56 * 1024 * 1024
SMEM_STAGE = 1024
MOE_TILE = 512

NT_DIMS = (((1,), (1,)), ((), ()))
TN_DIMS = (((0,), (0,)), ((), ()))


def _params(*sem):
    return pltpu.CompilerParams(dimension_semantics=sem, vmem_limit_bytes=VMEM_LIMIT)


def _rms(x, g):
    ms = jnp.mean(x * x, axis=-1, keepdims=True)
    return x * lax.rsqrt(ms + EPS) * g


def _sigmoid(x):
    return 1.0 / (1.0 + jnp.exp(-x))


def _norm_matmul_kernel(x_ref, g_ref, w_ref, o_ref, xn_ref):
    @pl.when(pl.program_id(1) == 0)
    def _():
        xn_ref[...] = _rms(x_ref[...], g_ref[...]).astype(BF16)

    o_ref[...] = jnp.dot(xn_ref[...], w_ref[...], preferred_element_type=F32).astype(o_ref.dtype)


def _norm_matmul(x, g, w, tm, tn, out_dtype, name):
    m, d = x.shape
    n = w.shape[1]
    return pl.pallas_call(
        _norm_matmul_kernel,
        grid=(m // tm, n // tn),
        in_specs=[pl.BlockSpec((tm, d), lambda i, j: (i, 0)),
                  pl.BlockSpec((1, d), lambda i, j: (0, 0)),
                  pl.BlockSpec((d, tn), lambda i, j: (0, j))],
        out_specs=pl.BlockSpec((tm, tn), lambda i, j: (i, j)),
        out_shape=jax.ShapeDtypeStruct((m, n), out_dtype),
        scratch_shapes=[pltpu.VMEM((tm, d), BF16)],
        compiler_params=_params("arbitrary", "arbitrary"),
        name=name,
    )(x, g.reshape(1, d), w)


def _rope_pair(x, cos, sin_signed):
    lane = lax.broadcasted_iota(I32, (1, LANES), 1)
    first_half = (lane % ATT_HEAD_DIM) < (ATT_HEAD_DIM // 2)
    rot = jnp.where(first_half, pltpu.roll(x, LANES - ATT_HEAD_DIM // 2, 1), pltpu.roll(x, ATT_HEAD_DIM // 2, 1))
    return x * cos + rot * sin_signed


def _attn_kernel(sink_ref, q_ref, k_ref, v_ref, cq_ref, sq_ref, ck_ref, sk_ref, o_ref, klo_ref, khi_ref, vt_ref,
                 *, seq, tq):
    qi = pl.program_id(1)
    win = tq + 2 * WINDOW
    half_lanes = lax.broadcasted_iota(I32, (1, LANES), 1) < ATT_HEAD_DIM
    heads_per_slab = LANES // ATT_HEAD_DIM

    @pl.when(qi == 0)
    def _():
        for j in range(KV_WIDTH // LANES):
            slab = _rope_pair(k_ref[:, j * LANES:(j + 1) * LANES], ck_ref[...], sk_ref[...])
            swapped = pltpu.roll(slab, ATT_HEAD_DIM, 1)
            g0 = j * heads_per_slab
            g1 = g0 + 1
            klo_ref[:, g0 * LANES:(g0 + 1) * LANES] = jnp.where(half_lanes, slab, 0.0).astype(BF16)
            khi_ref[:, g0 * LANES:(g0 + 1) * LANES] = jnp.where(half_lanes, 0.0, swapped).astype(BF16)
            klo_ref[:, g1 * LANES:(g1 + 1) * LANES] = jnp.where(half_lanes, swapped, 0.0).astype(BF16)
            khi_ref[:, g1 * LANES:(g1 + 1) * LANES] = jnp.where(half_lanes, 0.0, slab).astype(BF16)
            vt_ref[j * LANES:(j + 1) * LANES, :] = v_ref[:, j * LANES:(j + 1) * LANES].T.astype(BF16)

    q0 = qi * tq
    ks = pl.multiple_of(jnp.clip(q0 - WINDOW, 0, seq - win), WINDOW)
    kpos = ks + lax.broadcasted_iota(I32, (win, 1), 0)
    qpos = q0 + lax.broadcasted_iota(I32, (1, tq), 1)
    valid = jnp.abs(kpos - qpos) <= WINDOW
    cq = cq_ref[...]
    sq = sq_ref[...]
    group = ATT_HEADS // ATT_KV_HEADS
    scores = []
    for j in range(ATT_WIDTH // LANES):
        qs = (_rope_pair(q_ref[:, j * LANES:(j + 1) * LANES], cq, sq) * (ATT_HEAD_DIM ** -0.5)).astype(BF16)
        for hh in range(heads_per_slab):
            g = (j * heads_per_slab + hh) // group
            k_ref_sel = klo_ref if hh == 0 else khi_ref
            ksel = k_ref_sel[pl.ds(ks, win), g * LANES:(g + 1) * LANES]
            scores.append(lax.dot_general(ksel, qs, NT_DIMS, preferred_element_type=F32))
    for j in range(ATT_WIDTH // LANES):
        outs = []
        for hh in range(heads_per_slab):
            h = j * heads_per_slab + hh
            g = h // group
            st = jnp.where(valid, scores[h], -1e30)
            sk = sink_ref[h]
            m = jnp.maximum(jnp.max(st, axis=0, keepdims=True), sk)
            p = jnp.exp(st - m)
            denom = jnp.sum(p, axis=0, keepdims=True) + jnp.exp(sk - m)
            vt = vt_ref[g * ATT_HEAD_DIM:(g + 1) * ATT_HEAD_DIM, pl.ds(ks, win)]
            ot = jnp.dot(vt, p.astype(BF16), preferred_element_type=F32)
            outs.append(ot / denom)
        o_ref[:, j * LANES:(j + 1) * LANES] = jnp.concatenate(outs, axis=0).T.astype(o_ref.dtype)


def _window_attention(z, sink, cos, sin_signed, batch, seq, tq):
    m = z.shape[0]
    nq = seq // tq
    qblk = COL_Q_ATT // ATT_WIDTH
    kblk = COL_K_ATT // KV_WIDTH
    vblk = COL_V_ATT // KV_WIDTH
    return pl.pallas_call(
        functools.partial(_attn_kernel, seq=seq, tq=tq),
        grid=(batch, nq),
        in_specs=[pl.BlockSpec(memory_space=pltpu.SMEM),
                  pl.BlockSpec((tq, ATT_WIDTH), lambda b, i: (b * nq + i, qblk)),
                  pl.BlockSpec((seq, KV_WIDTH), lambda b, i: (b, kblk)),
                  pl.BlockSpec((seq, KV_WIDTH), lambda b, i: (b, vblk)),
                  pl.BlockSpec((tq, LANES), lambda b, i: (i, 0)),
                  pl.BlockSpec((tq, LANES), lambda b, i: (i, 0)),
                  pl.BlockSpec((seq, LANES), lambda b, i: (0, 0)),
                  pl.BlockSpec((seq, LANES), lambda b, i: (0, 0))],
        out_specs=pl.BlockSpec((tq, ATT_WIDTH), lambda b, i: (b * nq + i, 0)),
        out_shape=jax.ShapeDtypeStruct((m, ATT_WIDTH), BF16),
        scratch_shapes=[pltpu.VMEM((seq, ATT_KV_HEADS * LANES), BF16),
                        pltpu.VMEM((seq, ATT_KV_HEADS * LANES), BF16),
                        pltpu.VMEM((KV_WIDTH, seq), BF16)],
        compiler_params=_params("arbitrary", "arbitrary"),
        name="window_attention",
    )(sink, z, z, z, cos, sin_signed, cos, sin_signed)


def _split_cumsum(tri, x):
    hi = x.astype(BF16)
    r1 = x - hi.astype(F32)
    mid = r1.astype(BF16)
    lo = (r1 - mid.astype(F32)).astype(BF16)
    dot = functools.partial(jnp.dot, preferred_element_type=F32)
    return dot(tri, hi) + dot(tri, mid) + dot(tri, lo)


def _run_interleaved(stage_generators):
    results = [None] * len(stage_generators)
    live = list(range(len(stage_generators)))
    while live:
        for i in list(live):
            try:
                next(stage_generators[i])
            except StopIteration as done:
                results[i] = done.value
                live.remove(i)
    return results


def _hgrn_group(zq, zf, v, lbv, state, mask, tri, forward):
    rows = zq.shape[0]
    c = HG_CHUNK
    nchunk = rows // c
    q = zq * _sigmoid(zq) * (HG_DIM ** -0.5)
    f = lbv + (1.0 - lbv) * _sigmoid(zf)
    logf = jnp.log(f)
    k = 1.0 - f
    yield
    b = _split_cumsum(tri, logf)
    yield
    ref_row = c // 2 - 1 if forward else c // 2
    last_row = c - 1 if forward else 0
    qe, ke, kd, qb, dec = [], [], [], [], []
    for n in range(nchunk):
        sl = slice(n * c, (n + 1) * c)
        bn, qn, kn = b[sl], q[sl], k[sl]
        bref = bn[ref_row:ref_row + 1]
        blast = bn[last_row:last_row + 1]
        qe.append(qn * jnp.exp(bn - bref))
        ke.append(kn * jnp.exp(bref - bn))
        kd.append((kn * jnp.exp(blast - bn)).astype(BF16))
        qb.append((qn * jnp.exp(bn)).astype(BF16))
        dec.append(jnp.exp(blast))
    qe = jnp.concatenate(qe, axis=0).astype(BF16)
    ke = jnp.concatenate(ke, axis=0).astype(BF16)
    vb = v.astype(BF16)
    a = lax.dot_general(qe, ke, NT_DIMS, preferred_element_type=F32)

    zero_blk = jnp.zeros((c, HG_DIM), BF16)

    def block_diag(blocks):
        return jnp.concatenate(
            [jnp.concatenate([blk if j == n else zero_blk for j in range(nchunk)], axis=1)
             for n, blk in enumerate(blocks)], axis=0)

    u_all = lax.dot_general(vb, block_diag(kd), TN_DIMS, preferred_element_type=F32)
    yield
    a = jnp.where(mask, a, 0.0).astype(BF16)
    o = jnp.dot(a, vb, preferred_element_type=F32)
    entering = [None] * nchunk
    order = range(nchunk) if forward else range(nchunk - 1, -1, -1)
    for n in order:
        entering[n] = state.astype(BF16)
        state = state * dec[n] + u_all[:, n * HG_DIM:(n + 1) * HG_DIM]
    o = o + lax.dot_general(block_diag(qb), jnp.concatenate(entering, axis=1), NT_DIMS,
                            preferred_element_type=F32)
    return o, state


def _hgrn_kernel(zq_ref, zff_ref, zfb_ref, zi_ref, zg_ref, lb_ref, gn_ref, o_ref, accf_ref, accb_ref, *, seq, rows):
    ngroups = seq // rows
    r = lax.broadcasted_iota(I32, (rows, rows), 0)
    cidx = lax.broadcasted_iota(I32, (rows, rows), 1)
    same = (r // HG_CHUNK) == (cidx // HG_CHUNK)
    mask_f = jnp.logical_and(same, cidx <= r)
    mask_b = jnp.logical_and(same, cidx >= r)
    tri_f = jnp.where(mask_f, 1.0, 0.0).astype(BF16)
    tri_b = jnp.where(mask_b, 1.0, 0.0).astype(BF16)
    lb_f = lb_ref[0:1, :]
    lb_b = lb_ref[1:2, :]
    zero_state = jnp.zeros((HG_DIM, HG_DIM), F32)

    def readout(sl, tot):
        zg = zg_ref[sl, :]
        y = _rms(tot, gn_ref[...]) * (zg * _sigmoid(zg))
        o_ref[sl, :] = y.astype(o_ref.dtype)

    def scan_body(it, states, meet):
        sf, sb = states
        slf = pl.ds(pl.multiple_of(it * rows, rows), rows)
        slb = pl.ds(pl.multiple_of((ngroups - 1 - it) * rows, rows), rows)
        (of, sf), (ob, sb) = _run_interleaved([
            _hgrn_group(zq_ref[slf, :], zff_ref[slf, :], zi_ref[slf, :], lb_f, sf, mask_f, tri_f, True),
            _hgrn_group(zq_ref[slb, :], zfb_ref[slb, :], zi_ref[slb, :], lb_b, sb, mask_b, tri_b, False)])
        if meet:
            readout(slf, of + accb_ref[slf, :])
            readout(slb, accf_ref[slb, :] + ob)
        else:
            accf_ref[slf, :] = of
            accb_ref[slb, :] = ob
        return sf, sb

    assert ngroups % 2 == 0, "the two scans must meet between two groups"
    states = lax.fori_loop(0, ngroups // 2, functools.partial(scan_body, meet=False), (zero_state, zero_state))
    lax.fori_loop(ngroups // 2, ngroups, functools.partial(scan_body, meet=True), states)


def _hgrn2(z, lb, hg_norm, batch, seq, rows):
    m = z.shape[0]

    def zspec(col):
        base = col // HG_DIM
        return pl.BlockSpec((seq, HG_DIM), lambda b, h: (b, base + h))

    return pl.pallas_call(
        functools.partial(_hgrn_kernel, seq=seq, rows=rows),
        grid=(batch, HG_HEADS),
        in_specs=[zspec(COL_Q_HG), zspec(COL_F_FWD), zspec(COL_F_BWD), zspec(COL_I_HG), zspec(COL_G_HG),
                  pl.BlockSpec((2, HG_DIM), lambda b, h: (0, h)),
                  pl.BlockSpec((1, HG_DIM), lambda b, h: (0, 0))],
        out_specs=pl.BlockSpec((seq, HG_DIM), lambda b, h: (b, h)),
        out_shape=jax.ShapeDtypeStruct((m, HG_WIDTH), BF16),
        scratch_shapes=[pltpu.VMEM((seq, HG_DIM), F32), pltpu.VMEM((seq, HG_DIM), F32)],
        compiler_params=_params("arbitrary", "arbitrary"),
        name="hgrn2",
    )(z, z, z, z, z, lb, hg_norm.reshape(1, HG_DIM))


def _out_projection_math(a_ref, hg_ref, x_ref, wa_ref, wh_ref):
    acc = jnp.dot(a_ref[...], wa_ref[...], preferred_element_type=F32)
    acc = acc + jnp.dot(hg_ref[...], wh_ref[...], preferred_element_type=F32)
    return x_ref[...] + acc


def _cross_attention_math(x, g_ref, kv_ref, wq_ref, wo_ref):
    h = _rms(x, g_ref[...]).astype(BF16)
    q = jnp.dot(h, wq_ref[...], preferred_element_type=F32).astype(BF16)
    scores = []
    for hd in range(X_HEADS):
        sl = slice(hd * X_HEAD_DIM, (hd + 1) * X_HEAD_DIM)
        scores.append(lax.dot_general(q[:, sl], kv_ref[:, sl], NT_DIMS, preferred_element_type=F32))
    outs = []
    for hd in range(X_HEADS):
        vh = kv_ref[:, X_WIDTH + hd * X_HEAD_DIM:X_WIDTH + (hd + 1) * X_HEAD_DIM]
        s = scores[hd] * (X_HEAD_DIM ** -0.5)
        p = jnp.exp(s - jnp.max(s, axis=-1, keepdims=True))
        denom = jnp.sum(p, axis=-1, keepdims=True)
        outs.append(jnp.dot(p.astype(BF16), vh, preferred_element_type=F32) / denom)
    oc = jnp.concatenate(outs, axis=-1).astype(BF16)
    return x + jnp.dot(oc, wo_ref[...], preferred_element_type=F32)


def _router_math(x, g_ref, wr_ref, br_ref, h_ref, idx_ref, gate_ref, rank_ref, cnt_ref, base_ref):
    tm = x.shape[0]
    h = _rms(x, g_ref[...])
    hh = h.astype(BF16)
    half = h.shape[1] // 2
    slabs = half // LANES
    bits = lax.bitcast_convert_type(hh.astype(F32), U32)
    packed = (bits[:, :half] >> 16) | (bits[:, half:] & jnp.uint32(0xFFFF0000))
    for j in range(slabs):
        h_ref[pl.ds(j, tm, stride=slabs), :] = packed[:, j * LANES:(j + 1) * LANES]
    hl = (h - hh.astype(F32)).astype(BF16)
    w = wr_ref[...]
    wh = w.astype(BF16)
    wl = (w - wh.astype(F32)).astype(BF16)
    nt = functools.partial(lax.dot_general, dimension_numbers=NT_DIMS, preferred_element_type=F32)
    logits = nt(wh, hh) + nt(wh, hl) + nt(wl, hh) + br_ref[...]

    eio = lax.broadcasted_iota(I32, (N_EXPERTS, tm), 0).astype(F32)
    work = logits
    vals, onehots = [], []
    for k in range(TOP_K):
        mx = jnp.max(work, axis=0, keepdims=True)
        ix = jnp.min(jnp.where(work == mx, eio, float(N_EXPERTS)), axis=0, keepdims=True)
        sel = eio == ix
        vals.append(mx)
        onehots.append(sel)
        idx_ref[k:k + 1, :] = ix.astype(I32)
        work = jnp.where(sel, -jnp.inf, work)
    ex = [jnp.exp(v - vals[0]) for v in vals]
    denom = ex[0] + ex[1] + ex[2] + ex[3]
    for k in range(TOP_K):
        gate_ref[k:k + 1, :] = ex[k] / denom

    oh = [jnp.where(s, 1.0, 0.0) for s in onehots]
    oh_all = oh[0] + oh[1] + oh[2] + oh[3]
    r = lax.broadcasted_iota(I32, (tm, tm), 0)
    c = lax.broadcasted_iota(I32, (tm, tm), 1)
    upper = jnp.where(r < c, 1.0, 0.0).astype(BF16)
    before = jnp.dot(oh_all.astype(BF16), upper, preferred_element_type=F32) + base_ref[...]
    for k in range(TOP_K):
        rank_ref[k:k + 1, :] = jnp.sum(oh[k] * before, axis=0, keepdims=True).astype(I32)
    base_ref[...] = base_ref[...] + jnp.sum(oh_all, axis=1, keepdims=True)
    cnt_ref[...] = base_ref[...]


def _mix_tail_kernel(a_ref, hg_ref, x_ref, wa_ref, wh_ref, gx_ref, kv_ref, wq_ref, wo_ref, gr_ref, wr_ref, br_ref,
                     cnt0_ref, x2_ref, h_ref, idx_ref, gate_ref, rank_ref, cnt_ref, base_ref):
    @pl.when(jnp.logical_and(pl.program_id(0) == 0, pl.program_id(1) == 0))
    def _():
        base_ref[...] = cnt0_ref[...]

    x1 = _out_projection_math(a_ref, hg_ref, x_ref, wa_ref, wh_ref)
    x2 = _cross_attention_math(x1, gx_ref, kv_ref, wq_ref, wo_ref)
    x2_ref[...] = x2
    _router_math(x2, gr_ref, wr_ref, br_ref, h_ref, idx_ref, gate_ref, rank_ref, cnt_ref, base_ref)


def _mix_tail(att, hg, x, w_out, g_cross, kv, w_xq, w_xo, g_ffn, w_router_t, b_router, cnt0, batch, seq, n_mem, tm):
    m, d = x.shape
    nt = seq // tm

    def tile(width):
        return pl.BlockSpec((tm, width), lambda b, i: (b * nt + i, 0))

    def resident(shape, row_block=0):
        return pl.BlockSpec(shape, lambda b, i: (row_block, 0), pipeline_mode=pl.Buffered(1))

    row4 = pl.BlockSpec((TOP_K, tm), lambda b, i: (0, b * nt + i))
    slabs = d // 2 // LANES
    return pl.pallas_call(
        _mix_tail_kernel,
        grid=(batch, nt),
        in_specs=[tile(ATT_WIDTH), tile(HG_WIDTH), tile(d),
                  resident((ATT_WIDTH, d)), resident((HG_WIDTH, d), ATT_WIDTH // HG_WIDTH),
                  resident((1, d)),
                  pl.BlockSpec((n_mem, 2 * X_WIDTH), lambda b, i: (b, 0)),
                  resident((d, X_WIDTH)), resident((X_WIDTH, d)),
                  resident((1, d)), resident((N_EXPERTS, d)), resident((N_EXPERTS, 1)),
                  resident((N_EXPERTS, 1))],
        out_specs=[tile(d), pl.BlockSpec((tm * slabs, LANES), lambda b, i: (b * nt + i, 0)), row4, row4, row4,
                   pl.BlockSpec((N_EXPERTS, 1), lambda b, i: (0, 0))],
        out_shape=[jax.ShapeDtypeStruct((m, d), F32),
                   jax.ShapeDtypeStruct((m * slabs, LANES), U32),
                   jax.ShapeDtypeStruct((TOP_K, m), I32),
                   jax.ShapeDtypeStruct((TOP_K, m), F32),
                   jax.ShapeDtypeStruct((TOP_K, m), I32),
                   jax.ShapeDtypeStruct((N_EXPERTS, 1), F32)],
        scratch_shapes=[pltpu.VMEM((N_EXPERTS, 1), F32)],
        compiler_params=_params("arbitrary", "arbitrary"),
        name="mix_tail",
    )(att, hg, x, w_out, w_out, g_cross.reshape(1, d), kv, w_xq, w_xo, g_ffn.reshape(1, d), w_router_t,
      b_router.reshape(N_EXPERTS, 1), cnt0)


def _assign_rows_kernel(pstart_ref, idx_ref, rank_ref, dest_ref):
    idx = idx_ref[...]
    start = jnp.zeros_like(idx)
    for e in range(N_EXPERTS):
        start = jnp.where(idx == e, pstart_ref[e], start)
    dest_ref[...] = start + rank_ref[...]


def _assign_rows(pstart, idx, rank):
    full = pl.BlockSpec(idx.shape, lambda: (0, 0))
    return pl.pallas_call(
        _assign_rows_kernel,
        in_specs=[pl.BlockSpec(memory_space=pltpu.SMEM), full, full],
        out_specs=full,
        out_shape=jax.ShapeDtypeStruct(idx.shape, I32),
        name="assign_rows",
    )(pstart, idx, rank)


def _dispatch_kernel(seg_ref, dest_hbm, h_ref, xs_prev_hbm, xs_hbm, dsm, zrow, sem_idx, sem_row,
                     *, m_tokens, p_rows, fill_padding):
    del xs_prev_hbm
    i = pl.program_id(0)
    tt = SMEM_STAGE
    slabs = h_ref.shape[0] // tt

    def token_rows(ref, tok):
        return ref.at[pl.ds(pl.multiple_of(tok * slabs, slabs), slabs)]

    def idx_copy(k):
        return pltpu.make_async_copy(dest_hbm.at[pl.ds(k * m_tokens + i * tt, tt)],
                                     dsm.at[pl.ds(k * tt, tt)], sem_idx)

    for k in range(TOP_K):
        idx_copy(k).start()
    for k in range(TOP_K):
        idx_copy(k).wait()

    def row_copy(t, dst_row):
        return pltpu.make_async_copy(token_rows(h_ref, t), token_rows(xs_hbm, dst_row), sem_row)

    def issue(t, carry):
        for k in range(TOP_K):
            row_copy(t, dsm[k * tt + t]).start(priority=k % 2)
        return carry

    lax.fori_loop(0, tt, issue, 0, unroll=8)

    for k in range(TOP_K):
        pltpu.make_async_copy(h_ref, xs_hbm.at[pl.ds(0, tt * slabs)], sem_row).wait()

    @pl.when(jnp.logical_and(i == 0, jnp.bool_(fill_padding)))
    def _():
        zrow[...] = jnp.zeros_like(zrow)
        max_chunk = zrow.shape[0] // slabs

        def zero_copy(dst_row, nrows):
            return pltpu.make_async_copy(
                zrow.at[pl.ds(0, nrows * slabs)],
                xs_hbm.at[pl.ds(pl.multiple_of(dst_row * slabs, slabs), nrows * slabs)], sem_row)

        def zero_rows(dst_row, nrows):
            zero_copy(dst_row, nrows).start()
            zero_copy(dst_row, nrows).wait()

        def per_expert(e, carry):
            lo = seg_ref[0, e] + seg_ref[1, e]
            pad = seg_ref[2, e] - lo
            chunk = max_chunk
            while chunk >= 1:
                take = pad & chunk

                @pl.when(take != 0)
                def _():
                    zero_rows(lo, chunk)

                lo = lo + take
                chunk //= 2
            return carry

        lax.fori_loop(0, N_EXPERTS, per_expert, 0)

        tail_lo = seg_ref[2, N_EXPERTS - 1] // max_chunk
        tail_hi = p_rows // max_chunk

        def tail_start(c, carry):
            zero_copy(c * max_chunk, max_chunk).start()
            return carry

        def tail_wait(c, carry):
            zero_copy(c * max_chunk, max_chunk).wait()
            return carry

        lax.fori_loop(tail_lo, tail_hi, tail_start, 0)
        lax.fori_loop(tail_lo, tail_hi, tail_wait, 0)


def _dispatch(seg, dest_flat, h, xs_prev, m, p_rows, fill_padding):
    slabs = h.shape[0] // m
    return pl.pallas_call(
        functools.partial(_dispatch_kernel, m_tokens=m, p_rows=p_rows, fill_padding=fill_padding),
        grid=(m // SMEM_STAGE,),
        in_specs=[pl.BlockSpec(memory_space=pltpu.SMEM),
                  pl.BlockSpec(memory_space=pl.ANY),
                  pl.BlockSpec((SMEM_STAGE * slabs, LANES), lambda i: (i, 0)),
                  pl.BlockSpec(memory_space=pl.ANY)],
        out_specs=pl.BlockSpec(memory_space=pl.ANY),
        out_shape=jax.ShapeDtypeStruct((p_rows * slabs, LANES), h.dtype),
        input_output_aliases={3: 0},
        scratch_shapes=[pltpu.SMEM((TOP_K * SMEM_STAGE,), I32),
                        pltpu.VMEM((MOE_TILE // 2 * slabs, LANES), h.dtype),
                        pltpu.SemaphoreType.DMA,
                        pltpu.SemaphoreType.DMA],
        compiler_params=_params("arbitrary"),
        name="dispatch",
    )(seg, dest_flat, h, xs_prev)


def _expert_kernel(te_ref, tv_ref, nu_ref, x_ref, w1g_ref, w1l_ref, b1g_ref, b1l_ref, w2_ref, b2_ref,
                   o_ref, xb_ref):
    i = pl.program_id(0)
    f = pl.program_id(1)
    nf = pl.num_programs(1)
    live = tv_ref[i] > 0

    @pl.when(live)
    def _():
        @pl.when(f == 0)
        def _():
            o_ref[...] = jnp.broadcast_to(b2_ref[0], o_ref.shape)

        tmb, d = xb_ref.shape
        half = d // 2
        slabs = half // LANES
        for j in range(slabs):
            words = x_ref[pl.ds(j, tmb, stride=slabs), :]
            lo = slice(j * LANES, (j + 1) * LANES)
            hi = slice(half + j * LANES, half + (j + 1) * LANES)
            xb_ref[:, lo] = lax.bitcast_convert_type(words << 16, F32).astype(BF16)
            xb_ref[:, hi] = lax.bitcast_convert_type(words & jnp.uint32(0xFFFF0000), F32).astype(BF16)

        xb = xb_ref[...]
        glu = jnp.dot(xb, w1g_ref[0], preferred_element_type=F32) + b1g_ref[0]
        lin = jnp.dot(xb, w1l_ref[0], preferred_element_type=F32) + b1l_ref[0]
        glu = jnp.minimum(glu, SWIGLU_LIMIT)
        lin = jnp.clip(lin, -SWIGLU_LIMIT, SWIGLU_LIMIT)
        act = glu * _sigmoid(SWIGLU_ALPHA * glu) * (lin + 1.0)
        o_ref[...] += jnp.dot(act.astype(BF16), w2_ref[0], preferred_element_type=F32)

    @pl.when(jnp.logical_and(jnp.logical_not(live), f == nf - 1))
    def _():
        o_ref[...] = jnp.zeros_like(o_ref)


def _expert_mlp(tile_e, tile_valid, n_used, xs, w1, b1, w2, b2, tmb, tf):
    d = w1.shape[1]
    p_rows = xs.shape[0] // (d // 2 // LANES)
    d_ff = w2.shape[1]
    nf = d_ff // tf
    n_tiles = p_rows // tmb

    def fsel(i, f, tv):
        return jnp.where(tv[i] > 0, f, nf - 1)

    grid_spec = pltpu.PrefetchScalarGridSpec(
        num_scalar_prefetch=3,
        grid=(n_tiles, nf),
        in_specs=[
            pl.BlockSpec((tmb * (d // 2 // LANES), LANES), lambda i, f, te, tv, nu: (jnp.minimum(i, nu[0] - 1), 0)),
            pl.BlockSpec((1, d, tf), lambda i, f, te, tv, nu: (te[i], 0, fsel(i, f, tv))),
            pl.BlockSpec((1, d, tf), lambda i, f, te, tv, nu: (te[i], 0, nf + fsel(i, f, tv))),
            pl.BlockSpec((1, 1, tf), lambda i, f, te, tv, nu: (te[i], 0, fsel(i, f, tv))),
            pl.BlockSpec((1, 1, tf), lambda i, f, te, tv, nu: (te[i], 0, nf + fsel(i, f, tv))),
            pl.BlockSpec((1, tf, d), lambda i, f, te, tv, nu: (te[i], fsel(i, f, tv), 0)),
            pl.BlockSpec((1, 1, d), lambda i, f, te, tv, nu: (te[i], 0, 0)),
        ],
        out_specs=pl.BlockSpec((tmb, d), lambda i, f, te, tv, nu: (i, 0)),
        scratch_shapes=[pltpu.VMEM((tmb, d), BF16)],
    )
    return pl.pallas_call(
        _expert_kernel,
        grid_spec=grid_spec,
        out_shape=jax.ShapeDtypeStruct((p_rows, d), F32),
        compiler_params=_params("arbitrary", "arbitrary"),
        name="expert_mlp",
    )(tile_e, tile_valid, n_used, xs, w1, w1, b1, b1, w2, b2)


def _combine_kernel(dest_hbm, ys_hbm, x_ref, gate_ref, g_ref, o_ref, dsm, buf, sem_idx, sem_row, *, m_tokens, tc):
    i = pl.program_id(0)
    n = pl.num_programs(0)
    per_stage = SMEM_STAGE // tc

    def issue_tile(tile):
        slot = tile % 2
        off = (tile % per_stage) * tc

        @pl.when(tile % per_stage == 0)
        def _():
            base = (tile // per_stage) * SMEM_STAGE

            def idx_copy(k):
                return pltpu.make_async_copy(dest_hbm.at[pl.ds(k * m_tokens + base, SMEM_STAGE)],
                                             dsm.at[pl.ds(k * SMEM_STAGE, SMEM_STAGE)], sem_idx)

            for k in range(TOP_K):
                idx_copy(k).start()
            for k in range(TOP_K):
                idx_copy(k).wait()

        def issue(t, carry):
            for k in range(TOP_K):
                pltpu.make_async_copy(ys_hbm.at[pl.ds(dsm[k * SMEM_STAGE + off + t], 1)],
                                      buf.at[slot, k, pl.ds(t, 1)], sem_row.at[slot]).start(priority=k % 2)
            return carry

        lax.fori_loop(0, tc, issue, 0, unroll=8)

    @pl.when(i == 0)
    def _():
        issue_tile(i)

    @pl.when(i + 1 < n)
    def _():
        issue_tile(i + 1)

    slot = i % 2
    for k in range(TOP_K):
        pltpu.make_async_copy(ys_hbm.at[pl.ds(0, tc)], buf.at[slot, k], sem_row.at[slot]).wait()

    gate = gate_ref[...]
    y = x_ref[...]
    for k in range(TOP_K):
        y = y + buf[slot, k] * gate[:, k:k + 1]
    o_ref[...] = _rms(y, g_ref[...])


def _combine(dest_flat, ys, x, gate_t, g, tc):
    m, d = x.shape
    return pl.pallas_call(
        functools.partial(_combine_kernel, m_tokens=m, tc=tc),
        grid=(m // tc,),
        in_specs=[pl.BlockSpec(memory_space=pl.ANY),
                  pl.BlockSpec(memory_space=pl.ANY),
                  pl.BlockSpec((tc, d), lambda i: (i, 0)),
                  pl.BlockSpec((tc, TOP_K), lambda i: (i, 0)),
                  pl.BlockSpec((1, d), lambda i: (0, 0))],
        out_specs=pl.BlockSpec((tc, d), lambda i: (i, 0)),
        out_shape=jax.ShapeDtypeStruct((m, d), F32),
        scratch_shapes=[pltpu.SMEM((TOP_K * SMEM_STAGE,), I32),
                        pltpu.VMEM((2, TOP_K, tc, d), F32),
                        pltpu.SemaphoreType.DMA,
                        pltpu.SemaphoreType.DMA((2,))],
        compiler_params=_params("arbitrary"),
        name="combine",
    )(dest_flat, ys, x, gate_t, g.reshape(1, d))


def _rope_tables(seq):
    half = ATT_HEAD_DIM // 2
    inv = ROPE_THETA ** (-jnp.arange(0, ATT_HEAD_DIM, 2, dtype=F32) / ATT_HEAD_DIM)
    ang = jnp.arange(seq, dtype=F32)[:, None] * inv[None, :]
    cos = jnp.tile(jnp.cos(ang), (1, LANES // half))
    sin = jnp.sin(ang)
    sin_signed = jnp.tile(jnp.concatenate([-sin, sin], axis=-1), (1, LANES // ATT_HEAD_DIM))
    return cos, sin_signed


def _pick(n, prefs):
    for p in prefs:
        if n % p == 0:
            return p
    return n


def _uninitialized_kernel(o_ref):
    del o_ref


def _uninitialized(shape, dtype):
    return pl.pallas_call(
        _uninitialized_kernel,
        out_specs=pl.BlockSpec(memory_space=pl.ANY),
        out_shape=jax.ShapeDtypeStruct(shape, dtype),
        name="uninitialized",
    )()


def _trunk_front(x, mem, wts, cnt0):
    batch, seq, d = x.shape
    n_mem = mem.shape[1]
    m = batch * seq
    x2d = x.reshape(m, d)

    n_in = wts["w_in"].shape[1]
    z = _norm_matmul(x2d, wts["norm_mix"], wts["w_in"], _pick(m, (1024, 512, 256, 128)),
                     _pick(n_in, (1664, 1024, 512, 256, 128)), F32, "in_projection")

    cos, sin_signed = _rope_tables(seq)
    att = _window_attention(z, wts["att_sink"], cos, sin_signed, batch, seq, _pick(seq, (256, 128)))
    hg = _hgrn2(z, wts["lb"], wts["hg_norm"], batch, seq, _pick(seq, (256, 128, 64)))
    kv = _norm_matmul(mem.reshape(batch * n_mem, d), wts["norm_mem"], wts["w_xkv"], n_mem, 2 * X_WIDTH, BF16,
                      "memory_kv")
    return _mix_tail(att, hg, x2d, wts["w_out"], wts["norm_cross"], kv, wts["w_xq"], wts["w_xo"], wts["norm_ffn"],
                     wts["w_router_t"], wts["b_router"], cnt0, batch, seq, n_mem, _pick(seq, (512, 256, 128)))


def _moe_and_final_norm(fronts, shapes, wts, moe_tile, ff_tile):
    counts = fronts[-1][5][:, 0].astype(I32)
    padded = (counts + moe_tile - 1) // moe_tile * moe_tile
    pend = jnp.cumsum(padded)
    pstart = pend - padded
    m_total = sum(f[0].shape[0] for f in fronts)
    n_tiles = -(-(m_total * TOP_K) // moe_tile) + N_EXPERTS
    p_rows = n_tiles * moe_tile
    tile_row = jnp.arange(n_tiles, dtype=I32) * moe_tile
    tile_e = jnp.minimum(jnp.sum((pend[None, :] <= tile_row[:, None]).astype(I32), axis=1), N_EXPERTS - 1)
    owner = tile_e[:, None] == jnp.arange(N_EXPERTS, dtype=I32)[None, :]
    seg_end = jnp.sum(jnp.where(owner, (pstart + counts)[None, :], 0), axis=1)
    tile_valid = jnp.clip(seg_end - tile_row, 0, moe_tile).astype(I32)
    n_used = (pend[-1:] // moe_tile).astype(I32)
    seg = jnp.stack([pstart, counts, pend]).astype(I32)

    h_rows, lanes = fronts[0][1].shape
    slabs = h_rows // fronts[0][0].shape[0]
    xs = _uninitialized((p_rows * slabs, lanes), fronts[0][1].dtype)
    dests = []
    for n, (x2, h3, idx, gate, rank, cnt) in enumerate(fronts):
        dest_flat = _assign_rows(pstart.astype(I32), idx, rank).reshape(-1)
        dests.append(dest_flat)
        xs = _dispatch(seg, dest_flat, h3, xs, x2.shape[0], p_rows, fill_padding=(n == 0))
    ys = _expert_mlp(tile_e, tile_valid, n_used, xs, wts["w_moe1"], wts["b_moe1"], wts["w_moe2"], wts["b_moe2"],
                     moe_tile, ff_tile)
    outs = []
    for (x2, h3, idx, gate, rank, cnt), dest_flat, shape in zip(fronts, dests, shapes):
        y = _combine(dest_flat, ys, x2, gate.T, wts["norm_final"], _pick(x2.shape[0], (256, 128)))
        outs.append(y.reshape(shape))
    return tuple(outs)


def kernel(x_prompt, x_sample, mem_prompt, mem_sample, norm_mix, w_in, att_sink, hg_lb_logits, hg_norm, w_out,
           norm_cross, norm_mem, w_xq, w_xkv, w_xo, norm_ffn, w_router, b_router, w_moe1, b_moe1, w_moe2, b_moe2,
           norm_final):
    assert w_in.shape[0] == 1, "the final norm is fused after the single layer"
    lb_all = jnp.cumsum(jax.nn.softmax(hg_lb_logits.astype(F32), axis=0), axis=0)
    d_ff = w_moe2.shape[2]
    wts = dict(
        norm_mix=norm_mix[0], w_in=w_in[0].astype(BF16), att_sink=att_sink[0], lb=lb_all[0],
        hg_norm=hg_norm[0], w_out=w_out[0].astype(BF16), norm_cross=norm_cross[0], norm_mem=norm_mem[0],
        w_xq=w_xq[0].astype(BF16), w_xkv=w_xkv[0].astype(BF16), w_xo=w_xo[0].astype(BF16),
        norm_ffn=norm_ffn[0], w_router_t=w_router[0].T, b_router=b_router[0],
        w_moe1=w_moe1[0].astype(BF16), b_moe1=b_moe1[0][:, None, :],
        w_moe2=w_moe2[0].astype(BF16), b_moe2=b_moe2[0][:, None, :],
        norm_final=norm_final,
    )
    ff_tile = _pick(d_ff, (1024, 512, 256, 128))
    fronts = []
    cnt = jnp.zeros((N_EXPERTS, 1), F32)
    for x, mem in ((x_prompt, mem_prompt), (x_sample, mem_sample)):
        front = _trunk_front(x, mem, wts, cnt)
        cnt = front[5]
        fronts.append(front)
    return _moe_and_final_norm(fronts, (x_prompt.shape, x_sample.shape), wts, MOE_TILE, ff_tile)
```

```python
import functools

import jax
import jax.numpy as jnp
from jax import lax
from jax.experimental import pallas as pl
from jax.experimental.pallas import tpu as pltpu

F32 = jnp.float32
BF16 = jnp.bfloat16
I32 = jnp.int32
U32 = jnp.uint32

EPS = 1e-5
ATT_HEADS = 16
ATT_KV_HEADS = 4
ATT_HEAD_DIM = 64
ATT_WIDTH = ATT_HEADS * ATT_HEAD_DIM
KV_WIDTH = ATT_KV_HEADS * ATT_HEAD_DIM
WINDOW = 128
ROPE_THETA = 10000.0
HG_HEADS = 8
HG_DIM = 128
HG_WIDTH = HG_HEADS * HG_DIM
HG_CHUNK = 64
X_HEADS = 4
X_HEAD_DIM = 128
X_WIDTH = X_HEADS * X_HEAD_DIM
N_EXPERTS = 32
TOP_K = 4
SWIGLU_ALPHA = 1.702
SWIGLU_LIMIT = 7.0

COL_Q_ATT = 0
COL_K_ATT = ATT_WIDTH
COL_V_ATT = ATT_WIDTH + KV_WIDTH
COL_Q_HG = ATT_WIDTH + 2 * KV_WIDTH
COL_F_FWD = COL_Q_HG + HG_WIDTH
COL_F_BWD = COL_F_FWD + HG_WIDTH
COL_I_HG = COL_F_BWD + HG_WIDTH
COL_G_HG = COL_I_HG + HG_WIDTH

LANES = 128
SUBLANES = 8
VMEM_LIMIT = 56 * 1024 * 1024
SMEM_STAGE = 1024
MOE_TILE = 512

NT_DIMS = (((1,), (1,)), ((), ()))
TN_DIMS = (((0,), (0,)), ((), ()))


def _params(*sem):
    return pltpu.CompilerParams(dimension_semantics=sem, vmem_limit_bytes=VMEM_LIMIT)


def _rms(x, g):
    ms = jnp.mean(x * x, axis=-1, keepdims=True)
    return x * lax.rsqrt(ms + EPS) * g


def _sigmoid(x):
    return 1.0 / (1.0 + jnp.exp(-x))


def _norm_matmul_kernel(x_ref, g_ref, w_ref, o_ref, xn_ref):
    @pl.when(pl.program_id(1) == 0)
    def _():
        xn_ref[...] = _rms(x_ref[...], g_ref[...]).astype(BF16)

    o_ref[...] = jnp.dot(xn_ref[...], w_ref[...], preferred_element_type=F32).astype(o_ref.dtype)


def _norm_matmul(x, g, w, tm, tn, out_dtype, name):
    m, d = x.shape
    n = w.shape[1]
    return pl.pallas_call(
        _norm_matmul_kernel,
        grid=(m // tm, n // tn),
        in_specs=[pl.BlockSpec((tm, d), lambda i, j: (i, 0)),
                  pl.BlockSpec((1, d), lambda i, j: (0, 0)),
                  pl.BlockSpec((d, tn), lambda i, j: (0, j))],
        out_specs=pl.BlockSpec((tm, tn), lambda i, j: (i, j)),
        out_shape=jax.ShapeDtypeStruct((m, n), out_dtype),
        scratch_shapes=[pltpu.VMEM((tm, d), BF16)],
        compiler_params=_params("arbitrary", "arbitrary"),
        name=name,
    )(x, g.reshape(1, d), w)


def _rope_pair(x, cos, sin_signed):
    lane = lax.broadcasted_iota(I32, (1, LANES), 1)
    first_half = (lane % ATT_HEAD_DIM) < (ATT_HEAD_DIM // 2)
    rot = jnp.where(first_half, pltpu.roll(x, LANES - ATT_HEAD_DIM // 2, 1), pltpu.roll(x, ATT_HEAD_DIM // 2, 1))
    return x * cos + rot * sin_signed


def _attn_kernel(sink_ref, q_ref, k_ref, v_ref, cq_ref, sq_ref, ck_ref, sk_ref, o_ref, klo_ref, khi_ref, vt_ref,
                 *, seq, tq):
    qi = pl.program_id(1)
    win = tq + 2 * WINDOW
    half_lanes = lax.broadcasted_iota(I32, (1, LANES), 1) < ATT_HEAD_DIM
    heads_per_slab = LANES // ATT_HEAD_DIM

    @pl.when(qi == 0)
    def _():
        for j in range(KV_WIDTH // LANES):
            slab = _rope_pair(k_ref[:, j * LANES:(j + 1) * LANES], ck_ref[...], sk_ref[...])
            swapped = pltpu.roll(slab, ATT_HEAD_DIM, 1)
            g0 = j * heads_per_slab
            g1 = g0 + 1
            klo_ref[:, g0 * LANES:(g0 + 1) * LANES] = jnp.where(half_lanes, slab, 0.0).astype(BF16)
            khi_ref[:, g0 * LANES:(g0 + 1) * LANES] = jnp.where(half_lanes, 0.0, swapped).astype(BF16)
            klo_ref[:, g1 * LANES:(g1 + 1) * LANES] = jnp.where(half_lanes, swapped, 0.0).astype(BF16)
            khi_ref[:, g1 * LANES:(g1 + 1) * LANES] = jnp.where(half_lanes, 0.0, slab).astype(BF16)
            vt_ref[j * LANES:(j + 1) * LANES, :] = v_ref[:, j * LANES:(j + 1) * LANES].T.astype(BF16)

    q0 = qi * tq
    ks = pl.multiple_of(jnp.clip(q0 - WINDOW, 0, seq - win), WINDOW)
    kpos = ks + lax.broadcasted_iota(I32, (win, 1), 0)
    qpos = q0 + lax.broadcasted_iota(I32, (1, tq), 1)
    valid = jnp.abs(kpos - qpos) <= WINDOW
    cq = cq_ref[...]
    sq = sq_ref[...]
    group = ATT_HEADS // ATT_KV_HEADS
    scores = []
    for j in range(ATT_WIDTH // LANES):
        qs = (_rope_pair(q_ref[:, j * LANES:(j + 1) * LANES], cq, sq) * (ATT_HEAD_DIM ** -0.5)).astype(BF16)
        for hh in range(heads_per_slab):
            g = (j * heads_per_slab + hh) // group
            k_ref_sel = klo_ref if hh == 0 else khi_ref
            ksel = k_ref_sel[pl.ds(ks, win), g * LANES:(g + 1) * LANES]
            scores.append(lax.dot_general(ksel, qs, NT_DIMS, preferred_element_type=F32))
    for j in range(ATT_WIDTH // LANES):
        outs = []
        for hh in range(heads_per_slab):
            h = j * heads_per_slab + hh
            g = h // group
            st = jnp.where(valid, scores[h], -1e30)
            sk = sink_ref[h]
            m = jnp.maximum(jnp.max(st, axis=0, keepdims=True), sk)
            p = jnp.exp(st - m)
            denom = jnp.sum(p, axis=0, keepdims=True) + jnp.exp(sk - m)
            vt = vt_ref[g * ATT_HEAD_DIM:(g + 1) * ATT_HEAD_DIM, pl.ds(ks, win)]
            ot = jnp.dot(vt, p.astype(BF16), preferred_element_type=F32)
            outs.append(ot / denom)
        o_ref[:, j * LANES:(j + 1) * LANES] = jnp.concatenate(outs, axis=0).T.astype(o_ref.dtype)


def _window_attention(z, sink, cos, sin_signed, batch, seq, tq):
    m = z.shape[0]
    nq = seq // tq
    qblk = COL_Q_ATT // ATT_WIDTH
    kblk = COL_K_ATT // KV_WIDTH
    vblk = COL_V_ATT // KV_WIDTH
    return pl.pallas_call(
        functools.partial(_attn_kernel, seq=seq, tq=tq),
        grid=(batch, nq),
        in_specs=[pl.BlockSpec(memory_space=pltpu.SMEM),
                  pl.BlockSpec((tq, ATT_WIDTH), lambda b, i: (b * nq + i, qblk)),
                  pl.BlockSpec((seq, KV_WIDTH), lambda b, i: (b, kblk)),
                  pl.BlockSpec((seq, KV_WIDTH), lambda b, i: (b, vblk)),
                  pl.BlockSpec((tq, LANES), lambda b, i: (i, 0)),
                  pl.BlockSpec((tq, LANES), lambda b, i: (i, 0)),
                  pl.BlockSpec((seq, LANES), lambda b, i: (0, 0)),
                  pl.BlockSpec((seq, LANES), lambda b, i: (0, 0))],
        out_specs=pl.BlockSpec((tq, ATT_WIDTH), lambda b, i: (b * nq + i, 0)),
        out_shape=jax.ShapeDtypeStruct((m, ATT_WIDTH), BF16),
        scratch_shapes=[pltpu.VMEM((seq, ATT_KV_HEADS * LANES), BF16),
                        pltpu.VMEM((seq, ATT_KV_HEADS * LANES), BF16),
                        pltpu.VMEM((KV_WIDTH, seq), BF16)],
        compiler_params=_params("arbitrary", "arbitrary"),
        name="window_attention",
    )(sink, z, z, z, cos, sin_signed, cos, sin_signed)


def _split_cumsum(tri, x):
    hi = x.astype(BF16)
    r1 = x - hi.astype(F32)
    mid = r1.astype(BF16)
    lo = (r1 - mid.astype(F32)).astype(BF16)
    dot = functools.partial(jnp.dot, preferred_element_type=F32)
    return dot(tri, hi) + dot(tri, mid) + dot(tri, lo)


def _run_interleaved(stage_generators):
    results = [None] * len(stage_generators)
    live = list(range(len(stage_generators)))
    while live:
        for i in list(live):
            try:
                next(stage_generators[i])
            except StopIteration as done:
                results[i] = done.value
                live.remove(i)
    return results


def _hgrn_group(zq, zf, v, lbv, state, mask, tri, forward):
    rows = zq.shape[0]
    c = HG_CHUNK
    nchunk = rows // c
    q = zq * _sigmoid(zq) * (HG_DIM ** -0.5)
    f = lbv + (1.0 - lbv) * _sigmoid(zf)
    logf = jnp.log(f)
    k = 1.0 - f
    yield
    b = _split_cumsum(tri, logf)
    yield
    ref_row = c // 2 - 1 if forward else c // 2
    last_row = c - 1 if forward else 0
    qe, ke, kd, qb, dec = [], [], [], [], []
    for n in range(nchunk):
        sl = slice(n * c, (n + 1) * c)
        bn, qn, kn = b[sl], q[sl], k[sl]
        bref = bn[ref_row:ref_row + 1]
        blast = bn[last_row:last_row + 1]
        qe.append(qn * jnp.exp(bn - bref))
        ke.append(kn * jnp.exp(bref - bn))
        kd.append((kn * jnp.exp(blast - bn)).astype(BF16))
        qb.append((qn * jnp.exp(bn)).astype(BF16))
        dec.append(jnp.exp(blast))
    qe = jnp.concatenate(qe, axis=0).astype(BF16)
    ke = jnp.concatenate(ke, axis=0).astype(BF16)
    vb = v.astype(BF16)
    a = lax.dot_general(qe, ke, NT_DIMS, preferred_element_type=F32)

    zero_blk = jnp.zeros((c, HG_DIM), BF16)

    def block_diag(blocks):
        return jnp.concatenate(
            [jnp.concatenate([blk if j == n else zero_blk for j in range(nchunk)], axis=1)
             for n, blk in enumerate(blocks)], axis=0)

    u_all = lax.dot_general(vb, block_diag(kd), TN_DIMS, preferred_element_type=F32)
    yield
    a = jnp.where(mask, a, 0.0).astype(BF16)
    o = jnp.dot(a, vb, preferred_element_type=F32)
    entering = [None] * nchunk
    order = range(nchunk) if forward else range(nchunk - 1, -1, -1)
    for n in order:
        entering[n] = state.astype(BF16)
        state = state * dec[n] + u_all[:, n * HG_DIM:(n + 1) * HG_DIM]
    o = o + lax.dot_general(block_diag(qb), jnp.concatenate(entering, axis=1), NT_DIMS,
                            preferred_element_type=F32)
    return o, state


def _hgrn_kernel(zq_ref, zff_ref, zfb_ref, zi_ref, zg_ref, lb_ref, gn_ref, o_ref, accf_ref, accb_ref, *, seq, rows):
    ngroups = seq // rows
    r = lax.broadcasted_iota(I32, (rows, rows), 0)
    cidx = lax.broadcasted_iota(I32, (rows, rows), 1)
    same = (r // HG_CHUNK) == (cidx // HG_CHUNK)
    mask_f = jnp.logical_and(same, cidx <= r)
    mask_b = jnp.logical_and(same, cidx >= r)
    tri_f = jnp.where(mask_f, 1.0, 0.0).astype(BF16)
    tri_b = jnp.where(mask_b, 1.0, 0.0).astype(BF16)
    lb_f = lb_ref[0:1, :]
    lb_b = lb_ref[1:2, :]
    zero_state = jnp.zeros((HG_DIM, HG_DIM), F32)

    def readout(sl, tot):
        zg = zg_ref[sl, :]
        y = _rms(tot, gn_ref[...]) * (zg * _sigmoid(zg))
        o_ref[sl, :] = y.astype(o_ref.dtype)

    def scan_body(it, states, meet):
        sf, sb = states
        slf = pl.ds(pl.multiple_of(it * rows, rows), rows)
        slb = pl.ds(pl.multiple_of((ngroups - 1 - it) * rows, rows), rows)
        (of, sf), (ob, sb) = _run_interleaved([
            _hgrn_group(zq_ref[slf, :], zff_ref[slf, :], zi_ref[slf, :], lb_f, sf, mask_f, tri_f, True),
            _hgrn_group(zq_ref[slb, :], zfb_ref[slb, :], zi_ref[slb, :], lb_b, sb, mask_b, tri_b, False)])
        if meet:
            readout(slf, of + accb_ref[slf, :])
            readout(slb, accf_ref[slb, :] + ob)
        else:
            accf_ref[slf, :] = of
            accb_ref[slb, :] = ob
        return sf, sb

    assert ngroups % 2 == 0, "the two scans must meet between two groups"
    states = lax.fori_loop(0, ngroups // 2, functools.partial(scan_body, meet=False), (zero_state, zero_state))
    lax.fori_loop(ngroups // 2, ngroups, functools.partial(scan_body, meet=True), states)


def _hgrn2(z, lb, hg_norm, batch, seq, rows):
    m = z.shape[0]

    def zspec(col):
        base = col // HG_DIM
        return pl.BlockSpec((seq, HG_DIM), lambda b, h: (b, base + h))

    return pl.pallas_call(
        functools.partial(_hgrn_kernel, seq=seq, rows=rows),
        grid=(batch, HG_HEADS),
        in_specs=[zspec(COL_Q_HG), zspec(COL_F_FWD), zspec(COL_F_BWD), zspec(COL_I_HG), zspec(COL_G_HG),
                  pl.BlockSpec((2, HG_DIM), lambda b, h: (0, h)),
                  pl.BlockSpec((1, HG_DIM), lambda b, h: (0, 0))],
        out_specs=pl.BlockSpec((seq, HG_DIM), lambda b, h: (b, h)),
        out_shape=jax.ShapeDtypeStruct((m, HG_WIDTH), BF16),
        scratch_shapes=[pltpu.VMEM((seq, HG_DIM), F32), pltpu.VMEM((seq, HG_DIM), F32)],
        compiler_params=_params("arbitrary", "arbitrary"),
        name="hgrn2",
    )(z, z, z, z, z, lb, hg_norm.reshape(1, HG_DIM))


def _out_projection_math(a_ref, hg_ref, x_ref, wa_ref, wh_ref):
    acc = jnp.dot(a_ref[...], wa_ref[...], preferred_element_type=F32)
    acc = acc + jnp.dot(hg_ref[...], wh_ref[...], preferred_element_type=F32)
    return x_ref[...] + acc


def _cross_attention_math(x, g_ref, kv_ref, wq_ref, wo_ref):
    h = _rms(x, g_ref[...]).astype(BF16)
    q = jnp.dot(h, wq_ref[...], preferred_element_type=F32).astype(BF16)
    scores = []
    for hd in range(X_HEADS):
        sl = slice(hd * X_HEAD_DIM, (hd + 1) * X_HEAD_DIM)
        scores.append(lax.dot_general(q[:, sl], kv_ref[:, sl], NT_DIMS, preferred_element_type=F32))
    outs = []
    for hd in range(X_HEADS):
        vh = kv_ref[:, X_WIDTH + hd * X_HEAD_DIM:X_WIDTH + (hd + 1) * X_HEAD_DIM]
        s = scores[hd] * (X_HEAD_DIM ** -0.5)
        p = jnp.exp(s - jnp.max(s, axis=-1, keepdims=True))
        denom = jnp.sum(p, axis=-1, keepdims=True)
        outs.append(jnp.dot(p.astype(BF16), vh, preferred_element_type=F32) / denom)
    oc = jnp.concatenate(outs, axis=-1).astype(BF16)
    return x + jnp.dot(oc, wo_ref[...], preferred_element_type=F32)


def _router_math(x, g_ref, wr_ref, br_ref, h_ref, idx_ref, gate_ref, rank_ref, cnt_ref, base_ref):
    tm = x.shape[0]
    h = _rms(x, g_ref[...])
    hh = h.astype(BF16)
    half = h.shape[1] // 2
    slabs = half // LANES
    bits = lax.bitcast_convert_type(hh.astype(F32), U32)
    packed = (bits[:, :half] >> 16) | (bits[:, half:] & jnp.uint32(0xFFFF0000))
    for j in range(slabs):
        h_ref[pl.ds(j, tm, stride=slabs), :] = packed[:, j * LANES:(j + 1) * LANES]
    hl = (h - hh.astype(F32)).astype(BF16)
    w = wr_ref[...]
    wh = w.astype(BF16)
    wl = (w - wh.astype(F32)).astype(BF16)
    nt = functools.partial(lax.dot_general, dimension_numbers=NT_DIMS, preferred_element_type=F32)
    logits = nt(wh, hh) + nt(wh, hl) + nt(wl, hh) + br_ref[...]

    eio = lax.broadcasted_iota(I32, (N_EXPERTS, tm), 0).astype(F32)
    work = logits
    vals, onehots = [], []
    for k in range(TOP_K):
        mx = jnp.max(work, axis=0, keepdims=True)
        ix = jnp.min(jnp.where(work == mx, eio, float(N_EXPERTS)), axis=0, keepdims=True)
        sel = eio == ix
        vals.append(mx)
        onehots.append(sel)
        idx_ref[k:k + 1, :] = ix.astype(I32)
        work = jnp.where(sel, -jnp.inf, work)
    ex = [jnp.exp(v - vals[0]) for v in vals]
    denom = ex[0] + ex[1] + ex[2] + ex[3]
    for k in range(TOP_K):
        gate_ref[k:k + 1, :] = ex[k] / denom

    oh = [jnp.where(s, 1.0, 0.0) for s in onehots]
    oh_all = oh[0] + oh[1] + oh[2] + oh[3]
    r = lax.broadcasted_iota(I32, (tm, tm), 0)
    c = lax.broadcasted_iota(I32, (tm, tm), 1)
    upper = jnp.where(r < c, 1.0, 0.0).astype(BF16)
    before = jnp.dot(oh_all.astype(BF16), upper, preferred_element_type=F32) + base_ref[...]
    for k in range(TOP_K):
        rank_ref[k:k + 1, :] = jnp.sum(oh[k] * before, axis=0, keepdims=True).astype(I32)
    base_ref[...] = base_ref[...] + jnp.sum(oh_all, axis=1, keepdims=True)
    cnt_ref[...] = base_ref[...]


def _mix_tail_kernel(a_ref, hg_ref, x_ref, wa_ref, wh_ref, gx_ref, kv_ref, wq_ref, wo_ref, gr_ref, wr_ref, br_ref,
                     cnt0_ref, x2_ref, h_ref, idx_ref, gate_ref, rank_ref, cnt_ref, base_ref):
    @pl.when(jnp.logical_and(pl.program_id(0) == 0, pl.program_id(1) == 0))
    def _():
        base_ref[...] = cnt0_ref[...]

    x1 = _out_projection_math(a_ref, hg_ref, x_ref, wa_ref, wh_ref)
    x2 = _cross_attention_math(x1, gx_ref, kv_ref, wq_ref, wo_ref)
    x2_ref[...] = x2
    _router_math(x2, gr_ref, wr_ref, br_ref, h_ref, idx_ref, gate_ref, rank_ref, cnt_ref, base_ref)


def _mix_tail(att, hg, x, w_out, g_cross, kv, w_xq, w_xo, g_ffn, w_router_t, b_router, cnt0, batch, seq, n_mem, tm):
    m, d = x.shape
    nt = seq // tm

    def tile(width):
        return pl.BlockSpec((tm, width), lambda b, i: (b * nt + i, 0))

    def resident(shape, row_block=0):
        return pl.BlockSpec(shape, lambda b, i: (row_block, 0), pipeline_mode=pl.Buffered(1))

    row4 = pl.BlockSpec((TOP_K, tm), lambda b, i: (0, b * nt + i))
    slabs = d // 2 // LANES
    return pl.pallas_call(
        _mix_tail_kernel,
        grid=(batch, nt),
        in_specs=[tile(ATT_WIDTH), tile(HG_WIDTH), tile(d),
                  resident((ATT_WIDTH, d)), resident((HG_WIDTH, d), ATT_WIDTH // HG_WIDTH),
                  resident((1, d)),
                  pl.BlockSpec((n_mem, 2 * X_WIDTH), lambda b, i: (b, 0)),
                  resident((d, X_WIDTH)), resident((X_WIDTH, d)),
                  resident((1, d)), resident((N_EXPERTS, d)), resident((N_EXPERTS, 1)),
                  resident((N_EXPERTS, 1))],
        out_specs=[tile(d), pl.BlockSpec((tm * slabs, LANES), lambda b, i: (b * nt + i, 0)), row4, row4, row4,
                   pl.BlockSpec((N_EXPERTS, 1), lambda b, i: (0, 0))],
        out_shape=[jax.ShapeDtypeStruct((m, d), F32),
                   jax.ShapeDtypeStruct((m * slabs, LANES), U32),
                   jax.ShapeDtypeStruct((TOP_K, m), I32),
                   jax.ShapeDtypeStruct((TOP_K, m), F32),
                   jax.ShapeDtypeStruct((TOP_K, m), I32),
                   jax.ShapeDtypeStruct((N_EXPERTS, 1), F32)],
        scratch_shapes=[pltpu.VMEM((N_EXPERTS, 1), F32)],
        compiler_params=_params("arbitrary", "arbitrary"),
        name="mix_tail",
    )(att, hg, x, w_out, w_out, g_cross.reshape(1, d), kv, w_xq, w_xo, g_ffn.reshape(1, d), w_router_t,
      b_router.reshape(N_EXPERTS, 1), cnt0)


def _assign_rows_kernel(pstart_ref, idx_ref, rank_ref, dest_ref):
    idx = idx_ref[...]
    start = jnp.zeros_like(idx)
    for e in range(N_EXPERTS):
        start = jnp.where(idx == e, pstart_ref[e], start)
    dest_ref[...] = start + rank_ref[...]


def _assign_rows(pstart, idx, rank):
    full = pl.BlockSpec(idx.shape, lambda: (0, 0))
    return pl.pallas_call(
        _assign_rows_kernel,
        in_specs=[pl.BlockSpec(memory_space=pltpu.SMEM), full, full],
        out_specs=full,
        out_shape=jax.ShapeDtypeStruct(idx.shape, I32),
        name="assign_rows",
    )(pstart, idx, rank)


def _dispatch_kernel(seg_ref, dest_hbm, h_ref, xs_prev_hbm, xs_hbm, dsm, zrow, sem_idx, sem_row,
                     *, m_tokens, p_rows, fill_padding):
    del xs_prev_hbm
    i = pl.program_id(0)
    tt = SMEM_STAGE
    slabs = h_ref.shape[0] // tt

    def token_rows(ref, tok):
        return ref.at[pl.ds(pl.multiple_of(tok * slabs, slabs), slabs)]

    def idx_copy(k):
        return pltpu.make_async_copy(dest_hbm.at[pl.ds(k * m_tokens + i * tt, tt)],
                                     dsm.at[pl.ds(k * tt, tt)], sem_idx)

    for k in range(TOP_K):
        idx_copy(k).start()
    for k in range(TOP_K):
        idx_copy(k).wait()

    def row_copy(t, dst_row):
        return pltpu.make_async_copy(token_rows(h_ref, t), token_rows(xs_hbm, dst_row), sem_row)

    def issue(t, carry):
        for k in range(TOP_K):
            row_copy(t, dsm[k * tt + t]).start(priority=k % 2)
        return carry

    lax.fori_loop(0, tt, issue, 0, unroll=8)

    for k in range(TOP_K):
        pltpu.make_async_copy(h_ref, xs_hbm.at[pl.ds(0, tt * slabs)], sem_row).wait()

    @pl.when(jnp.logical_and(i == 0, jnp.bool_(fill_padding)))
    def _():
        zrow[...] = jnp.zeros_like(zrow)
        max_chunk = zrow.shape[0] // slabs

        def zero_copy(dst_row, nrows):
            return pltpu.make_async_copy(
                zrow.at[pl.ds(0, nrows * slabs)],
                xs_hbm.at[pl.ds(pl.multiple_of(dst_row * slabs, slabs), nrows * slabs)], sem_row)

        def zero_rows(dst_row, nrows):
            zero_copy(dst_row, nrows).start()
            zero_copy(dst_row, nrows).wait()

        def per_expert(e, carry):
            lo = seg_ref[0, e] + seg_ref[1, e]
            pad = seg_ref[2, e] - lo
            chunk = max_chunk
            while chunk >= 1:
                take = pad & chunk

                @pl.when(take != 0)
                def _():
                    zero_rows(lo, chunk)

                lo = lo + take
                chunk //= 2
            return carry

        lax.fori_loop(0, N_EXPERTS, per_expert, 0)

        tail_lo = seg_ref[2, N_EXPERTS - 1] // max_chunk
        tail_hi = p_rows // max_chunk

        def tail_start(c, carry):
            zero_copy(c * max_chunk, max_chunk).start()
            return carry

        def tail_wait(c, carry):
            zero_copy(c * max_chunk, max_chunk).wait()
            return carry

        lax.fori_loop(tail_lo, tail_hi, tail_start, 0)
        lax.fori_loop(tail_lo, tail_hi, tail_wait, 0)


def _dispatch(seg, dest_flat, h, xs_prev, m, p_rows, fill_padding):
    slabs = h.shape[0] // m
    return pl.pallas_call(
        functools.partial(_dispatch_kernel, m_tokens=m, p_rows=p_rows, fill_padding=fill_padding),
        grid=(m // SMEM_STAGE,),
        in_specs=[pl.BlockSpec(memory_space=pltpu.SMEM),
                  pl.BlockSpec(memory_space=pl.ANY),
                  pl.BlockSpec((SMEM_STAGE * slabs, LANES), lambda i: (i, 0)),
                  pl.BlockSpec(memory_space=pl.ANY)],
        out_specs=pl.BlockSpec(memory_space=pl.ANY),
        out_shape=jax.ShapeDtypeStruct((p_rows * slabs, LANES), h.dtype),
        input_output_aliases={3: 0},
        scratch_shapes=[pltpu.SMEM((TOP_K * SMEM_STAGE,), I32),
                        pltpu.VMEM((MOE_TILE // 2 * slabs, LANES), h.dtype),
                        pltpu.SemaphoreType.DMA,
                        pltpu.SemaphoreType.DMA],
        compiler_params=_params("arbitrary"),
        name="dispatch",
    )(seg, dest_flat, h, xs_prev)


def _expert_kernel(te_ref, tv_ref, nu_ref, x_ref, w1g_ref, w1l_ref, b1g_ref, b1l_ref, w2_ref, b2_ref,
                   o_ref, xb_ref):
    i = pl.program_id(0)
    f = pl.program_id(1)
    nf = pl.num_programs(1)
    live = tv_ref[i] > 0

    @pl.when(live)
    def _():
        @pl.when(f == 0)
        def _():
            o_ref[...] = jnp.broadcast_to(b2_ref[0], o_ref.shape)

        tmb, d = xb_ref.shape
        half = d // 2
        slabs = half // LANES
        for j in range(slabs):
            words = x_ref[pl.ds(j, tmb, stride=slabs), :]
            lo = slice(j * LANES, (j + 1) * LANES)
            hi = slice(half + j * LANES, half + (j + 1) * LANES)
            xb_ref[:, lo] = lax.bitcast_convert_type(words << 16, F32).astype(BF16)
            xb_ref[:, hi] = lax.bitcast_convert_type(words & jnp.uint32(0xFFFF0000), F32).astype(BF16)

        xb = xb_ref[...]
        glu = jnp.dot(xb, w1g_ref[0], preferred_element_type=F32) + b1g_ref[0]
        lin = jnp.dot(xb, w1l_ref[0], preferred_element_type=F32) + b1l_ref[0]
        glu = jnp.minimum(glu, SWIGLU_LIMIT)
        lin = jnp.clip(lin, -SWIGLU_LIMIT, SWIGLU_LIMIT)
        act = glu * _sigmoid(SWIGLU_ALPHA * glu) * (lin + 1.0)
        o_ref[...] += jnp.dot(act.astype(BF16), w2_ref[0], preferred_element_type=F32)

    @pl.when(jnp.logical_and(jnp.logical_not(live), f == nf - 1))
    def _():
        o_ref[...] = jnp.zeros_like(o_ref)


def _expert_mlp(tile_e, tile_valid, n_used, xs, w1, b1, w2, b2, tmb, tf):
    d = w1.shape[1]
    p_rows = xs.shape[0] // (d // 2 // LANES)
    d_ff = w2.shape[1]
    nf = d_ff // tf
    n_tiles = p_rows // tmb

    def fsel(i, f, tv):
        return jnp.where(tv[i] > 0, f, nf - 1)

    grid_spec = pltpu.PrefetchScalarGridSpec(
        num_scalar_prefetch=3,
        grid=(n_tiles, nf),
        in_specs=[
            pl.BlockSpec((tmb * (d // 2 // LANES), LANES), lambda i, f, te, tv, nu: (jnp.minimum(i, nu[0] - 1), 0)),
            pl.BlockSpec((1, d, tf), lambda i, f, te, tv, nu: (te[i], 0, fsel(i, f, tv))),
            pl.BlockSpec((1, d, tf), lambda i, f, te, tv, nu: (te[i], 0, nf + fsel(i, f, tv))),
            pl.BlockSpec((1, 1, tf), lambda i, f, te, tv, nu: (te[i], 0, fsel(i, f, tv))),
            pl.BlockSpec((1, 1, tf), lambda i, f, te, tv, nu: (te[i], 0, nf + fsel(i, f, tv))),
            pl.BlockSpec((1, tf, d), lambda i, f, te, tv, nu: (te[i], fsel(i, f, tv), 0)),
            pl.BlockSpec((1, 1, d), lambda i, f, te, tv, nu: (te[i], 0, 0)),
        ],
        out_specs=pl.BlockSpec((tmb, d), lambda i, f, te, tv, nu: (i, 0)),
        scratch_shapes=[pltpu.VMEM((tmb, d), BF16)],
    )
    return pl.pallas_call(
        _expert_kernel,
        grid_spec=grid_spec,
        out_shape=jax.ShapeDtypeStruct((p_rows, d), F32),
        compiler_params=_params("arbitrary", "arbitrary"),
        name="expert_mlp",
    )(tile_e, tile_valid, n_used, xs, w1, w1, b1, b1, w2, b2)


def _combine_kernel(dest_hbm, ys_hbm, x_ref, gate_ref, g_ref, o_ref, dsm, buf, sem_idx, sem_row, *, m_tokens, tc):
    i = pl.program_id(0)
    n = pl.num_programs(0)
    per_stage = SMEM_STAGE // tc

    def issue_tile(tile):
        slot = tile % 2
        off = (tile % per_stage) * tc

        @pl.when(tile % per_stage == 0)
        def _():
            base = (tile // per_stage) * SMEM_STAGE

            def idx_copy(k):
                return pltpu.make_async_copy(dest_hbm.at[pl.ds(k * m_tokens + base, SMEM_STAGE)],
                                             dsm.at[pl.ds(k * SMEM_STAGE, SMEM_STAGE)], sem_idx)

            for k in range(TOP_K):
                idx_copy(k).start()
            for k in range(TOP_K):
                idx_copy(k).wait()

        def issue(t8, carry):
            for u in range(SUBLANES):
                for k in range(TOP_K):
                    row = dsm[k * SMEM_STAGE + off + t8 * SUBLANES + u]
                    pltpu.make_async_copy(ys_hbm.at[pl.ds(row, 1)], buf.at[slot, k, t8, pl.ds(u, 1)],
                                          sem_row.at[slot]).start(priority=k % 2)
            return carry

        lax.fori_loop(0, tc // SUBLANES, issue, 0)

    @pl.when(i == 0)
    def _():
        issue_tile(i)

    @pl.when(i + 1 < n)
    def _():
        issue_tile(i + 1)

    slot = i % 2
    for k in range(TOP_K):
        pltpu.make_async_copy(buf.at[slot, k], buf.at[slot, k], sem_row.at[slot]).wait()

    gate = gate_ref[...]
    y = x_ref[...]
    for k in range(TOP_K):
        y = y + buf[slot, k].reshape(y.shape) * gate[:, k:k + 1]
    o_ref[...] = _rms(y, g_ref[...])


def _combine(dest_flat, ys, x, gate_t, g, tc):
    m, d = x.shape
    return pl.pallas_call(
        functools.partial(_combine_kernel, m_tokens=m, tc=tc),
        grid=(m // tc,),
        in_specs=[pl.BlockSpec(memory_space=pl.ANY),
                  pl.BlockSpec(memory_space=pl.ANY),
                  pl.BlockSpec((tc, d), lambda i: (i, 0)),
                  pl.BlockSpec((tc, TOP_K), lambda i: (i, 0)),
                  pl.BlockSpec((1, d), lambda i: (0, 0))],
        out_specs=pl.BlockSpec((tc, d), lambda i: (i, 0)),
        out_shape=jax.ShapeDtypeStruct((m, d), F32),
        scratch_shapes=[pltpu.SMEM((TOP_K * SMEM_STAGE,), I32),
                        pltpu.VMEM((2, TOP_K, tc // SUBLANES, SUBLANES, d), F32),
                        pltpu.SemaphoreType.DMA,
                        pltpu.SemaphoreType.DMA((2,))],
        compiler_params=_params("arbitrary"),
        name="combine",
    )(dest_flat, ys, x, gate_t, g.reshape(1, d))


def _rope_tables(seq):
    half = ATT_HEAD_DIM // 2
    inv = ROPE_THETA ** (-jnp.arange(0, ATT_HEAD_DIM, 2, dtype=F32) / ATT_HEAD_DIM)
    ang = jnp.arange(seq, dtype=F32)[:, None] * inv[None, :]
    cos = jnp.tile(jnp.cos(ang), (1, LANES // half))
    sin = jnp.sin(ang)
    sin_signed = jnp.tile(jnp.concatenate([-sin, sin], axis=-1), (1, LANES // ATT_HEAD_DIM))
    return cos, sin_signed


def _pick(n, prefs):
    for p in prefs:
        if n % p == 0:
            return p
    return n


def _uninitialized_kernel(o_ref):
    del o_ref


def _uninitialized(shape, dtype):
    return pl.pallas_call(
        _uninitialized_kernel,
        out_specs=pl.BlockSpec(memory_space=pl.ANY),
        out_shape=jax.ShapeDtypeStruct(shape, dtype),
        name="uninitialized",
    )()


def _trunk_front(x, mem, wts, cnt0):
    batch, seq, d = x.shape
    n_mem = mem.shape[1]
    m = batch * seq
    x2d = x.reshape(m, d)

    n_in = wts["w_in"].shape[1]
    z = _norm_matmul(x2d, wts["norm_mix"], wts["w_in"], _pick(m, (1024, 512, 256, 128)),
                     _pick(n_in, (1664, 1024, 512, 256, 128)), F32, "in_projection")

    cos, sin_signed = _rope_tables(seq)
    att = _window_attention(z, wts["att_sink"], cos, sin_signed, batch, seq, _pick(seq, (256, 128)))
    hg = _hgrn2(z, wts["lb"], wts["hg_norm"], batch, seq, _pick(seq, (256, 128, 64)))
    kv = _norm_matmul(mem.reshape(batch * n_mem, d), wts["norm_mem"], wts["w_xkv"], n_mem, 2 * X_WIDTH, BF16,
                      "memory_kv")
    return _mix_tail(att, hg, x2d, wts["w_out"], wts["norm_cross"], kv, wts["w_xq"], wts["w_xo"], wts["norm_ffn"],
                     wts["w_router_t"], wts["b_router"], cnt0, batch, seq, n_mem, _pick(seq, (512, 256, 128)))


def _moe_and_final_norm(fronts, shapes, wts, moe_tile, ff_tile):
    counts = fronts[-1][5][:, 0].astype(I32)
    padded = (counts + moe_tile - 1) // moe_tile * moe_tile
    pend = jnp.cumsum(padded)
    pstart = pend - padded
    m_total = sum(f[0].shape[0] for f in fronts)
    n_tiles = -(-(m_total * TOP_K) // moe_tile) + N_EXPERTS
    p_rows = n_tiles * moe_tile
    tile_row = jnp.arange(n_tiles, dtype=I32) * moe_tile
    tile_e = jnp.minimum(jnp.sum((pend[None, :] <= tile_row[:, None]).astype(I32), axis=1), N_EXPERTS - 1)
    owner = tile_e[:, None] == jnp.arange(N_EXPERTS, dtype=I32)[None, :]
    seg_end = jnp.sum(jnp.where(owner, (pstart + counts)[None, :], 0), axis=1)
    tile_valid = jnp.clip(seg_end - tile_row, 0, moe_tile).astype(I32)
    n_used = (pend[-1:] // moe_tile).astype(I32)
    seg = jnp.stack([pstart, counts, pend]).astype(I32)

    h_rows, lanes = fronts[0][1].shape
    slabs = h_rows // fronts[0][0].shape[0]
    xs = _uninitialized((p_rows * slabs, lanes), fronts[0][1].dtype)
    dests = []
    for n, (x2, h3, idx, gate, rank, cnt) in enumerate(fronts):
        dest_flat = _assign_rows(pstart.astype(I32), idx, rank).reshape(-1)
        dests.append(dest_flat)
        xs = _dispatch(seg, dest_flat, h3, xs, x2.shape[0], p_rows, fill_padding=(n == 0))
    ys = _expert_mlp(tile_e, tile_valid, n_used, xs, wts["w_moe1"], wts["b_moe1"], wts["w_moe2"], wts["b_moe2"],
                     moe_tile, ff_tile)
    outs = []
    for (x2, h3, idx, gate, rank, cnt), dest_flat, shape in zip(fronts, dests, shapes):
        y = _combine(dest_flat, ys, x2, gate.T, wts["norm_final"], _pick(x2.shape[0], (256, 128)))
        outs.append(y.reshape(shape))
    return tuple(outs)


def kernel(x_prompt, x_sample, mem_prompt, mem_sample, norm_mix, w_in, att_sink, hg_lb_logits, hg_norm, w_out,
           norm_cross, norm_mem, w_xq, w_xkv, w_xo, norm_ffn, w_router, b_router, w_moe1, b_moe1, w_moe2, b_moe2,
           norm_final):
    assert w_in.shape[0] == 1, "the final norm is fused after the single layer"
    lb_all = jnp.cumsum(jax.nn.softmax(hg_lb_logits.astype(F32), axis=0), axis=0)
    d_ff = w_moe2.shape[2]
    wts = dict(
        norm_mix=norm_mix[0], w_in=w_in[0].astype(BF16), att_sink=att_sink[0], lb=lb_all[0],
        hg_norm=hg_norm[0], w_out=w_out[0].astype(BF16), norm_cross=norm_cross[0], norm_mem=norm_mem[0],
        w_xq=w_xq[0].astype(BF16), w_xkv=w_xkv[0].astype(BF16), w_xo=w_xo[0].astype(BF16),
        norm_ffn=norm_ffn[0], w_router_t=w_router[0].T, b_router=b_router[0],
        w_moe1=w_moe1[0].astype(BF16), b_moe1=b_moe1[0][:, None, :],
        w_moe2=w_moe2[0].astype(BF16), b_moe2=b_moe2[0][:, None, :],
        norm_final=norm_final,
    )
    ff_tile = _pick(d_ff, (1024, 512, 256, 128))
    fronts = []
    cnt = jnp.zeros((N_EXPERTS, 1), F32)
    for x, mem in ((x_prompt, mem_prompt), (x_sample, mem_sample)):
        front = _trunk_front(x, mem, wts, cnt)
        cnt = front[5]
        fronts.append(front)
    return _moe_and_final_norm(fronts, (x_prompt.shape, x_sample.shape), wts, MOE_TILE, ff_tile)
```

```python
import functools

import jax
import jax.numpy as jnp
from jax import lax
from jax.experimental import pallas as pl
from jax.experimental.pallas import tpu as pltpu

F32 = jnp.float32
BF16 = jnp.bfloat16
I32 = jnp.int32
U32 = jnp.uint32

EPS = 1e-5
ATT_HEADS = 16
ATT_KV_HEADS = 4
ATT_HEAD_DIM = 64
ATT_WIDTH = ATT_HEADS * ATT_HEAD_DIM
KV_WIDTH = ATT_KV_HEADS * ATT_HEAD_DIM
WINDOW = 128
ROPE_THETA = 10000.0
HG_HEADS = 8
HG_DIM = 128
HG_WIDTH = HG_HEADS * HG_DIM
HG_CHUNK = 64
X_HEADS = 4
X_HEAD_DIM = 128
X_WIDTH = X_HEADS * X_HEAD_DIM
N_EXPERTS = 32
TOP_K = 4
SWIGLU_ALPHA = 1.702
SWIGLU_LIMIT = 7.0

COL_Q_ATT = 0
COL_K_ATT = ATT_WIDTH
COL_V_ATT = ATT_WIDTH + KV_WIDTH
COL_Q_HG = ATT_WIDTH + 2 * KV_WIDTH
COL_F_FWD = COL_Q_HG + HG_WIDTH
COL_F_BWD = COL_F_FWD + HG_WIDTH
COL_I_HG = COL_F_BWD + HG_WIDTH
COL_G_HG = COL_I_HG + HG_WIDTH

LANES = 128
SUBLANES = 8
VMEM_LIMIT = 56 * 1024 * 1024
SMEM_STAGE = 1024
MOE_TILE = 512

NT_DIMS = (((1,), (1,)), ((), ()))
TN_DIMS = (((0,), (0,)), ((), ()))


def _params(*sem):
    return pltpu.CompilerParams(dimension_semantics=sem, vmem_limit_bytes=VMEM_LIMIT)


def _rms(x, g):
    ms = jnp.mean(x * x, axis=-1, keepdims=True)
    return x * lax.rsqrt(ms + EPS) * g


def _sigmoid(x):
    return 1.0 / (1.0 + jnp.exp(-x))


def _norm_matmul_kernel(x_ref, g_ref, w_ref, o_ref, xn_ref):
    @pl.when(pl.program_id(1) == 0)
    def _():
        xn_ref[...] = _rms(x_ref[...], g_ref[...]).astype(BF16)

    o_ref[...] = jnp.dot(xn_ref[...], w_ref[...], preferred_element_type=F32).astype(o_ref.dtype)


def _norm_matmul(x, g, w, tm, tn, out_dtype, name):
    m, d = x.shape
    n = w.shape[1]
    return pl.pallas_call(
        _norm_matmul_kernel,
        grid=(m // tm, n // tn),
        in_specs=[pl.BlockSpec((tm, d), lambda i, j: (i, 0)),
                  pl.BlockSpec((1, d), lambda i, j: (0, 0)),
                  pl.BlockSpec((d, tn), lambda i, j: (0, j))],
        out_specs=pl.BlockSpec((tm, tn), lambda i, j: (i, j)),
        out_shape=jax.ShapeDtypeStruct((m, n), out_dtype),
        scratch_shapes=[pltpu.VMEM((tm, d), BF16)],
        compiler_params=_params("arbitrary", "arbitrary"),
        name=name,
    )(x, g.reshape(1, d), w)


def _rope_pair(x, cos, sin_signed):
    lane = lax.broadcasted_iota(I32, (1, LANES), 1)
    first_half = (lane % ATT_HEAD_DIM) < (ATT_HEAD_DIM // 2)
    rot = jnp.where(first_half, pltpu.roll(x, LANES - ATT_HEAD_DIM // 2, 1), pltpu.roll(x, ATT_HEAD_DIM // 2, 1))
    return x * cos + rot * sin_signed


def _attn_kernel(sink_ref, q_ref, k_ref, v_ref, cq_ref, sq_ref, ck_ref, sk_ref, o_ref, klo_ref, khi_ref, vt_ref,
                 *, seq, tq):
    qi = pl.program_id(1)
    win = tq + 2 * WINDOW
    half_lanes = lax.broadcasted_iota(I32, (1, LANES), 1) < ATT_HEAD_DIM
    heads_per_slab = LANES // ATT_HEAD_DIM

    @pl.when(qi == 0)
    def _():
        for j in range(KV_WIDTH // LANES):
            slab = _rope_pair(k_ref[:, j * LANES:(j + 1) * LANES], ck_ref[...], sk_ref[...])
            swapped = pltpu.roll(slab, ATT_HEAD_DIM, 1)
            g0 = j * heads_per_slab
            g1 = g0 + 1
            klo_ref[:, g0 * LANES:(g0 + 1) * LANES] = jnp.where(half_lanes, slab, 0.0).astype(BF16)
            khi_ref[:, g0 * LANES:(g0 + 1) * LANES] = jnp.where(half_lanes, 0.0, swapped).astype(BF16)
            klo_ref[:, g1 * LANES:(g1 + 1) * LANES] = jnp.where(half_lanes, swapped, 0.0).astype(BF16)
            khi_ref[:, g1 * LANES:(g1 + 1) * LANES] = jnp.where(half_lanes, 0.0, slab).astype(BF16)
            vt_ref[j * LANES:(j + 1) * LANES, :] = v_ref[:, j * LANES:(j + 1) * LANES].T.astype(BF16)

    q0 = qi * tq
    ks = pl.multiple_of(jnp.clip(q0 - WINDOW, 0, seq - win), WINDOW)
    kpos = ks + lax.broadcasted_iota(I32, (win, 1), 0)
    qpos = q0 + lax.broadcasted_iota(I32, (1, tq), 1)
    valid = jnp.abs(kpos - qpos) <= WINDOW
    cq = cq_ref[...]
    sq = sq_ref[...]
    group = ATT_HEADS // ATT_KV_HEADS
    scores = []
    for j in range(ATT_WIDTH // LANES):
        qs = (_rope_pair(q_ref[:, j * LANES:(j + 1) * LANES], cq, sq) * (ATT_HEAD_DIM ** -0.5)).astype(BF16)
        for hh in range(heads_per_slab):
            g = (j * heads_per_slab + hh) // group
            k_ref_sel = klo_ref if hh == 0 else khi_ref
            ksel = k_ref_sel[pl.ds(ks, win), g * LANES:(g + 1) * LANES]
            scores.append(lax.dot_general(ksel, qs, NT_DIMS, preferred_element_type=F32))
    for j in range(ATT_WIDTH // LANES):
        outs = []
        for hh in range(heads_per_slab):
            h = j * heads_per_slab + hh
            g = h // group
            st = jnp.where(valid, scores[h], -1e30)
            sk = sink_ref[h]
            m = jnp.maximum(jnp.max(st, axis=0, keepdims=True), sk)
            p = jnp.exp(st - m)
            denom = jnp.sum(p, axis=0, keepdims=True) + jnp.exp(sk - m)
            vt = vt_ref[g * ATT_HEAD_DIM:(g + 1) * ATT_HEAD_DIM, pl.ds(ks, win)]
            ot = jnp.dot(vt, p.astype(BF16), preferred_element_type=F32)
            outs.append(ot / denom)
        o_ref[:, j * LANES:(j + 1) * LANES] = jnp.concatenate(outs, axis=0).T.astype(o_ref.dtype)


def _window_attention(z, sink, cos, sin_signed, batch, seq, tq):
    m = z.shape[0]
    nq = seq // tq
    qblk = COL_Q_ATT // ATT_WIDTH
    kblk = COL_K_ATT // KV_WIDTH
    vblk = COL_V_ATT // KV_WIDTH
    return pl.pallas_call(
        functools.partial(_attn_kernel, seq=seq, tq=tq),
        grid=(batch, nq),
        in_specs=[pl.BlockSpec(memory_space=pltpu.SMEM),
                  pl.BlockSpec((tq, ATT_WIDTH), lambda b, i: (b * nq + i, qblk)),
                  pl.BlockSpec((seq, KV_WIDTH), lambda b, i: (b, kblk)),
                  pl.BlockSpec((seq, KV_WIDTH), lambda b, i: (b, vblk)),
                  pl.BlockSpec((tq, LANES), lambda b, i: (i, 0)),
                  pl.BlockSpec((tq, LANES), lambda b, i: (i, 0)),
                  pl.BlockSpec((seq, LANES), lambda b, i: (0, 0)),
                  pl.BlockSpec((seq, LANES), lambda b, i: (0, 0))],
        out_specs=pl.BlockSpec((tq, ATT_WIDTH), lambda b, i: (b * nq + i, 0)),
        out_shape=jax.ShapeDtypeStruct((m, ATT_WIDTH), BF16),
        scratch_shapes=[pltpu.VMEM((seq, ATT_KV_HEADS * LANES), BF16),
                        pltpu.VMEM((seq, ATT_KV_HEADS * LANES), BF16),
                        pltpu.VMEM((KV_WIDTH, seq), BF16)],
        compiler_params=_params("arbitrary", "arbitrary"),
        name="window_attention",
    )(sink, z, z, z, cos, sin_signed, cos, sin_signed)


def _split_cumsum(tri, x):
    hi = x.astype(BF16)
    lo = (x - hi.astype(F32)).astype(BF16)
    dot = functools.partial(jnp.dot, preferred_element_type=F32)
    return dot(tri, hi) + dot(tri, lo)


def _run_interleaved(stage_generators):
    results = [None] * len(stage_generators)
    live = list(range(len(stage_generators)))
    while live:
        for i in list(live):
            try:
                next(stage_generators[i])
            except StopIteration as done:
                results[i] = done.value
                live.remove(i)
    return results


def _hgrn_group(zq, zf, v, lbv, state, mask, tri, forward):
    rows = zq.shape[0]
    c = HG_CHUNK
    nchunk = rows // c
    q = zq * _sigmoid(zq) * (HG_DIM ** -0.5)
    f = lbv + (1.0 - lbv) * _sigmoid(zf)
    logf = jnp.log(f)
    k = 1.0 - f
    yield
    b = _split_cumsum(tri, logf)
    yield
    ref_row = c // 2 - 1 if forward else c // 2
    last_row = c - 1 if forward else 0
    qe, ke, kd, qb, dec = [], [], [], [], []
    for n in range(nchunk):
        sl = slice(n * c, (n + 1) * c)
        bn, qn, kn = b[sl], q[sl], k[sl]
        bref = bn[ref_row:ref_row + 1]
        blast = bn[last_row:last_row + 1]
        qe.append(qn * jnp.exp(bn - bref))
        ke.append(kn * jnp.exp(bref - bn))
        kd.append((kn * jnp.exp(blast - bn)).astype(BF16))
        qb.append((qn * jnp.exp(bn)).astype(BF16))
        dec.append(jnp.exp(blast))
    qe = jnp.concatenate(qe, axis=0).astype(BF16)
    ke = jnp.concatenate(ke, axis=0).astype(BF16)
    vb = v.astype(BF16)
    a = lax.dot_general(qe, ke, NT_DIMS, preferred_element_type=F32)

    zero_blk = jnp.zeros((c, HG_DIM), BF16)

    def block_diag(blocks):
        return jnp.concatenate(
            [jnp.concatenate([blk if j == n else zero_blk for j in range(nchunk)], axis=1)
             for n, blk in enumerate(blocks)], axis=0)

    u_all = lax.dot_general(vb, block_diag(kd), TN_DIMS, preferred_element_type=F32)
    yield
    a = jnp.where(mask, a, 0.0).astype(BF16)
    o = jnp.dot(a, vb, preferred_element_type=F32)
    entering = [None] * nchunk
    order = range(nchunk) if forward else range(nchunk - 1, -1, -1)
    for n in order:
        entering[n] = state.astype(BF16)
        state = state * dec[n] + u_all[:, n * HG_DIM:(n + 1) * HG_DIM]
    o = o + lax.dot_general(block_diag(qb), jnp.concatenate(entering, axis=1), NT_DIMS,
                            preferred_element_type=F32)
    return o, state


def _hgrn_kernel(zq_ref, zff_ref, zfb_ref, zi_ref, zg_ref, lb_ref, gn_ref, o_ref, accf_ref, accb_ref, *, seq, rows):
    ngroups = seq // rows
    r = lax.broadcasted_iota(I32, (rows, rows), 0)
    cidx = lax.broadcasted_iota(I32, (rows, rows), 1)
    same = (r // HG_CHUNK) == (cidx // HG_CHUNK)
    mask_f = jnp.logical_and(same, cidx <= r)
    mask_b = jnp.logical_and(same, cidx >= r)
    tri_f = jnp.where(mask_f, 1.0, 0.0).astype(BF16)
    tri_b = jnp.where(mask_b, 1.0, 0.0).astype(BF16)
    lb_f = lb_ref[0:1, :]
    lb_b = lb_ref[1:2, :]
    zero_state = jnp.zeros((HG_DIM, HG_DIM), F32)

    def readout(sl, tot):
        zg = zg_ref[sl, :]
        y = _rms(tot, gn_ref[...]) * (zg * _sigmoid(zg))
        o_ref[sl, :] = y.astype(o_ref.dtype)

    def scan_body(it, states, meet):
        sf, sb = states
        slf = pl.ds(pl.multiple_of(it * rows, rows), rows)
        slb = pl.ds(pl.multiple_of((ngroups - 1 - it) * rows, rows), rows)
        (of, sf), (ob, sb) = _run_interleaved([
            _hgrn_group(zq_ref[slf, :], zff_ref[slf, :], zi_ref[slf, :], lb_f, sf, mask_f, tri_f, True),
            _hgrn_group(zq_ref[slb, :], zfb_ref[slb, :], zi_ref[slb, :], lb_b, sb, mask_b, tri_b, False)])
        if meet:
            readout(slf, of + accb_ref[slf, :])
            readout(slb, accf_ref[slb, :] + ob)
        else:
            accf_ref[slf, :] = of
            accb_ref[slb, :] = ob
        return sf, sb

    assert ngroups % 2 == 0, "the two scans must meet between two groups"
    states = lax.fori_loop(0, ngroups // 2, functools.partial(scan_body, meet=False), (zero_state, zero_state))
    lax.fori_loop(ngroups // 2, ngroups, functools.partial(scan_body, meet=True), states)


def _hgrn2(z, lb, hg_norm, batch, seq, rows):
    m = z.shape[0]

    def zspec(col):
        base = col // HG_DIM
        return pl.BlockSpec((seq, HG_DIM), lambda b, h: (b, base + h))

    return pl.pallas_call(
        functools.partial(_hgrn_kernel, seq=seq, rows=rows),
        grid=(batch, HG_HEADS),
        in_specs=[zspec(COL_Q_HG), zspec(COL_F_FWD), zspec(COL_F_BWD), zspec(COL_I_HG), zspec(COL_G_HG),
                  pl.BlockSpec((2, HG_DIM), lambda b, h: (0, h)),
                  pl.BlockSpec((1, HG_DIM), lambda b, h: (0, 0))],
        out_specs=pl.BlockSpec((seq, HG_DIM), lambda b, h: (b, h)),
        out_shape=jax.ShapeDtypeStruct((m, HG_WIDTH), BF16),
        scratch_shapes=[pltpu.VMEM((seq, HG_DIM), F32), pltpu.VMEM((seq, HG_DIM), F32)],
        compiler_params=_params("arbitrary", "arbitrary"),
        name="hgrn2",
    )(z, z, z, z, z, lb, hg_norm.reshape(1, HG_DIM))


def _out_projection_math(a_ref, hg_ref, x_ref, wa_ref, wh_ref):
    acc = jnp.dot(a_ref[...], wa_ref[...], preferred_element_type=F32)
    acc = acc + jnp.dot(hg_ref[...], wh_ref[...], preferred_element_type=F32)
    return x_ref[...] + acc


def _cross_attention_math(x, g_ref, kv_ref, wq_ref, wo_ref):
    h = _rms(x, g_ref[...]).astype(BF16)
    q = jnp.dot(h, wq_ref[...], preferred_element_type=F32).astype(BF16)
    scores = []
    for hd in range(X_HEADS):
        sl = slice(hd * X_HEAD_DIM, (hd + 1) * X_HEAD_DIM)
        scores.append(lax.dot_general(q[:, sl], kv_ref[:, sl], NT_DIMS, preferred_element_type=F32))
    outs = []
    for hd in range(X_HEADS):
        vh = kv_ref[:, X_WIDTH + hd * X_HEAD_DIM:X_WIDTH + (hd + 1) * X_HEAD_DIM]
        s = scores[hd] * (X_HEAD_DIM ** -0.5)
        p = jnp.exp(s - jnp.max(s, axis=-1, keepdims=True))
        denom = jnp.sum(p, axis=-1, keepdims=True)
        outs.append(jnp.dot(p.astype(BF16), vh, preferred_element_type=F32) / denom)
    oc = jnp.concatenate(outs, axis=-1).astype(BF16)
    return x + jnp.dot(oc, wo_ref[...], preferred_element_type=F32)


def _router_math(x, g_ref, wr_ref, br_ref, h_ref, idx_ref, gate_ref, rank_ref, cnt_ref, base_ref):
    tm = x.shape[0]
    h = _rms(x, g_ref[...])
    hh = h.astype(BF16)
    half = h.shape[1] // 2
    slabs = half // LANES
    bits = lax.bitcast_convert_type(hh.astype(F32), U32)
    packed = (bits[:, :half] >> 16) | (bits[:, half:] & jnp.uint32(0xFFFF0000))
    for j in range(slabs):
        h_ref[pl.ds(j, tm, stride=slabs), :] = packed[:, j * LANES:(j + 1) * LANES]
    hl = (h - hh.astype(F32)).astype(BF16)
    w = wr_ref[...]
    wh = w.astype(BF16)
    wl = (w - wh.astype(F32)).astype(BF16)
    nt = functools.partial(lax.dot_general, dimension_numbers=NT_DIMS, preferred_element_type=F32)
    logits = nt(wh, hh) + nt(wh, hl) + nt(wl, hh) + br_ref[...]

    eio = lax.broadcasted_iota(I32, (N_EXPERTS, tm), 0).astype(F32)
    work = logits
    vals, onehots = [], []
    for k in range(TOP_K):
        mx = jnp.max(work, axis=0, keepdims=True)
        ix = jnp.min(jnp.where(work == mx, eio, float(N_EXPERTS)), axis=0, keepdims=True)
        sel = eio == ix
        vals.append(mx)
        onehots.append(sel)
        idx_ref[k:k + 1, :] = ix.astype(I32)
        work = jnp.where(sel, -jnp.inf, work)
    ex = [jnp.exp(v - vals[0]) for v in vals]
    denom = ex[0] + ex[1] + ex[2] + ex[3]
    for k in range(TOP_K):
        gate_ref[k:k + 1, :] = ex[k] / denom

    oh = [jnp.where(s, 1.0, 0.0) for s in onehots]
    oh_all = oh[0] + oh[1] + oh[2] + oh[3]
    r = lax.broadcasted_iota(I32, (tm, tm), 0)
    c = lax.broadcasted_iota(I32, (tm, tm), 1)
    upper = jnp.where(r < c, 1.0, 0.0).astype(BF16)
    before = jnp.dot(oh_all.astype(BF16), upper, preferred_element_type=F32) + base_ref[...]
    for k in range(TOP_K):
        rank_ref[k:k + 1, :] = jnp.sum(oh[k] * before, axis=0, keepdims=True).astype(I32)
    base_ref[...] = base_ref[...] + jnp.sum(oh_all, axis=1, keepdims=True)
    cnt_ref[...] = base_ref[...]


def _mix_tail_kernel(a_ref, hg_ref, x_ref, wa_ref, wh_ref, gx_ref, kv_ref, wq_ref, wo_ref, gr_ref, wr_ref, br_ref,
                     cnt0_ref, x2_ref, h_ref, idx_ref, gate_ref, rank_ref, cnt_ref, base_ref):
    @pl.when(jnp.logical_and(pl.program_id(0) == 0, pl.program_id(1) == 0))
    def _():
        base_ref[...] = cnt0_ref[...]

    x1 = _out_projection_math(a_ref, hg_ref, x_ref, wa_ref, wh_ref)
    x2 = _cross_attention_math(x1, gx_ref, kv_ref, wq_ref, wo_ref)
    x2_ref[...] = x2
    _router_math(x2, gr_ref, wr_ref, br_ref, h_ref, idx_ref, gate_ref, rank_ref, cnt_ref, base_ref)


def _mix_tail(att, hg, x, w_out, g_cross, kv, w_xq, w_xo, g_ffn, w_router_t, b_router, cnt0, batch, seq, n_mem, tm):
    m, d = x.shape
    nt = seq // tm

    def tile(width):
        return pl.BlockSpec((tm, width), lambda b, i: (b * nt + i, 0))

    def resident(shape, row_block=0):
        return pl.BlockSpec(shape, lambda b, i: (row_block, 0), pipeline_mode=pl.Buffered(1))

    row4 = pl.BlockSpec((TOP_K, tm), lambda b, i: (0, b * nt + i))
    slabs = d // 2 // LANES
    return pl.pallas_call(
        _mix_tail_kernel,
        grid=(batch, nt),
        in_specs=[tile(ATT_WIDTH), tile(HG_WIDTH), tile(d),
                  resident((ATT_WIDTH, d)), resident((HG_WIDTH, d), ATT_WIDTH // HG_WIDTH),
                  resident((1, d)),
                  pl.BlockSpec((n_mem, 2 * X_WIDTH), lambda b, i: (b, 0)),
                  resident((d, X_WIDTH)), resident((X_WIDTH, d)),
                  resident((1, d)), resident((N_EXPERTS, d)), resident((N_EXPERTS, 1)),
                  resident((N_EXPERTS, 1))],
        out_specs=[tile(d), pl.BlockSpec((tm * slabs, LANES), lambda b, i: (b * nt + i, 0)), row4, row4, row4,
                   pl.BlockSpec((N_EXPERTS, 1), lambda b, i: (0, 0))],
        out_shape=[jax.ShapeDtypeStruct((m, d), F32),
                   jax.ShapeDtypeStruct((m * slabs, LANES), U32),
                   jax.ShapeDtypeStruct((TOP_K, m), I32),
                   jax.ShapeDtypeStruct((TOP_K, m), F32),
                   jax.ShapeDtypeStruct((TOP_K, m), I32),
                   jax.ShapeDtypeStruct((N_EXPERTS, 1), F32)],
        scratch_shapes=[pltpu.VMEM((N_EXPERTS, 1), F32)],
        compiler_params=_params("arbitrary", "arbitrary"),
        name="mix_tail",
    )(att, hg, x, w_out, w_out, g_cross.reshape(1, d), kv, w_xq, w_xo, g_ffn.reshape(1, d), w_router_t,
      b_router.reshape(N_EXPERTS, 1), cnt0)


def _assign_rows_kernel(pstart_ref, idx_ref, rank_ref, dest_ref):
    idx = idx_ref[...]
    start = jnp.zeros_like(idx)
    for e in range(N_EXPERTS):
        start = jnp.where(idx == e, pstart_ref[e], start)
    dest_ref[...] = start + rank_ref[...]


def _assign_rows(pstart, idx, rank):
    full = pl.BlockSpec(idx.shape, lambda: (0, 0))
    return pl.pallas_call(
        _assign_rows_kernel,
        in_specs=[pl.BlockSpec(memory_space=pltpu.SMEM), full, full],
        out_specs=full,
        out_shape=jax.ShapeDtypeStruct(idx.shape, I32),
        name="assign_rows",
    )(pstart, idx, rank)


def _dispatch_kernel(seg_ref, dest_hbm, h_ref, xs_prev_hbm, xs_hbm, dsm, zrow, sem_idx, sem_row,
                     *, m_tokens, p_rows, fill_padding):
    del xs_prev_hbm
    i = pl.program_id(0)
    tt = SMEM_STAGE
    slabs = h_ref.shape[0] // tt

    def token_rows(ref, tok):
        return ref.at[pl.ds(pl.multiple_of(tok * slabs, slabs), slabs)]

    def idx_copy(k):
        return pltpu.make_async_copy(dest_hbm.at[pl.ds(k * m_tokens + i * tt, tt)],
                                     dsm.at[pl.ds(k * tt, tt)], sem_idx)

    for k in range(TOP_K):
        idx_copy(k).start()
    for k in range(TOP_K):
        idx_copy(k).wait()

    def row_copy(t, dst_row):
        return pltpu.make_async_copy(token_rows(h_ref, t), token_rows(xs_hbm, dst_row), sem_row)

    def issue(t, carry):
        for k in range(TOP_K):
            row_copy(t, dsm[k * tt + t]).start(priority=k % 2)
        return carry

    lax.fori_loop(0, tt, issue, 0, unroll=8)

    for k in range(TOP_K):
        pltpu.make_async_copy(h_ref, xs_hbm.at[pl.ds(0, tt * slabs)], sem_row).wait()

    @pl.when(jnp.logical_and(i == 0, jnp.bool_(fill_padding)))
    def _():
        zrow[...] = jnp.zeros_like(zrow)
        max_chunk = zrow.shape[0] // slabs

        def zero_copy(dst_row, nrows):
            return pltpu.make_async_copy(
                zrow.at[pl.ds(0, nrows * slabs)],
                xs_hbm.at[pl.ds(pl.multiple_of(dst_row * slabs, slabs), nrows * slabs)], sem_row)

        def zero_rows(dst_row, nrows):
            zero_copy(dst_row, nrows).start()
            zero_copy(dst_row, nrows).wait()

        def per_expert(e, carry):
            lo = seg_ref[0, e] + seg_ref[1, e]
            pad = seg_ref[2, e] - lo
            chunk = max_chunk
            while chunk >= 1:
                take = pad & chunk

                @pl.when(take != 0)
                def _():
                    zero_rows(lo, chunk)

                lo = lo + take
                chunk //= 2
            return carry

        lax.fori_loop(0, N_EXPERTS, per_expert, 0)

        tail_lo = seg_ref[2, N_EXPERTS - 1] // max_chunk
        tail_hi = p_rows // max_chunk

        def tail_start(c, carry):
            zero_copy(c * max_chunk, max_chunk).start()
            return carry

        def tail_wait(c, carry):
            zero_copy(c * max_chunk, max_chunk).wait()
            return carry

        lax.fori_loop(tail_lo, tail_hi, tail_start, 0)
        lax.fori_loop(tail_lo, tail_hi, tail_wait, 0)


def _dispatch(seg, dest_flat, h, xs_prev, m, p_rows, fill_padding):
    slabs = h.shape[0] // m
    return pl.pallas_call(
        functools.partial(_dispatch_kernel, m_tokens=m, p_rows=p_rows, fill_padding=fill_padding),
        grid=(m // SMEM_STAGE,),
        in_specs=[pl.BlockSpec(memory_space=pltpu.SMEM),
                  pl.BlockSpec(memory_space=pl.ANY),
                  pl.BlockSpec((SMEM_STAGE * slabs, LANES), lambda i: (i, 0)),
                  pl.BlockSpec(memory_space=pl.ANY)],
        out_specs=pl.BlockSpec(memory_space=pl.ANY),
        out_shape=jax.ShapeDtypeStruct((p_rows * slabs, LANES), h.dtype),
        input_output_aliases={3: 0},
        scratch_shapes=[pltpu.SMEM((TOP_K * SMEM_STAGE,), I32),
                        pltpu.VMEM((MOE_TILE // 2 * slabs, LANES), h.dtype),
                        pltpu.SemaphoreType.DMA,
                        pltpu.SemaphoreType.DMA],
        compiler_params=_params("arbitrary"),
        name="dispatch",
    )(seg, dest_flat, h, xs_prev)


def _expert_kernel(te_ref, tv_ref, nu_ref, x_ref, w1g_ref, w1l_ref, b1g_ref, b1l_ref, w2_ref, b2_ref,
                   o_ref, xb_ref):
    i = pl.program_id(0)
    f = pl.program_id(1)
    nf = pl.num_programs(1)
    live = tv_ref[i] > 0

    @pl.when(live)
    def _():
        @pl.when(f == 0)
        def _():
            o_ref[...] = jnp.broadcast_to(b2_ref[0], o_ref.shape)

        tmb, d = xb_ref.shape
        half = d // 2
        slabs = half // LANES
        for j in range(slabs):
            words = x_ref[pl.ds(j, tmb, stride=slabs), :]
            lo = slice(j * LANES, (j + 1) * LANES)
            hi = slice(half + j * LANES, half + (j + 1) * LANES)
            xb_ref[:, lo] = lax.bitcast_convert_type(words << 16, F32).astype(BF16)
            xb_ref[:, hi] = lax.bitcast_convert_type(words & jnp.uint32(0xFFFF0000), F32).astype(BF16)

        xb = xb_ref[...]
        glu = jnp.dot(xb, w1g_ref[0], preferred_element_type=F32) + b1g_ref[0]
        lin = jnp.dot(xb, w1l_ref[0], preferred_element_type=F32) + b1l_ref[0]
        glu = jnp.minimum(glu, SWIGLU_LIMIT)
        lin = jnp.clip(lin, -SWIGLU_LIMIT, SWIGLU_LIMIT)
        act = glu * _sigmoid(SWIGLU_ALPHA * glu) * (lin + 1.0)
        o_ref[...] += jnp.dot(act.astype(BF16), w2_ref[0], preferred_element_type=F32)

    @pl.when(jnp.logical_and(jnp.logical_not(live), f == nf - 1))
    def _():
        o_ref[...] = jnp.zeros_like(o_ref)


def _expert_mlp(tile_e, tile_valid, n_used, xs, w1, b1, w2, b2, tmb, tf):
    d = w1.shape[1]
    p_rows = xs.shape[0] // (d // 2 // LANES)
    d_ff = w2.shape[1]
    nf = d_ff // tf
    n_tiles = p_rows // tmb

    def fsel(i, f, tv):
        return jnp.where(tv[i] > 0, f, nf - 1)

    grid_spec = pltpu.PrefetchScalarGridSpec(
        num_scalar_prefetch=3,
        grid=(n_tiles, nf),
        in_specs=[
            pl.BlockSpec((tmb * (d // 2 // LANES), LANES), lambda i, f, te, tv, nu: (jnp.minimum(i, nu[0] - 1), 0)),
            pl.BlockSpec((1, d, tf), lambda i, f, te, tv, nu: (te[i], 0, fsel(i, f, tv))),
            pl.BlockSpec((1, d, tf), lambda i, f, te, tv, nu: (te[i], 0, nf + fsel(i, f, tv))),
            pl.BlockSpec((1, 1, tf), lambda i, f, te, tv, nu: (te[i], 0, fsel(i, f, tv))),
            pl.BlockSpec((1, 1, tf), lambda i, f, te, tv, nu: (te[i], 0, nf + fsel(i, f, tv))),
            pl.BlockSpec((1, tf, d), lambda i, f, te, tv, nu: (te[i], fsel(i, f, tv), 0)),
            pl.BlockSpec((1, 1, d), lambda i, f, te, tv, nu: (te[i], 0, 0)),
        ],
        out_specs=pl.BlockSpec((tmb, d), lambda i, f, te, tv, nu: (i, 0)),
        scratch_shapes=[pltpu.VMEM((tmb, d), BF16)],
    )
    return pl.pallas_call(
        _expert_kernel,
        grid_spec=grid_spec,
        out_shape=jax.ShapeDtypeStruct((p_rows, d), F32),
        compiler_params=_params("arbitrary", "arbitrary"),
        name="expert_mlp",
    )(tile_e, tile_valid, n_used, xs, w1, w1, b1, b1, w2, b2)


def _combine_kernel(dest_hbm, ys_hbm, x_ref, gate_ref, g_ref, o_ref, dsm, buf, sem_idx, sem_row, *, m_tokens, tc):
    i = pl.program_id(0)
    n = pl.num_programs(0)
    per_stage = SMEM_STAGE // tc

    def issue_tile(tile):
        slot = tile % 2
        off = (tile % per_stage) * tc

        @pl.when(tile % per_stage == 0)
        def _():
            base = (tile // per_stage) * SMEM_STAGE

            def idx_copy(k):
                return pltpu.make_async_copy(dest_hbm.at[pl.ds(k * m_tokens + base, SMEM_STAGE)],
                                             dsm.at[pl.ds(k * SMEM_STAGE, SMEM_STAGE)], sem_idx)

            for k in range(TOP_K):
                idx_copy(k).start()
            for k in range(TOP_K):
                idx_copy(k).wait()

        def issue(t8, carry):
            for u in range(SUBLANES):
                for k in range(TOP_K):
                    row = dsm[k * SMEM_STAGE + off + t8 * SUBLANES + u]
                    pltpu.make_async_copy(ys_hbm.at[pl.ds(row, 1)], buf.at[slot, k, t8, pl.ds(u, 1)],
                                          sem_row.at[slot]).start(priority=k % 2)
            return carry

        lax.fori_loop(0, tc // SUBLANES, issue, 0)

    @pl.when(i == 0)
    def _():
        issue_tile(i)

    @pl.when(i + 1 < n)
    def _():
        issue_tile(i + 1)

    slot = i % 2
    for k in range(TOP_K):
        pltpu.make_async_copy(buf.at[slot, k], buf.at[slot, k], sem_row.at[slot]).wait()

    gate = gate_ref[...]
    y = x_ref[...]
    for k in range(TOP_K):
        y = y + buf[slot, k].reshape(y.shape) * gate[:, k:k + 1]
    o_ref[...] = _rms(y, g_ref[...])


def _combine(dest_flat, ys, x, gate_t, g, tc):
    m, d = x.shape
    return pl.pallas_call(
        functools.partial(_combine_kernel, m_tokens=m, tc=tc),
        grid=(m // tc,),
        in_specs=[pl.BlockSpec(memory_space=pl.ANY),
                  pl.BlockSpec(memory_space=pl.ANY),
                  pl.BlockSpec((tc, d), lambda i: (i, 0)),
                  pl.BlockSpec((tc, TOP_K), lambda i: (i, 0)),
                  pl.BlockSpec((1, d), lambda i: (0, 0))],
        out_specs=pl.BlockSpec((tc, d), lambda i: (i, 0)),
        out_shape=jax.ShapeDtypeStruct((m, d), F32),
        scratch_shapes=[pltpu.SMEM((TOP_K * SMEM_STAGE,), I32),
                        pltpu.VMEM((2, TOP_K, tc // SUBLANES, SUBLANES, d), F32),
                        pltpu.SemaphoreType.DMA,
                        pltpu.SemaphoreType.DMA((2,))],
        compiler_params=_params("arbitrary"),
        name="combine",
    )(dest_flat, ys, x, gate_t, g.reshape(1, d))


def _rope_tables(seq):
    half = ATT_HEAD_DIM // 2
    inv = ROPE_THETA ** (-jnp.arange(0, ATT_HEAD_DIM, 2, dtype=F32) / ATT_HEAD_DIM)
    ang = jnp.arange(seq, dtype=F32)[:, None] * inv[None, :]
    cos = jnp.tile(jnp.cos(ang), (1, LANES // half))
    sin = jnp.sin(ang)
    sin_signed = jnp.tile(jnp.concatenate([-sin, sin], axis=-1), (1, LANES // ATT_HEAD_DIM))
    return cos, sin_signed


def _pick(n, prefs):
    for p in prefs:
        if n % p == 0:
            return p
    return n


def _uninitialized_kernel(o_ref):
    del o_ref


def _uninitialized(shape, dtype):
    return pl.pallas_call(
        _uninitialized_kernel,
        out_specs=pl.BlockSpec(memory_space=pl.ANY),
        out_shape=jax.ShapeDtypeStruct(shape, dtype),
        name="uninitialized",
    )()


def _trunk_front(x, mem, wts, cnt0):
    batch, seq, d = x.shape
    n_mem = mem.shape[1]
    m = batch * seq
    x2d = x.reshape(m, d)

    n_in = wts["w_in"].shape[1]
    z = _norm_matmul(x2d, wts["norm_mix"], wts["w_in"], _pick(m, (1024, 512, 256, 128)),
                     _pick(n_in, (1664, 1024, 512, 256, 128)), F32, "in_projection")

    cos, sin_signed = _rope_tables(seq)
    att = _window_attention(z, wts["att_sink"], cos, sin_signed, batch, seq, _pick(seq, (256, 128)))
    hg = _hgrn2(z, wts["lb"], wts["hg_norm"], batch, seq, _pick(seq, (256, 128, 64)))
    kv = _norm_matmul(mem.reshape(batch * n_mem, d), wts["norm_mem"], wts["w_xkv"], n_mem, 2 * X_WIDTH, BF16,
                      "memory_kv")
    return _mix_tail(att, hg, x2d, wts["w_out"], wts["norm_cross"], kv, wts["w_xq"], wts["w_xo"], wts["norm_ffn"],
                     wts["w_router_t"], wts["b_router"], cnt0, batch, seq, n_mem, _pick(seq, (512, 256, 128)))


def _moe_and_final_norm(fronts, shapes, wts, moe_tile, ff_tile):
    counts = fronts[-1][5][:, 0].astype(I32)
    padded = (counts + moe_tile - 1) // moe_tile * moe_tile
    pend = jnp.cumsum(padded)
    pstart = pend - padded
    m_total = sum(f[0].shape[0] for f in fronts)
    n_tiles = -(-(m_total * TOP_K) // moe_tile) + N_EXPERTS
    p_rows = n_tiles * moe_tile
    tile_row = jnp.arange(n_tiles, dtype=I32) * moe_tile
    tile_e = jnp.minimum(jnp.sum((pend[None, :] <= tile_row[:, None]).astype(I32), axis=1), N_EXPERTS - 1)
    owner = tile_e[:, None] == jnp.arange(N_EXPERTS, dtype=I32)[None, :]
    seg_end = jnp.sum(jnp.where(owner, (pstart + counts)[None, :], 0), axis=1)
    tile_valid = jnp.clip(seg_end - tile_row, 0, moe_tile).astype(I32)
    n_used = (pend[-1:] // moe_tile).astype(I32)
    seg = jnp.stack([pstart, counts, pend]).astype(I32)

    h_rows, lanes = fronts[0][1].shape
    slabs = h_rows // fronts[0][0].shape[0]
    xs = _uninitialized((p_rows * slabs, lanes), fronts[0][1].dtype)
    dests = []
    for n, (x2, h3, idx, gate, rank, cnt) in enumerate(fronts):
        dest_flat = _assign_rows(pstart.astype(I32), idx, rank).reshape(-1)
        dests.append(dest_flat)
        xs = _dispatch(seg, dest_flat, h3, xs, x2.shape[0], p_rows, fill_padding=(n == 0))
    ys = _expert_mlp(tile_e, tile_valid, n_used, xs, wts["w_moe1"], wts["b_moe1"], wts["w_moe2"], wts["b_moe2"],
                     moe_tile, ff_tile)
    outs = []
    for (x2, h3, idx, gate, rank, cnt), dest_flat, shape in zip(fronts, dests, shapes):
        y = _combine(dest_flat, ys, x2, gate.T, wts["norm_final"], _pick(x2.shape[0], (256, 128)))
        outs.append(y.reshape(shape))
    return tuple(outs)


def kernel(x_prompt, x_sample, mem_prompt, mem_sample, norm_mix, w_in, att_sink, hg_lb_logits, hg_norm, w_out,
           norm_cross, norm_mem, w_xq, w_xkv, w_xo, norm_ffn, w_router, b_router, w_moe1, b_moe1, w_moe2, b_moe2,
           norm_final):
    assert w_in.shape[0] == 1, "the final norm is fused after the single layer"
    lb_all = jnp.cumsum(jax.nn.softmax(hg_lb_logits.astype(F32), axis=0), axis=0)
    d_ff = w_moe2.shape[2]
    wts = dict(
        norm_mix=norm_mix[0], w_in=w_in[0].astype(BF16), att_sink=att_sink[0], lb=lb_all[0],
        hg_norm=hg_norm[0], w_out=w_out[0].astype(BF16), norm_cross=norm_cross[0], norm_mem=norm_mem[0],
        w_xq=w_xq[0].astype(BF16), w_xkv=w_xkv[0].astype(BF16), w_xo=w_xo[0].astype(BF16),
        norm_ffn=norm_ffn[0], w_router_t=w_router[0].T, b_router=b_router[0],
        w_moe1=w_moe1[0].astype(BF16), b_moe1=b_moe1[0][:, None, :],
        w_moe2=w_moe2[0].astype(BF16), b_moe2=b_moe2[0][:, None, :],
        norm_final=norm_final,
    )
    ff_tile = _pick(d_ff, (1024, 512, 256, 128))
    fronts = []
    cnt = jnp.zeros((N_EXPERTS, 1), F32)
    for x, mem in ((x_prompt, mem_prompt), (x_sample, mem_sample)):
        front = _trunk_front(x, mem, wts, cnt)
        cnt = front[5]
        fronts.append(front)
    return _moe_and_final_norm(fronts, (x_prompt.shape, x_sample.shape), wts, MOE_TILE, ff_tile)
```
